```python
import jax, jax.numpy as jnp
from jax import lax
import numpy as np

D_MODEL = 1024
BATCH = 8
SEQ = 4096
DEPTH = 1

HEAD_DIM = 64
N_FOX_HEADS = 8
N_MOBA_HEADS = 8
FOX_WIDTH = N_FOX_HEADS * HEAD_DIM
MOBA_WIDTH = N_MOBA_HEADS * HEAD_DIM
MIX_WIDTH = FOX_WIDTH + MOBA_WIDTH
IN_COLS = 3 * FOX_WIDTH + N_FOX_HEADS + 3 * MOBA_WIDTH
FOX_Q_BLOCK = 128
MOBA_BLOCK = 256
MOBA_TOPK = 3
MOBA_Q_CHUNK = 16
ROPE_THETA = 500000.0
ROPE_DIM = HEAD_DIM // 4
N_GROUPS = 4
EXPERTS_PER_GROUP = 8
N_EXPERTS = N_GROUPS * EXPERTS_PER_GROUP
TOPK_IN_GROUP = 2
D_EXPERT = 256
EPS = 1e-6
F32 = jnp.float32

kernel_name = "hymba_fox_moba_hmoe_layer"


def rms_norm(x, g):
    xf = x.astype(F32)
    y = xf * lax.rsqrt(jnp.mean(xf * xf, axis=-1, keepdims=True) + EPS)
    return (y * g.astype(F32)).astype(x.dtype)


def to_heads(t, n_heads):
    b, s, _ = t.shape
    return t.reshape(b, s, n_heads, HEAD_DIM).transpose(0, 2, 1, 3)


def from_heads(t):
    b, h, s, d = t.shape
    return t.transpose(0, 2, 1, 3).reshape(b, s, h * d)


def partial_rotary(t):
    seq = t.shape[2]
    half = ROPE_DIM // 2
    inv_freq = ROPE_THETA ** (-jnp.arange(half, dtype=F32) / half)
    ang = jnp.arange(seq, dtype=F32)[:, None] * inv_freq[None, :]
    cos, sin = jnp.cos(ang), jnp.sin(ang)
    tr = t[..., :ROPE_DIM].astype(F32)
    t1, t2 = tr[..., :half], tr[..., half:]
    rot = jnp.concatenate([t1 * cos - t2 * sin, t2 * cos + t1 * sin], axis=-1).astype(t.dtype)
    return jnp.concatenate([rot, t[..., ROPE_DIM:]], axis=-1)


def forgetting_attention(q, k, v, log_f):
    b, h, s, d = q.shape
    nb = s // FOX_Q_BLOCK
    c = jnp.cumsum(log_f, axis=-1)
    q_blocks = q.reshape(b, h, nb, FOX_Q_BLOCK, d).transpose(2, 0, 1, 3, 4)
    c_blocks = c.reshape(b, h, nb, FOX_Q_BLOCK).transpose(2, 0, 1, 3)
    key_pos = jnp.arange(s)
    scale = HEAD_DIM ** -0.5

    def one_block(args):
        i, qb, cb = args
        sc = jnp.einsum('bhqd,bhkd->bhqk', qb, k, preferred_element_type=F32) * scale
        sc = sc + cb[..., :, None] - c[:, :, None, :]
        q_pos = i * FOX_Q_BLOCK + jnp.arange(FOX_Q_BLOCK)
        sc = jnp.where(key_pos[None, :] <= q_pos[:, None], sc, -jnp.inf)
        p = jax.nn.softmax(sc, axis=-1)
        return jnp.einsum('bhqk,bhkd->bhqd', p.astype(v.dtype), v)

    out = lax.map(one_block, (jnp.arange(nb), q_blocks, c_blocks))
    return out.transpose(1, 2, 0, 3, 4).reshape(b, h, s, d)


def moba_attention(q, k, v):
    b, h, s, d = q.shape
    nblk = -(-s // MOBA_BLOCK)
    pad = nblk * MOBA_BLOCK - s
    k_blk = jnp.pad(k, ((0, 0), (0, 0), (0, pad), (0, 0))).reshape(b, h, nblk, MOBA_BLOCK, d)
    v_blk = jnp.pad(v, ((0, 0), (0, 0), (0, pad), (0, 0))).reshape(b, h, nblk, MOBA_BLOCK, d)
    k_mean = jnp.mean(k_blk.astype(F32), axis=3)
    topk = min(MOBA_TOPK, nblk)
    nq = s // MOBA_Q_CHUNK
    q_chunks = q.reshape(b, h, nq, MOBA_Q_CHUNK, d).transpose(2, 0, 1, 3, 4)
    b_idx = jnp.arange(b)[:, None, None, None]
    h_idx = jnp.arange(h)[None, :, None, None]
    blk_ids = jnp.arange(nblk)
    scale = HEAD_DIM ** -0.5

    def one_chunk(args):
        i, qc = args
        q_pos = i * MOBA_Q_CHUNK + jnp.arange(MOBA_Q_CHUNK)
        own = q_pos[0] // MOBA_BLOCK
        gate = jnp.einsum('bhqd,bhnd->bhqn', qc.astype(F32), k_mean)
        gate = jnp.where(blk_ids < own, gate, -jnp.inf)
        _, sel = lax.top_k(gate, topk)
        sel_valid = sel < own
        k_sel = k_blk[b_idx, h_idx, sel]
        v_sel = v_blk[b_idx, h_idx, sel]
        s_sel = jnp.einsum('bhqd,bhqjkd->bhqjk', qc, k_sel, preferred_element_type=F32) * scale
        s_sel = jnp.where(sel_valid[..., None], s_sel, -jnp.inf)
        s_sel = s_sel.reshape(b, h, MOBA_Q_CHUNK, topk * MOBA_BLOCK)
        k_own = lax.dynamic_index_in_dim(k_blk, own, axis=2, keepdims=False)
        v_own = lax.dynamic_index_in_dim(v_blk, own, axis=2, keepdims=False)
        own_pos = own * MOBA_BLOCK + jnp.arange(MOBA_BLOCK)
        s_own = jnp.einsum('bhqd,bhkd->bhqk', qc, k_own, preferred_element_type=F32) * scale
        s_own = jnp.where(own_pos[None, :] <= q_pos[:, None], s_own, -jnp.inf)
        p = jax.nn.softmax(jnp.concatenate([s_sel, s_own], axis=-1), axis=-1)
        p_sel = p[..., :topk * MOBA_BLOCK].reshape(b, h, MOBA_Q_CHUNK, topk, MOBA_BLOCK)
        p_own = p[..., topk * MOBA_BLOCK:]
        return (jnp.einsum('bhqjk,bhqjkd->bhqd', p_sel.astype(v.dtype), v_sel)
                + jnp.einsum('bhqk,bhkd->bhqd', p_own.astype(v.dtype), v_own))

    out = lax.map(one_chunk, (jnp.arange(nq), q_chunks))
    return out.transpose(1, 2, 0, 3, 4).reshape(b, h, s, d)


def hierarchical_moe(h, w_rg, b_rg, w_re, b_re, w_gate, w_up, w_down):
    g_prob = jax.nn.softmax(jnp.einsum('bsd,dg->bsg', h, w_rg, preferred_element_type=F32)
                            + b_rg.astype(F32), axis=-1)
    g_top, g_idx = lax.top_k(g_prob, 1)
    e_logits = jnp.einsum('bsd,dge->bsge', h, w_re, preferred_element_type=F32) + b_re.astype(F32)
    e_logits = jnp.take_along_axis(e_logits, g_idx[..., None], axis=2)[..., 0, :]
    e_prob = jax.nn.softmax(e_logits, axis=-1)
    e_top, e_idx = lax.top_k(e_prob, TOPK_IN_GROUP)
    e_w = e_top / jnp.sum(e_top, axis=-1, keepdims=True) * g_top
    flat_idx = g_idx * EXPERTS_PER_GROUP + e_idx
    combine = jnp.sum(jax.nn.one_hot(flat_idx, N_EXPERTS, dtype=F32) * e_w[..., None], axis=-2)

    def per_sequence(args):
        hs, cs = args
        a = jnp.einsum('sd,edf->sef', hs, w_gate)
        u = jnp.einsum('sd,edf->sef', hs, w_up)
        act = jax.nn.silu(a) * u * cs[..., None].astype(hs.dtype)
        return jnp.einsum('sef,efd->sd', act, w_down)

    return lax.map(per_sequence, (h, combine))


def setup_inputs(seed: int = 0) -> dict:
    key = jax.random.key(seed)
    ks = jax.random.split(key, 16)
    nrm = jax.random.normal
    return {
        "x": nrm(ks[0], (BATCH, SEQ, D_MODEL), F32),
        "norm_mix_g": 1.0 + 0.02 * nrm(ks[1], (DEPTH, D_MODEL), F32),
        "w_in": nrm(ks[2], (DEPTH, D_MODEL, IN_COLS), F32) * D_MODEL ** -0.5,
        "b_forget": 3.0 + 0.5 * nrm(ks[3], (DEPTH, N_FOX_HEADS), F32),
        "fox_out_g": 1.0 + 0.02 * nrm(ks[4], (DEPTH, FOX_WIDTH), F32),
        "moba_out_g": 1.0 + 0.02 * nrm(ks[5], (DEPTH, MOBA_WIDTH), F32),
        "w_out": nrm(ks[6], (DEPTH, MIX_WIDTH, D_MODEL), F32) * MIX_WIDTH ** -0.5,
        "norm_ffn_g": 1.0 + 0.02 * nrm(ks[7], (DEPTH, D_MODEL), F32),
        "w_router_group": nrm(ks[8], (DEPTH, D_MODEL, N_GROUPS), F32) * D_MODEL ** -0.5,
        "b_router_group": 0.01 * nrm(ks[9], (DEPTH, N_GROUPS), F32),
        "w_router_expert": nrm(ks[10], (DEPTH, D_MODEL, N_GROUPS, EXPERTS_PER_GROUP), F32) * D_MODEL ** -0.5,
        "b_router_expert": 0.01 * nrm(ks[11], (DEPTH, N_GROUPS, EXPERTS_PER_GROUP), F32),
        "w_gate": nrm(ks[12], (DEPTH, N_EXPERTS, D_MODEL, D_EXPERT), F32) * D_MODEL ** -0.5,
        "w_up": nrm(ks[13], (DEPTH, N_EXPERTS, D_MODEL, D_EXPERT), F32) * D_MODEL ** -0.5,
        "w_down": nrm(ks[14], (DEPTH, N_EXPERTS, D_EXPERT, D_MODEL), F32) * D_EXPERT ** -0.5,
        "norm_final_g": 1.0 + 0.02 * nrm(ks[15], (D_MODEL,), F32),
    }


def reference(x, norm_mix_g, w_in, b_forget, fox_out_g, moba_out_g, w_out, norm_ffn_g,
              w_router_group, b_router_group, w_router_expert, b_router_expert,
              w_gate, w_up, w_down, norm_final_g):
    splits = [FOX_WIDTH, 2 * FOX_WIDTH, 3 * FOX_WIDTH, 3 * FOX_WIDTH + N_FOX_HEADS,
              3 * FOX_WIDTH + N_FOX_HEADS + MOBA_WIDTH,
              3 * FOX_WIDTH + N_FOX_HEADS + 2 * MOBA_WIDTH]
    for l in range(DEPTH):
        h = rms_norm(x, norm_mix_g[l])
        proj = jnp.einsum('bsd,dc->bsc', h, w_in[l])
        fq, fk, fv, f_logit, mq, mk, mv = jnp.split(proj, splits, axis=-1)
        log_f = jax.nn.log_sigmoid(f_logit.astype(F32) + b_forget[l].astype(F32)).transpose(0, 2, 1)
        fox = forgetting_attention(to_heads(fq, N_FOX_HEADS), to_heads(fk, N_FOX_HEADS),
                                   to_heads(fv, N_FOX_HEADS), log_f)
        moba = moba_attention(partial_rotary(to_heads(mq, N_MOBA_HEADS)),
                              partial_rotary(to_heads(mk, N_MOBA_HEADS)),
                              to_heads(mv, N_MOBA_HEADS))
        mixed = jnp.concatenate([rms_norm(from_heads(fox), fox_out_g[l]),
                                 rms_norm(from_heads(moba), moba_out_g[l])], axis=-1)
        x = x + jnp.einsum('bsc,cd->bsd', mixed, w_out[l])
        h = rms_norm(x, norm_ffn_g[l])
        x = x + hierarchical_moe(h, w_router_group[l], b_router_group[l], w_router_expert[l],
                                 b_router_expert[l], w_gate[l], w_up[l], w_down[l])
    return rms_norm(x, norm_final_g)
```

```python
import functools

import jax
import jax.numpy as jnp
from jax import lax
from jax.experimental import pallas as pl
from jax.experimental.pallas import tpu as pltpu

F32 = jnp.float32
BF16 = jnp.bfloat16
I32 = jnp.int32

HEAD_DIM = 64
N_FOX_HEADS = 8
N_MOBA_HEADS = 8
FOX_WIDTH = N_FOX_HEADS * HEAD_DIM
MOBA_WIDTH = N_MOBA_HEADS * HEAD_DIM
MOBA_BLOCK = 256
MOBA_TOPK = 3
ROPE_THETA = 500000.0
ROPE_DIM = HEAD_DIM // 4
N_GROUPS = 4
EXPERTS_PER_GROUP = 8
N_EXPERTS = N_GROUPS * EXPERTS_PER_GROUP
EPS = 1e-6

LANES = 128
SUBLANES = 8
HEADS_PER_BLOCK = LANES // HEAD_DIM
VMEM_LIMIT = 56 * 1024 * 1024

TM_PROJ = 512
TQ = 256
TM_EXPERT = 256
TM_ROWS = 256

NEG_INF = float("-inf")


def _params(sem):
    return pltpu.CompilerParams(dimension_semantics=sem, vmem_limit_bytes=VMEM_LIMIT)


def _rms(x, g):
    return x * lax.rsqrt(jnp.mean(x * x, axis=-1, keepdims=True) + EPS) * g


def _split3(x):
    hi = x.astype(BF16)
    r = x - hi.astype(F32)
    mid = r.astype(BF16)
    lo = (r - mid.astype(F32)).astype(BF16)
    return hi, mid, lo


def _dot(a, b):
    return jnp.dot(a, b, preferred_element_type=F32)


def _dot_nt(a, b):
    return lax.dot_general(a, b, (((1,), (1,)), ((), ())), preferred_element_type=F32)


def _inproj_kernel(x_ref, g_ref, w_ref, wl_ref, bf_ref, cosm_ref, sina_ref, sinb_ref,
                   fq_ref, fk_ref, fv_ref, mq_ref, mk_ref, mv_ref, c_ref, kmean_ref, carry_ref):
    j = pl.program_id(1)
    tm = x_ref.shape[1]
    h = _rms(x_ref[0], g_ref[...]).astype(BF16)
    scale = HEAD_DIM ** -0.5

    def proj(seg):
        return _dot(h, w_ref[:, seg * FOX_WIDTH:(seg + 1) * FOX_WIDTH])

    fq_ref[0] = (proj(0) * scale).astype(BF16)
    fk_ref[0] = proj(1).astype(BF16)
    fv_ref[0] = proj(2).astype(BF16)
    mv_ref[0] = proj(5).astype(BF16)

    cosm, sina, sinb = cosm_ref[...], sina_ref[...], sinb_ref[...]

    def rotary(t):
        outs = []
        for g in range(MOBA_WIDTH // LANES):
            tg = t[:, g * LANES:(g + 1) * LANES]
            outs.append(tg * cosm + pltpu.roll(tg, LANES - ROPE_DIM // 2, 1) * sina
                        + pltpu.roll(tg, ROPE_DIM // 2, 1) * sinb)
        return jnp.concatenate(outs, axis=1)

    mq_ref[0] = (rotary(proj(3)) * scale).astype(BF16)
    mk = rotary(proj(4))
    mk_ref[0] = mk.astype(BF16)
    nblk_tile = tm // MOBA_BLOCK
    means = [jnp.mean(mk[r * MOBA_BLOCK:(r + 1) * MOBA_BLOCK], axis=0, keepdims=True)
             for r in range(nblk_tile)]
    means += [jnp.zeros_like(means[0])] * (kmean_ref.shape[2] - nblk_tile)
    kmean_ref[0, 0] = jnp.concatenate(means, axis=0)

    z = _dot(h, wl_ref[...]) + bf_ref[...]
    log_f = jnp.minimum(z, 0.0) - jnp.log1p(jnp.exp(-jnp.abs(z)))
    row = lax.broadcasted_iota(I32, (tm, tm), 0)
    col = lax.broadcasted_iota(I32, (tm, tm), 1)
    tri = jnp.where(row >= col, 1.0, 0.0).astype(BF16)
    hi, mid, lo = _split3(log_f)
    local = _dot(tri, hi) + _dot(tri, mid) + _dot(tri, lo)

    @pl.when(j == 0)
    def _():
        carry_ref[...] = jnp.zeros_like(carry_ref)

    c = local + carry_ref[...]
    carry_ref[...] = c[tm - 1:tm, :]
    c_ref[0] = c[:, :N_FOX_HEADS]


def _inproj(x, g, w_main, w_logit, b_logit, cosm, sina, sinb):
    b, s, d = x.shape
    tm = TM_PROJ
    act =jax.ShapeDtypeStruct((b, s, FOX_WIDTH), BF16)
    out_shape = [act] * 6 + [jax.ShapeDtypeStruct((b, s, N_FOX_HEADS), F32),
                             jax.ShapeDtypeStruct((b, s // tm, SUBLANES, MOBA_WIDTH), F32)]
    act_spec = pl.BlockSpec((1, tm, FOX_WIDTH), lambda bi, j: (bi, j, 0))
    tab_spec = pl.BlockSpec((tm, LANES), lambda bi, j: (j, 0))
    const = lambda bi, j: (0, 0)
    *acts, c, kmean = pl.pallas_call(
        _inproj_kernel,
        grid=(b, s // tm),
        in_specs=[pl.BlockSpec((1, tm, d), lambda bi, j: (bi, j, 0)),
                  pl.BlockSpec((1, d), const),
                  pl.BlockSpec(w_main.shape, const),
                  pl.BlockSpec(w_logit.shape, const),
                  pl.BlockSpec((1, LANES), const),
                  tab_spec, tab_spec, tab_spec],
        out_specs=[act_spec] * 6 + [pl.BlockSpec((1, tm, N_FOX_HEADS), lambda bi, j: (bi, j, 0)),
                                    pl.BlockSpec((1, 1, SUBLANES, MOBA_WIDTH),
                                                 lambda bi, j: (bi, j, 0, 0))],
        out_shape=out_shape,
        scratch_shapes=[pltpu.VMEM((1, LANES), F32)],
        compiler_params=_params(("arbitrary", "arbitrary")),
        name="inproj",
    )(x, g, w_main, w_logit, b_logit, cosm, sina, sinb)
    kmean = kmean[:, :, :tm // MOBA_BLOCK].reshape(b, s // MOBA_BLOCK, MOBA_WIDTH)
    return (*acts, c, kmean)


def _softmax_step(s, hh, m_ref, l_ref):
    m_prev = m_ref[hh]
    m_new = jnp.maximum(m_prev, jnp.max(s, axis=1, keepdims=True))
    alpha = jnp.exp(m_prev - m_new)
    p = jnp.exp(s - m_new)
    l_ref[hh] = alpha * l_ref[hh] + jnp.sum(p, axis=1, keepdims=True)
    m_ref[hh] = m_new
    return alpha, p


def _attn_init(m_ref, l_ref, acc_ref):
    m_ref[...] = jnp.full(m_ref.shape, NEG_INF, F32)
    l_ref[...] = jnp.zeros_like(l_ref)
    acc_ref[...] = jnp.zeros_like(acc_ref)


def _fox_kernel(q_ref, k_ref, v_ref, ccol_ref, crow_ref, o_ref, m_ref, l_ref, acc_ref):
    i = pl.program_id(2)
    tq = q_ref.shape[1]
    q = q_ref[0]
    head0 = lax.broadcasted_iota(I32, (tq, LANES), 1) < HEAD_DIM
    zero = jnp.zeros_like(q)
    qs = (jnp.where(head0, q, zero), jnp.where(head0, zero, q))
    ct = ccol_ref[0, 0]
    causal = (lax.broadcasted_iota(I32, (tq, tq), 1) <= lax.broadcasted_iota(I32, (tq, tq), 0))
    _attn_init(m_ref, l_ref, acc_ref)

    def tile(kt, diag):
        ks = pl.multiple_of(kt * tq, tq)
        k = k_ref[0, pl.ds(ks, tq), :]
        v = v_ref[0, pl.ds(ks, tq), :]
        cs = crow_ref[0, 0, kt]
        alphas, pvs = [], []
        for hh in range(HEADS_PER_BLOCK):
            s = _dot_nt(qs[hh], k) + (ct[:, hh:hh + 1] - cs[hh:hh + 1, :])
            if diag:
                s = jnp.where(causal, s, NEG_INF)
            alpha, p = _softmax_step(s, hh, m_ref, l_ref)
            alphas.append(alpha)
            pvs.append(_dot(p.astype(BF16), v))
        acc_ref[...] = (acc_ref[...] * jnp.where(head0, alphas[0], alphas[1])
                        + jnp.where(head0, pvs[0], pvs[1]))

    tile(i, True)
    lax.fori_loop(0, i, lambda kt, c: (tile(kt, False), c)[1], 0)
    o_ref[0] = acc_ref[...] / jnp.where(head0, l_ref[0], l_ref[1])


def _moba_kernel(q_ref, k_ref, v_ref, kmean_ref, o_ref, m_ref, l_ref, acc_ref, bias_ref):
    i = pl.program_id(2)
    tq = q_ref.shape[1]
    nblk = kmean_ref.shape[1]
    q = q_ref[0]
    head0 = lax.broadcasted_iota(I32, (tq, LANES), 1) < HEAD_DIM
    zero = jnp.zeros_like(q)
    qs = (jnp.where(head0, q, zero), jnp.where(head0, zero, q))
    causal = (lax.broadcasted_iota(I32, (tq, tq), 1) <= lax.broadcasted_iota(I32, (tq, tq), 0))
    _attn_init(m_ref, l_ref, acc_ref)

    km_parts = _split3(kmean_ref[0])
    blk = lax.broadcasted_iota(I32, (tq, nblk), 1).astype(F32)
    past = blk < i.astype(F32)
    for hh in range(HEADS_PER_BLOCK):
        gate = sum(_dot_nt(qs[hh], part) for part in km_parts)
        sel = jnp.zeros((tq, nblk), jnp.bool_)
        for _ in range(MOBA_TOPK):
            remaining = jnp.logical_and(past, jnp.logical_not(sel))
            g = jnp.where(remaining, gate, NEG_INF)
            first = jnp.min(jnp.where(g == jnp.max(g, axis=1, keepdims=True), blk, float(nblk)),
                            axis=1, keepdims=True)
            sel = jnp.logical_or(sel, jnp.logical_and(blk == first, remaining))
        bias_ref[hh] = jnp.where(sel, 0.0, NEG_INF)

    def tile(kt, diag):
        ks = pl.multiple_of(kt * tq, tq)
        k = k_ref[0, pl.ds(ks, tq), :]
        v = v_ref[0, pl.ds(ks, tq), :]
        alphas, pvs = [], []
        for hh in range(HEADS_PER_BLOCK):
            s = _dot_nt(qs[hh], k)
            if diag:
                s = jnp.where(causal, s, NEG_INF)
            else:
                s = s + jnp.max(jnp.where(blk == kt.astype(F32), bias_ref[hh], NEG_INF),
                                axis=1, keepdims=True)
            alpha, p = _softmax_step(s, hh, m_ref, l_ref)
            alphas.append(alpha)
            pvs.append(_dot(p.astype(BF16), v))
        acc_ref[...] = (acc_ref[...] * jnp.where(head0, alphas[0], alphas[1])
                        + jnp.where(head0, pvs[0], pvs[1]))

    tile(i, True)
    lax.fori_loop(0, i, lambda kt, c: (tile(kt, False), c)[1], 0)
    o_ref[0] = acc_ref[...] / jnp.where(head0, l_ref[0], l_ref[1])


def _attn_scratch(tq):
    return [pltpu.VMEM((HEADS_PER_BLOCK, tq, 1), F32), pltpu.VMEM((HEADS_PER_BLOCK, tq, 1), F32),
            pltpu.VMEM((tq, LANES), F32)]


def _fox(q, k, v, c):
    b, s, width = q.shape
    tq = TQ
    nq, nhb = s // tq, width // LANES
    c4 = c.reshape(b, s, nhb, HEADS_PER_BLOCK)
    ccol = c4.transpose(0, 2, 1, 3)
    crow = c4.reshape(b, nq, tq, nhb, HEADS_PER_BLOCK).transpose(0, 3, 1, 4, 2)
    q_spec = pl.BlockSpec((1, tq, LANES), lambda bi, hb, i: (bi, i, hb))
    kv_spec = pl.BlockSpec((1, s, LANES), lambda bi, hb, i: (bi, 0, hb))
    return pl.pallas_call(
        _fox_kernel,
        grid=(b, nhb, nq),
        in_specs=[q_spec, kv_spec, kv_spec,
                  pl.BlockSpec((1, 1, tq, HEADS_PER_BLOCK), lambda bi, hb, i: (bi, hb, i, 0)),
                  pl.BlockSpec((1, 1, nq, HEADS_PER_BLOCK, tq), lambda bi, hb, i: (bi, hb, 0, 0, 0))],
        out_specs=q_spec,
        out_shape=jax.ShapeDtypeStruct((b, s, width), F32),
        scratch_shapes=_attn_scratch(tq),
        compiler_params=_params(("arbitrary", "arbitrary", "arbitrary")),
        name="fox",
    )(q, k, v, ccol, crow)


def _moba(q, k, v, kmean):
    b, s, width = q.shape
    tq = TQ
    nq, nhb = s // tq, width // LANES
    nblk = kmean.shape[1]
    q_spec = pl.BlockSpec((1, tq, LANES), lambda bi, hb, i: (bi, i, hb))
    kv_spec = pl.BlockSpec((1, s, LANES), lambda bi, hb, i: (bi, 0, hb))
    return pl.pallas_call(
        _moba_kernel,
        grid=(b, nhb, nq),
        in_specs=[q_spec, kv_spec, kv_spec,
                  pl.BlockSpec((1, nblk, LANES), lambda bi, hb, i: (bi, 0, hb))],
        out_specs=q_spec,
        out_shape=jax.ShapeDtypeStruct((b, s, width), F32),
        scratch_shapes=_attn_scratch(tq) + [pltpu.VMEM((HEADS_PER_BLOCK, tq, nblk), F32)],
        compiler_params=_params(("arbitrary", "arbitrary", "arbitrary")),
        name="moba",
    )(q, k, v, kmean)


def _postattn_kernel(x_ref, fox_ref, moba_ref, gf_ref, gm_ref, wo_ref, gn_ref, wr_ref, br_ref,
                     x2_ref, h2_ref, idx_ref, wts_ref, rank_ref, cnt_ref, carry_ref):
    t = pl.program_id(0)
    tm = x_ref.shape[0]
    fw = fox_ref.shape[1]
    mixed_f = _rms(fox_ref[...], gf_ref[...]).astype(BF16)
    mixed_m = _rms(moba_ref[...], gm_ref[...]).astype(BF16)
    x2 = x_ref[...] + _dot(mixed_f, wo_ref[:fw, :]) + _dot(mixed_m, wo_ref[fw:, :])
    x2_ref[...] = x2
    h2 = _rms(x2, gn_ref[...])
    h2_ref[...] = h2

    h_hi = h2.astype(BF16)
    h_lo = (h2 - h_hi.astype(F32)).astype(BF16)
    wr = wr_ref[...]
    w_hi = wr.astype(BF16)
    w_lo = (wr - w_hi.astype(F32)).astype(BF16)
    logits = _dot(h_hi, w_hi) + _dot(h_hi, w_lo) + _dot(h_lo, w_hi) + br_ref[...]
    lane = lax.broadcasted_iota(I32, (tm, LANES), 1).astype(F32)

    def first_max(vals):
        mx = jnp.max(vals, axis=1, keepdims=True)
        return mx, jnp.min(jnp.where(vals == mx, lane, float(LANES)), axis=1, keepdims=True)

    gl = jnp.where(lane < N_GROUPS, logits, NEG_INF)
    gmax, g_idx = first_max(gl)
    g_top = 1.0 / jnp.sum(jnp.exp(gl - gmax), axis=1, keepdims=True)
    e_lo = N_GROUPS + EXPERTS_PER_GROUP * g_idx
    el = jnp.where(jnp.logical_and(lane >= e_lo, lane < e_lo + EXPERTS_PER_GROUP), logits, NEG_INF)
    emax, i1 = first_max(el)
    esum = jnp.sum(jnp.exp(el - emax), axis=1, keepdims=True)
    e2max, i2 = first_max(jnp.where(lane == i1, NEG_INF, el))
    p1 = 1.0 / esum
    p2 = jnp.exp(e2max - emax) / esum
    w1 = p1 / (p1 + p2) * g_top
    w2 = p2 / (p1 + p2) * g_top
    e1 = i1 - N_GROUPS
    e2 = i2 - N_GROUPS
    idx_ref[...] = jnp.where(lane == 0.0, e1, e2)[:, :2].astype(I32)
    wts_ref[...] = jnp.where(lane == 0.0, w1, w2)[:, :2]

    @pl.when(t == 0)
    def _():
        carry_ref[...] = jnp.zeros_like(carry_ref)

    row = lax.broadcasted_iota(I32, (tm, tm), 0)
    col = lax.broadcasted_iota(I32, (tm, tm), 1)
    strict = jnp.where(row > col, 1.0, 0.0).astype(BF16)
    hit1 = lane == e1
    hit2 = lane == e2
    oh1 = jnp.where(hit1, 1.0, 0.0)
    oh2 = jnp.where(hit2, 1.0, 0.0)
    tot1 = jnp.sum(oh1, axis=0, keepdims=True)
    tot2 = jnp.sum(oh2, axis=0, keepdims=True)
    base = carry_ref[...]
    before1 = _dot(strict, oh1.astype(BF16)) + base
    before2 = _dot(strict, oh2.astype(BF16)) + (base + tot1)
    r1 = jnp.sum(jnp.where(hit1, before1, 0.0), axis=1, keepdims=True)
    r2 = jnp.sum(jnp.where(hit2, before2, 0.0), axis=1, keepdims=True)
    rank_ref[...] = jnp.where(lane == 0.0, r1, r2)[:, :2].astype(I32)
    total = base + tot1 + tot2
    carry_ref[...] = total
    cnt_ref[...] = total.astype(I32)


def _postattn(x, fox, moba, gf, gm, wo, gn, wr, br):
    t, d = x.shape
    tm = TM_PROJ
    fw = fox.shape[1]
    const = lambda i: (0, 0)
    rows = lambda i: (i, 0)
    pair = pl.BlockSpec((tm, 2), rows)
    return pl.pallas_call(
        _postattn_kernel,
        grid=(t // tm,),
        in_specs=[pl.BlockSpec((tm, d), rows), pl.BlockSpec((tm, fw), rows),
                  pl.BlockSpec((tm, moba.shape[1]), rows),
                  pl.BlockSpec((1, fw), const), pl.BlockSpec((1, moba.shape[1]), const),
                  pl.BlockSpec(wo.shape, const), pl.BlockSpec((1, d), const),
                  pl.BlockSpec(wr.shape, const), pl.BlockSpec((1, LANES), const)],
        out_specs=[pl.BlockSpec((tm, d), rows), pl.BlockSpec((tm, d), rows), pair, pair, pair,
                   pl.BlockSpec((1, LANES), const)],
        out_shape=[jax.ShapeDtypeStruct((t, d), F32), jax.ShapeDtypeStruct((t, d), F32),
                   jax.ShapeDtypeStruct((t, 2), I32), jax.ShapeDtypeStruct((t, 2), F32),
                   jax.ShapeDtypeStruct((t, 2), I32), jax.ShapeDtypeStruct((1, LANES), I32)],
        scratch_shapes=[pltpu.VMEM((1, LANES), F32)],
        compiler_params=_params(("arbitrary",)),
        name="postattn",
    )(x, fox, moba, gf, gm, wo, gn, wr, br)


def _row_copy(src, src_row, dst, dst_row, sem):
    return pltpu.make_async_copy(src.at[pl.ds(src_row, 1)], dst.at[pl.ds(dst_row, 1)], sem)


def _dispatch_kernel(dest_ref, h_ref, init_ref, xs_ref, sem):
    del init_ref
    n = dest_ref.shape[2]
    base = pl.program_id(0) * (n // 2)

    def issue(a, c):
        _row_copy(h_ref, base + a // 2, xs_ref, dest_ref[0, 0, a], sem).start()
        return c

    lax.fori_loop(0, n, issue, 0)

    def drain(a, c):
        _row_copy(h_ref, base, xs_ref, 0, sem).wait()
        return c

    lax.fori_loop(0, n, drain, 0)


def _dispatch(dest, h2, n_rows):
    t, d = h2.shape
    tm = TM_ROWS
    dest3 = dest.reshape(t // tm, 1, 2 * tm)
    return pl.pallas_call(
        _dispatch_kernel,
        grid=(t // tm,),
        in_specs=[pl.BlockSpec((1, 1, 2 * tm), lambda i: (i, 0, 0), memory_space=pltpu.SMEM),
                  pl.BlockSpec(memory_space=pl.ANY), pl.BlockSpec(memory_space=pl.ANY)],
        out_specs=pl.BlockSpec(memory_space=pl.ANY),
        out_shape=jax.ShapeDtypeStruct((n_rows, d), F32),
        scratch_shapes=[pltpu.SemaphoreType.DMA(())],
        input_output_aliases={2: 0},
        compiler_params=_params(("arbitrary",)),
        name="dispatch",
    )(dest3, h2, jnp.zeros((n_rows, d), F32))


def _experts_kernel(te_ref, ts_ref, nv_ref, xs_ref, wg_ref, wu_ref, wd_ref, ys_ref):
    del te_ref, ts_ref
    t = pl.program_id(0)

    @pl.when(t < nv_ref[0])
    def _():
        xb = xs_ref[...].astype(BF16)
        a = _dot(xb, wg_ref[0])
        u = _dot(xb, wu_ref[0])
        act = (a * jax.nn.sigmoid(a) * u).astype(BF16)
        ys_ref[...] = _dot(act, wd_ref[0])

    @pl.when(t >= nv_ref[0])
    def _():
        ys_ref[...] = jnp.zeros_like(ys_ref)


def _experts(tile_expert, tile_src, n_valid, xs, wg, wu, wd):
    n_rows, d = xs.shape
    tm = TM_EXPERT
    n_tiles = n_rows // tm
    f = wg.shape[2]
    grid_spec = pltpu.PrefetchScalarGridSpec(
        num_scalar_prefetch=3,
        grid=(n_tiles,),
        in_specs=[pl.BlockSpec((tm, d), lambda t, te, ts, nv: (ts[t], 0)),
                  pl.BlockSpec((1, d, f), lambda t, te, ts, nv: (te[t], 0, 0)),
                  pl.BlockSpec((1, d, f), lambda t, te, ts, nv: (te[t], 0, 0)),
                  pl.BlockSpec((1, f, d), lambda t, te, ts, nv: (te[t], 0, 0))],
        out_specs=pl.BlockSpec((tm, d), lambda t, te, ts, nv: (t, 0)),
    )
    return pl.pallas_call(
        _experts_kernel,
        grid_spec=grid_spec,
        out_shape=jax.ShapeDtypeStruct((n_rows, d), F32),
        compiler_params=_params(("arbitrary",)),
        name="experts",
    )(tile_expert, tile_src, n_valid, xs, wg, wu, wd)


def _combine_kernel(dest_ref, x2_ref, wts_ref, g_ref, ys_ref, o_ref, buf_ref, sem):
    n = dest_ref.shape[2]
    tm = n // 2

    def issue(a, c):
        pltpu.make_async_copy(ys_ref.at[pl.ds(dest_ref[0, 0, a], 1)],
                              buf_ref.at[a % 2, pl.ds(a // 2, 1)], sem).start()
        return c

    lax.fori_loop(0, n, issue, 0)

    def drain(a, c):
        pltpu.make_async_copy(ys_ref.at[pl.ds(0, 1)], buf_ref.at[0, pl.ds(0, 1)], sem).wait()
        return c

    lax.fori_loop(0, n, drain, 0)
    w = wts_ref[...]
    y = x2_ref[...] + w[:, 0:1] * buf_ref[0] + w[:, 1:2] * buf_ref[1]
    o_ref[...] = _rms(y, g_ref[...])
    del tm


def _combine(dest, x2, wts, g, ys):
    t, d = x2.shape
    tm = TM_ROWS
    dest3 = dest.reshape(t // tm, 1, 2 * tm)
    rows = lambda i: (i, 0)
    return pl.pallas_call(
        _combine_kernel,
        grid=(t // tm,),
        in_specs=[pl.BlockSpec((1, 1, 2 * tm), lambda i: (i, 0, 0), memory_space=pltpu.SMEM),
                  pl.BlockSpec((tm, d), rows), pl.BlockSpec((tm, 2), rows),
                  pl.BlockSpec((1, d), lambda i: (0, 0)),
                  pl.BlockSpec(memory_space=pl.ANY)],
        out_specs=pl.BlockSpec((tm, d), rows),
        out_shape=jax.ShapeDtypeStruct((t, d), F32),
        scratch_shapes=[pltpu.VMEM((2, tm, d), F32), pltpu.SemaphoreType.DMA(())],
        compiler_params=_params(("arbitrary",)),
        name="combine",
    )(dest3, x2, wts, g, ys)


def _rotary_tables(seq):
    half = ROPE_DIM // 2
    inv_freq = ROPE_THETA ** (-jnp.arange(half, dtype=F32) / half)
    ang = jnp.arange(seq, dtype=F32)[:, None] * inv_freq[None, :]
    cos, sin = jnp.cos(ang), jnp.sin(ang)
    ones = jnp.ones((seq, HEAD_DIM - ROPE_DIM), F32)
    zeros = jnp.zeros((seq, HEAD_DIM - ROPE_DIM), F32)
    zh = jnp.zeros((seq, half), F32)
    cosm = jnp.concatenate([cos, cos, ones], axis=1)
    sina = jnp.concatenate([-sin, zh, zeros], axis=1)
    sinb = jnp.concatenate([zh, sin, zeros], axis=1)
    tile = lambda a: jnp.tile(a, (1, HEADS_PER_BLOCK))
    return tile(cosm), tile(sina), tile(sinb)


def _pad_lanes(a):
    return jnp.pad(a, ((0, 0), (0, LANES - a.shape[1])))


def kernel(x, norm_mix_g, w_in, b_forget, fox_out_g, moba_out_g, w_out, norm_ffn_g, w_router_group,
           b_router_group, w_router_expert, b_router_expert, w_gate, w_up, w_down, norm_final_g):
    b, s, d = x.shape
    t = b * s
    assert w_in.shape[0] == 1, "the closing RMSNorm is fused into the only layer's combine step"
    depth = 1
    cosm, sina, sinb = _rotary_tables(s)
    n_tiles = (2 * t) // TM_EXPERT + N_EXPERTS
    fw3 = 3 * FOX_WIDTH
    for l in range(depth):
        wl = w_in[l]
        w_main = jnp.concatenate([wl[:, :fw3], wl[:, fw3 + N_FOX_HEADS:]], axis=1).astype(BF16)
        w_logit = _pad_lanes(wl[:, fw3:fw3 + N_FOX_HEADS]).astype(BF16)
        b_logit = _pad_lanes(b_forget[l][None, :])
        fq, fk, fv, mq, mk, mv, c, kmean = _inproj(x, norm_mix_g[l][None, :], w_main, w_logit, b_logit,
                                                   cosm, sina, sinb)
        fox = _fox(fq, fk, fv, c)
        moba = _moba(mq, mk, mv, kmean)

        w_router = _pad_lanes(jnp.concatenate(
            [w_router_group[l], w_router_expert[l].reshape(d, N_EXPERTS)], axis=1))
        b_router = _pad_lanes(jnp.concatenate(
            [b_router_group[l], b_router_expert[l].reshape(N_EXPERTS)])[None, :])
        x2, h2, idx, wts, rank, counts = _postattn(
            x.reshape(t, d), fox.reshape(t, FOX_WIDTH), moba.reshape(t, MOBA_WIDTH),
            fox_out_g[l][None, :], moba_out_g[l][None, :], w_out[l].astype(BF16),
            norm_ffn_g[l][None, :], w_router, b_router)

        counts = counts[0, :N_EXPERTS]
        padded = (counts + TM_EXPERT - 1) // TM_EXPERT * TM_EXPERT
        ends = jnp.cumsum(padded)
        dest = (ends - padded)[idx] + rank
        n_valid = ends[-1] // TM_EXPERT
        tile_src = jnp.minimum(jnp.arange(n_tiles, dtype=I32), n_valid - 1)
        tile_expert = jnp.searchsorted(ends, tile_src * TM_EXPERT, side="right").astype(I32)
        xs = _dispatch(dest, h2, n_tiles * TM_EXPERT)
        ys = _experts(tile_expert, tile_src, n_valid.reshape(1).astype(I32), xs,
                      w_gate[l].astype(BF16), w_up[l].astype(BF16), w_down[l].astype(BF16))
        x = _combine(dest, x2, wts, norm_final_g[None, :], ys).reshape(b, s, d)
    return x
```

```python
import jax
import jax.numpy as jnp
from jax import lax
from jax.experimental import pallas as pl
from jax.experimental.pallas import tpu as pltpu

F32 = jnp.float32
BF16 = jnp.bfloat16
I32 = jnp.int32

HEAD_DIM = 64
N_FOX_HEADS = 8
N_MOBA_HEADS = 8
FOX_WIDTH = N_FOX_HEADS * HEAD_DIM
MOBA_WIDTH = N_MOBA_HEADS * HEAD_DIM
MOBA_BLOCK = 256
MOBA_TOPK = 3
ROPE_THETA = 500000.0
ROPE_DIM = HEAD_DIM // 4
N_GROUPS = 4
EXPERTS_PER_GROUP = 8
N_EXPERTS = N_GROUPS * EXPERTS_PER_GROUP
EPS = 1e-6

LANES = 128
SUBLANES = 8
HEADS_PER_BLOCK = LANES // HEAD_DIM
VMEM_LIMIT = 56 * 1024 * 1024
AUX_PER_HEAD = 6

TM_PROJ = 512
TQ = 256
TM_EXPERT = 256
TM_ROWS = 256

NEG_INF = float("-inf")


def _params(sem):
    return pltpu.CompilerParams(dimension_semantics=sem, vmem_limit_bytes=VMEM_LIMIT)


def _rms(x, g):
    return x * lax.rsqrt(jnp.mean(x * x, axis=-1, keepdims=True) + EPS) * g


def _split3(x):
    hi = x.astype(BF16)
    r = x - hi.astype(F32)
    mid = r.astype(BF16)
    lo = (r - mid.astype(F32)).astype(BF16)
    return hi, mid, lo


def _dot(a, b):
    return jnp.dot(a, b, preferred_element_type=F32)


def _dot_nt(a, b):
    return lax.dot_general(a, b, (((1,), (1,)), ((), ())), preferred_element_type=F32)


def _inproj_kernel(x_ref, g_ref, w_ref, wvt_ref, wl_ref, bf_ref, eq_ref, ek_ref, cq_ref, ck_ref,
                   cosm_ref, sina_ref, sinb_ref,
                   fq_ref, fqa_ref, fk_ref, fka_ref, fvt_ref, mq_ref, mk_ref, mvt_ref, kmean_ref,
                   carry_ref):
    j = pl.program_id(1)
    tm = x_ref.shape[1]
    tk = fvt_ref.shape[3]
    h = _rms(x_ref[0], g_ref[...]).astype(BF16)
    scale = HEAD_DIM ** -0.5

    def proj(seg):
        return _dot(h, w_ref[:, seg * FOX_WIDTH:(seg + 1) * FOX_WIDTH])

    fq_ref[0] = (proj(0) * scale).astype(BF16)
    fk_ref[0] = proj(1).astype(BF16)

    for vt_ref, seg in ((fvt_ref, 0), (mvt_ref, 1)):
        vt = _dot_nt(wvt_ref[seg], h).astype(BF16)
        for r in range(tm // tk):
            vt_ref[0, r] = vt[:, r * tk:(r + 1) * tk]

    cosm, sina, sinb = cosm_ref[...], sina_ref[...], sinb_ref[...]

    def rotary(t):
        outs = []
        for g in range(MOBA_WIDTH // LANES):
            tg = t[:, g * LANES:(g + 1) * LANES]
            outs.append(tg * cosm + pltpu.roll(tg, LANES - ROPE_DIM // 2, 1) * sina
                        + pltpu.roll(tg, ROPE_DIM // 2, 1) * sinb)
        return jnp.concatenate(outs, axis=1)

    mq_ref[0] = (rotary(proj(2)) * scale).astype(BF16)
    mk = rotary(proj(3))
    mk_ref[0] = mk.astype(BF16)
    nblk_tile = tm // MOBA_BLOCK
    means = [jnp.mean(mk[r * MOBA_BLOCK:(r + 1) * MOBA_BLOCK], axis=0, keepdims=True)
             for r in range(nblk_tile)]
    means += [jnp.zeros_like(means[0])] * (kmean_ref.shape[2] - nblk_tile)
    kmean_ref[0, 0] = jnp.concatenate(means, axis=0)

    z = _dot(h, wl_ref[...]) + bf_ref[...]
    log_f = jnp.minimum(z, 0.0) - jnp.log1p(jnp.exp(-jnp.abs(z)))
    row = lax.broadcasted_iota(I32, (tm, tm), 0)
    col = lax.broadcasted_iota(I32, (tm, tm), 1)
    tri = jnp.where(row >= col, 1.0, 0.0).astype(BF16)
    local = sum(_dot(tri, part) for part in _split3(log_f))

    @pl.when(j == 0)
    def _():
        carry_ref[...] = jnp.zeros_like(carry_ref)

    c = local + carry_ref[...]
    carry_ref[...] = c[tm - 1:tm, :]
    parts = _split3(c)
    fqa_ref[0] = (sum(_dot(p, eq_ref[n]) for n, p in enumerate(parts)) + cq_ref[...]).astype(BF16)
    fka_ref[0] = (sum(_dot(p, ek_ref[n]) for n, p in enumerate(parts)) + ck_ref[...]).astype(BF16)


def _aux_tables():
    nhb = N_FOX_HEADS // HEADS_PER_BLOCK
    width = nhb * LANES
    head = jnp.arange(N_FOX_HEADS)
    base = LANES * (head // HEADS_PER_BLOCK) + AUX_PER_HEAD * (head % HEADS_PER_BLOCK)
    eq = jnp.zeros((3, LANES, width), F32)
    ek = jnp.zeros((3, LANES, width), F32)
    cq = jnp.zeros((1, width), F32)
    ck = jnp.zeros((1, width), F32)
    for n in range(3):
        eq = eq.at[n, head, base + 3 + n].set(1.0)
        ek = ek.at[n, head, base + n].set(-1.0)
        cq = cq.at[0, base + n].set(1.0)
        ck = ck.at[0, base + 3 + n].set(1.0)
    return eq.astype(BF16), ek.astype(BF16), cq, ck


def _inproj(x, g, w_main, w_vt, w_logit, b_logit, cosm, sina, sinb):
    b, s, d = x.shape
    tm, tk = TM_PROJ, TQ
    eq, ek, cq, ck = _aux_tables()
    act = jax.ShapeDtypeStruct((b, s, FOX_WIDTH), BF16)
    vt = jax.ShapeDtypeStruct((b, s // tk, FOX_WIDTH, tk), BF16)
    out_shape = [act, act, act, act, vt, act, act, vt,
                 jax.ShapeDtypeStruct((b, s // tm, SUBLANES, MOBA_WIDTH), F32)]
    act_spec = pl.BlockSpec((1, tm, FOX_WIDTH), lambda bi, j: (bi, j, 0))
    vt_spec = pl.BlockSpec((1, tm // tk, FOX_WIDTH, tk), lambda bi, j: (bi, j, 0, 0))
    tab_spec = pl.BlockSpec((tm, LANES), lambda bi, j: (j, 0))
    const2 = lambda bi, j: (0, 0)
    const3 = lambda bi, j: (0, 0, 0)
    *acts, kmean = pl.pallas_call(
        _inproj_kernel,
        grid=(b, s // tm),
        in_specs=[pl.BlockSpec((1, tm, d), lambda bi, j: (bi, j, 0)),
                  pl.BlockSpec((1, d), const2),
                  pl.BlockSpec(w_main.shape, const2),
                  pl.BlockSpec(w_vt.shape, const3),
                  pl.BlockSpec(w_logit.shape, const2),
                  pl.BlockSpec((1, LANES), const2),
                  pl.BlockSpec(eq.shape, const3), pl.BlockSpec(ek.shape, const3),
                  pl.BlockSpec(cq.shape, const2), pl.BlockSpec(ck.shape, const2),
                  tab_spec, tab_spec, tab_spec],
        out_specs=[act_spec, act_spec, act_spec, act_spec, vt_spec, act_spec, act_spec, vt_spec,
                   pl.BlockSpec((1, 1, SUBLANES, MOBA_WIDTH), lambda bi, j: (bi, j, 0, 0))],
        out_shape=out_shape,
        scratch_shapes=[pltpu.VMEM((1, LANES), F32)],
        compiler_params=_params(("arbitrary", "arbitrary")),
        name="inproj",
    )(x, g, w_main, w_vt, w_logit, b_logit, eq, ek, cq, ck, cosm, sina, sinb)
    kmean = kmean[:, :, :tm // MOBA_BLOCK].reshape(b, s // MOBA_BLOCK, MOBA_WIDTH)
    return (*acts, kmean)


def _attend(scores, vt, m_ref, l_ref, acc_ref):
    stats = []
    for hh, s in enumerate(scores):
        m_prev = m_ref[hh]
        m_new = jnp.maximum(m_prev, jnp.max(s, axis=0, keepdims=True))
        m_ref[hh] = m_new
        stats.append((jnp.exp(m_prev - m_new), m_new))
    probs = []
    for hh, s in enumerate(scores):
        alpha, m_new = stats[hh]
        p = jnp.exp(s - m_new)
        l_ref[hh] = alpha * l_ref[hh] + jnp.sum(p, axis=0, keepdims=True)
        probs.append(p.astype(BF16))
    for hh, p in enumerate(probs):
        rows = slice(hh * HEAD_DIM, (hh + 1) * HEAD_DIM)
        acc_ref[rows, :] = stats[hh][0] * acc_ref[rows, :] + _dot(vt[rows, :], p)


def _attend_tiles(i, scores_of, vt_ref, m_ref, l_ref, acc_ref):
    def body(kt, scores):
        nxt = scores_of(kt, False)
        _attend(scores, vt_ref[0, jnp.where(kt == 0, i, kt - 1)], m_ref, l_ref, acc_ref)
        return nxt

    scores = lax.fori_loop(0, i, body, scores_of(i, True))
    _attend(scores, vt_ref[0, jnp.maximum(i - 1, 0)], m_ref, l_ref, acc_ref)


def _attn_init(m_ref, l_ref, acc_ref):
    m_ref[...] = jnp.full(m_ref.shape, NEG_INF, F32)
    l_ref[...] = jnp.zeros_like(l_ref)
    acc_ref[...] = jnp.zeros_like(acc_ref)


def _attn_finish(o_ref, l_ref, acc_ref):
    out_t = jnp.concatenate(
        [acc_ref[hh * HEAD_DIM:(hh + 1) * HEAD_DIM, :] / l_ref[hh] for hh in range(HEADS_PER_BLOCK)],
        axis=0)
    o_ref[0] = out_t.T


def _head_masks(tq):
    lane = lax.broadcasted_iota(I32, (tq, LANES), 1)
    return lane < HEAD_DIM, lane < AUX_PER_HEAD


def _key_le_query(tq):
    return lax.broadcasted_iota(I32, (tq, tq), 0) <= lax.broadcasted_iota(I32, (tq, tq), 1)


def _fox_kernel(q_ref, qa_ref, k_ref, ka_ref, vt_ref, o_ref, m_ref, l_ref, acc_ref):
    i = pl.program_id(2)
    tq = q_ref.shape[1]
    head0, aux0 = _head_masks(tq)
    q, qa = q_ref[0], qa_ref[0]
    zero = jnp.zeros_like(q)
    qq = (jnp.concatenate([jnp.where(head0, q, zero), jnp.where(aux0, qa, zero)], axis=1),
          jnp.concatenate([jnp.where(head0, zero, q), jnp.where(aux0, zero, qa)], axis=1))
    causal = _key_le_query(tq)
    _attn_init(m_ref, l_ref, acc_ref)

    def scores_of(kt, diag):
        ks = pl.multiple_of(kt * tq, tq)
        kk = jnp.concatenate([k_ref[0, pl.ds(ks, tq), :], ka_ref[0, pl.ds(ks, tq), :]], axis=1)
        scores = [_dot_nt(kk, qq[hh]) for hh in range(HEADS_PER_BLOCK)]
        if diag:
            scores = [jnp.where(causal, s, NEG_INF) for s in scores]
        return tuple(scores)

    _attend_tiles(i, scores_of, vt_ref, m_ref, l_ref, acc_ref)
    _attn_finish(o_ref, l_ref, acc_ref)


def _moba_kernel(q_ref, k_ref, vt_ref, kmean_ref, o_ref, m_ref, l_ref, acc_ref, bias_ref):
    i = pl.program_id(2)
    tq = q_ref.shape[1]
    nblk = kmean_ref.shape[1]
    head0, _ = _head_masks(tq)
    q = q_ref[0]
    zero = jnp.zeros_like(q)
    qs = (jnp.where(head0, q, zero), jnp.where(head0, zero, q))
    causal = _key_le_query(tq)
    _attn_init(m_ref, l_ref, acc_ref)

    km_parts = _split3(kmean_ref[0])
    blk = lax.broadcasted_iota(I32, (nblk, tq), 0).astype(F32)
    past = blk < i.astype(F32)
    for hh in range(HEADS_PER_BLOCK):
        gate = sum(_dot_nt(part, qs[hh]) for part in km_parts)
        sel = jnp.zeros((nblk, tq), jnp.bool_)
        for _ in range(MOBA_TOPK):
            remaining = jnp.logical_and(past, jnp.logical_not(sel))
            g = jnp.where(remaining, gate, NEG_INF)
            first = jnp.min(jnp.where(g == jnp.max(g, axis=0, keepdims=True), blk, float(nblk)),
                            axis=0, keepdims=True)
            sel = jnp.logical_or(sel, jnp.logical_and(blk == first, remaining))
        bias_ref[hh] = jnp.where(sel, 0.0, NEG_INF)

    def scores_of(kt, diag):
        ks = pl.multiple_of(kt * tq, tq)
        k = k_ref[0, pl.ds(ks, tq), :]
        scores = [_dot_nt(k, qs[hh]) for hh in range(HEADS_PER_BLOCK)]
        if diag:
            scores = [jnp.where(causal, s, NEG_INF) for s in scores]
        else:
            scores = [s + bias_ref[hh, pl.ds(kt, 1), :] for hh, s in enumerate(scores)]
        return tuple(scores)

    _attend_tiles(i, scores_of, vt_ref, m_ref, l_ref, acc_ref)
    _attn_finish(o_ref, l_ref, acc_ref)


def _attn_scratch(tq):
    return [pltpu.VMEM((HEADS_PER_BLOCK, 1, tq), F32), pltpu.VMEM((HEADS_PER_BLOCK, 1, tq), F32),
            pltpu.VMEM((LANES, tq), F32)]


def _attn_specs(s, tq):
    q_spec = pl.BlockSpec((1, tq, LANES), lambda bi, hb, i: (bi, i, hb))
    k_spec = pl.BlockSpec((1, s, LANES), lambda bi, hb, i: (bi, 0, hb))
    vt_spec = pl.BlockSpec((1, s // tq, LANES, tq), lambda bi, hb, i: (bi, 0, hb, 0))
    return q_spec, k_spec, vt_spec


def _fox(q, qa, k, ka, vt):
    b, s, width = q.shape
    tq = TQ
    q_spec, k_spec, vt_spec = _attn_specs(s, tq)
    return pl.pallas_call(
        _fox_kernel,
        grid=(b, width // LANES, s // tq),
        in_specs=[q_spec, q_spec, k_spec, k_spec, vt_spec],
        out_specs=q_spec,
        out_shape=jax.ShapeDtypeStruct((b, s, width), F32),
        scratch_shapes=_attn_scratch(tq),
        compiler_params=_params(("arbitrary", "arbitrary", "arbitrary")),
        name="fox",
    )(q, qa, k, ka, vt)


def _moba(q, k, vt, kmean):
    b, s, width = q.shape
    tq = TQ
    nblk = kmean.shape[1]
    q_spec, k_spec, vt_spec = _attn_specs(s, tq)
    return pl.pallas_call(
        _moba_kernel,
        grid=(b, width // LANES, s // tq),
        in_specs=[q_spec, k_spec, vt_spec,
                  pl.BlockSpec((1, nblk, LANES), lambda bi, hb, i: (bi, 0, hb))],
        out_specs=q_spec,
        out_shape=jax.ShapeDtypeStruct((b, s, width), F32),
        scratch_shapes=_attn_scratch(tq) + [pltpu.VMEM((HEADS_PER_BLOCK, nblk, tq), F32)],
        compiler_params=_params(("arbitrary", "arbitrary", "arbitrary")),
        name="moba",
    )(q, k, vt, kmean)


def _postattn_kernel(x_ref, fox_ref, moba_ref, gf_ref, gm_ref, wo_ref, gn_ref, wr_ref, br_ref,
                     x2_ref, h2_ref, idx_ref, wts_ref, rank_ref, cnt_ref, carry_ref):
    t = pl.program_id(0)
    tm = x_ref.shape[0]
    fw = fox_ref.shape[1]
    mixed_f = _rms(fox_ref[...], gf_ref[...]).astype(BF16)
    mixed_m = _rms(moba_ref[...], gm_ref[...]).astype(BF16)
    x2 = x_ref[...] + _dot(mixed_f, wo_ref[:fw, :]) + _dot(mixed_m, wo_ref[fw:, :])
    x2_ref[...] = x2
    h2 = _rms(x2, gn_ref[...])
    h2_ref[...] = h2

    h_hi = h2.astype(BF16)
    h_lo = (h2 - h_hi.astype(F32)).astype(BF16)
    wr = wr_ref[...]
    w_hi = wr.astype(BF16)
    w_lo = (wr - w_hi.astype(F32)).astype(BF16)
    logits = _dot(h_hi, w_hi) + _dot(h_hi, w_lo) + _dot(h_lo, w_hi) + br_ref[...]
    lane = lax.broadcasted_iota(I32, (tm, LANES), 1).astype(F32)

    def first_max(vals):
        mx = jnp.max(vals, axis=1, keepdims=True)
        return mx, jnp.min(jnp.where(vals == mx, lane, float(LANES)), axis=1, keepdims=True)

    gl = jnp.where(lane < N_GROUPS, logits, NEG_INF)
    gmax, g_idx = first_max(gl)
    g_top = 1.0 / jnp.sum(jnp.exp(gl - gmax), axis=1, keepdims=True)
    e_lo = N_GROUPS + EXPERTS_PER_GROUP * g_idx
    el = jnp.where(jnp.logical_and(lane >= e_lo, lane < e_lo + EXPERTS_PER_GROUP), logits, NEG_INF)
    emax, i1 = first_max(el)
    esum = jnp.sum(jnp.exp(el - emax), axis=1, keepdims=True)
    e2max, i2 = first_max(jnp.where(lane == i1, NEG_INF, el))
    p1 = 1.0 / esum
    p2 = jnp.exp(e2max - emax) / esum
    w1 = p1 / (p1 + p2) * g_top
    w2 = p2 / (p1 + p2) * g_top
    e1 = i1 - N_GROUPS
    e2 = i2 - N_GROUPS
    idx_ref[...] = jnp.where(lane == 0.0, e1, e2)[:, :2].astype(I32)
    wts_ref[...] = jnp.where(lane == 0.0, w1, w2)[:, :2]

    @pl.when(t == 0)
    def _():
        carry_ref[...] = jnp.zeros_like(carry_ref)

    row = lax.broadcasted_iota(I32, (tm, tm), 0)
    col = lax.broadcasted_iota(I32, (tm, tm), 1)
    strict = jnp.where(row > col, 1.0, 0.0).astype(BF16)
    hit1 = lane == e1
    hit2 = lane == e2
    oh1 = jnp.where(hit1, 1.0, 0.0)
    oh2 = jnp.where(hit2, 1.0, 0.0)
    tot1 = jnp.sum(oh1, axis=0, keepdims=True)
    tot2 = jnp.sum(oh2, axis=0, keepdims=True)
    base = carry_ref[...]
    before1 = _dot(strict, oh1.astype(BF16)) + base
    before2 = _dot(strict, oh2.astype(BF16)) + (base + tot1)
    r1 = jnp.sum(jnp.where(hit1, before1, 0.0), axis=1, keepdims=True)
    r2 = jnp.sum(jnp.where(hit2, before2, 0.0), axis=1, keepdims=True)
    rank_ref[...] = jnp.where(lane == 0.0, r1, r2)[:, :2].astype(I32)
    total = base + tot1 + tot2
    carry_ref[...] = total
    cnt_ref[...] = total.astype(I32)


def _postattn(x, fox, moba, gf, gm, wo, gn, wr, br):
    t, d = x.shape
    tm = TM_PROJ
    fw = fox.shape[1]
    const = lambda i: (0, 0)
    rows = lambda i: (i, 0)
    pair = pl.BlockSpec((tm, 2), rows)
    return pl.pallas_call(
        _postattn_kernel,
        grid=(t // tm,),
        in_specs=[pl.BlockSpec((tm, d), rows), pl.BlockSpec((tm, fw), rows),
                  pl.BlockSpec((tm, moba.shape[1]), rows),
                  pl.BlockSpec((1, fw), const), pl.BlockSpec((1, moba.shape[1]), const),
                  pl.BlockSpec(wo.shape, const), pl.BlockSpec((1, d), const),
                  pl.BlockSpec(wr.shape, const), pl.BlockSpec((1, LANES), const)],
        out_specs=[pl.BlockSpec((tm, d), rows), pl.BlockSpec((tm, d), rows), pair, pair, pair,
                   pl.BlockSpec((1, LANES), const)],
        out_shape=[jax.ShapeDtypeStruct((t, d), F32), jax.ShapeDtypeStruct((t, d), F32),
                   jax.ShapeDtypeStruct((t, 2), I32), jax.ShapeDtypeStruct((t, 2), F32),
                   jax.ShapeDtypeStruct((t, 2), I32), jax.ShapeDtypeStruct((1, LANES), I32)],
        scratch_shapes=[pltpu.VMEM((1, LANES), F32)],
        compiler_params=_params(("arbitrary",)),
        name="postattn",
    )(x, fox, moba, gf, gm, wo, gn, wr, br)


def _row_copy(src, src_row, dst, dst_row, sem):
    return pltpu.make_async_copy(src.at[pl.ds(src_row, 1)], dst.at[pl.ds(dst_row, 1)], sem)


def _dispatch_kernel(dest_ref, h_ref, init_ref, xs_ref, sem):
    del init_ref
    n = dest_ref.shape[2]

    def issue(a, c):
        _row_copy(h_ref, a // 2, xs_ref, dest_ref[0, 0, a], sem).start()
        return c

    lax.fori_loop(0, n, issue, 0)

    def drain(a, c):
        _row_copy(h_ref, 0, xs_ref, 0, sem).wait()
        return c

    lax.fori_loop(0, n, drain, 0)


def _dispatch(dest, h2, n_rows):
    t, d = h2.shape
    tm = TM_ROWS
    dest3 = dest.reshape(t // tm, 1, 2 * tm)
    return pl.pallas_call(
        _dispatch_kernel,
        grid=(t // tm,),
        in_specs=[pl.BlockSpec((1, 1, 2 * tm), lambda i: (i, 0, 0), memory_space=pltpu.SMEM),
                  pl.BlockSpec((tm, d), lambda i: (i, 0)), pl.BlockSpec(memory_space=pl.ANY)],
        out_specs=pl.BlockSpec(memory_space=pl.ANY),
        out_shape=jax.ShapeDtypeStruct((n_rows, d), F32),
        scratch_shapes=[pltpu.SemaphoreType.DMA(())],
        input_output_aliases={2: 0},
        compiler_params=_params(("arbitrary",)),
        name="dispatch",
    )(dest3, h2, jnp.zeros((n_rows, d), F32))


def _experts_kernel(te_ref, ts_ref, nv_ref, xs_ref, wg_ref, wu_ref, wd_ref, ys_ref):
    del te_ref, ts_ref
    t = pl.program_id(0)

    @pl.when(t < nv_ref[0])
    def _():
        xb = xs_ref[...].astype(BF16)
        a = _dot(xb, wg_ref[0])
        u = _dot(xb, wu_ref[0])
        act = (a * jax.nn.sigmoid(a) * u).astype(BF16)
        ys_ref[...] = _dot(act, wd_ref[0])

    @pl.when(t >= nv_ref[0])
    def _():
        ys_ref[...] = jnp.zeros_like(ys_ref)


def _experts(tile_expert, tile_src, n_valid, xs, wg, wu, wd):
    n_rows, d = xs.shape
    tm = TM_EXPERT
    n_tiles = n_rows // tm
    f = wg.shape[2]
    grid_spec = pltpu.PrefetchScalarGridSpec(
        num_scalar_prefetch=3,
        grid=(n_tiles,),
        in_specs=[pl.BlockSpec((tm, d), lambda t, te, ts, nv: (ts[t], 0)),
                  pl.BlockSpec((1, d, f), lambda t, te, ts, nv: (te[t], 0, 0)),
                  pl.BlockSpec((1, d, f), lambda t, te, ts, nv: (te[t], 0, 0)),
                  pl.BlockSpec((1, f, d), lambda t, te, ts, nv: (te[t], 0, 0))],
        out_specs=pl.BlockSpec((tm, d), lambda t, te, ts, nv: (t, 0)),
    )
    return pl.pallas_call(
        _experts_kernel,
        grid_spec=grid_spec,
        out_shape=jax.ShapeDtypeStruct((n_rows, d), F32),
        compiler_params=_params(("arbitrary",)),
        name="experts",
    )(tile_expert, tile_src, n_valid, xs, wg, wu, wd)


def _combine_kernel(dest_ref, x2_ref, wts_ref, g_ref, ys_ref, o_ref, buf_ref, sem):
    n = dest_ref.shape[2]

    def issue(a, c):
        pltpu.make_async_copy(ys_ref.at[pl.ds(dest_ref[0, 0, a], 1)],
                              buf_ref.at[a % 2, pl.ds(a // 2, 1)], sem).start()
        return c

    lax.fori_loop(0, n, issue, 0)

    def drain(a, c):
        pltpu.make_async_copy(ys_ref.at[pl.ds(0, 1)], buf_ref.at[0, pl.ds(0, 1)], sem).wait()
        return c

    lax.fori_loop(0, n, drain, 0)
    w = wts_ref[...]
    y = x2_ref[...] + w[:, 0:1] * buf_ref[0] + w[:, 1:2] * buf_ref[1]
    o_ref[...] = _rms(y, g_ref[...])


def _combine(dest, x2, wts, g, ys):
    t, d = x2.shape
    tm = TM_ROWS
    dest3 = dest.reshape(t // tm, 1, 2 * tm)
    rows = lambda i: (i, 0)
    return pl.pallas_call(
        _combine_kernel,
        grid=(t // tm,),
        in_specs=[pl.BlockSpec((1, 1, 2 * tm), lambda i: (i, 0, 0), memory_space=pltpu.SMEM),
                  pl.BlockSpec((tm, d), rows), pl.BlockSpec((tm, 2), rows),
                  pl.BlockSpec((1, d), lambda i: (0, 0)),
                  pl.BlockSpec(memory_space=pl.ANY)],
        out_specs=pl.BlockSpec((tm, d), rows),
        out_shape=jax.ShapeDtypeStruct((t, d), F32),
        scratch_shapes=[pltpu.VMEM((2, tm, d), F32), pltpu.SemaphoreType.DMA(())],
        compiler_params=_params(("arbitrary",)),
        name="combine",
    )(dest3, x2, wts, g, ys)


def _rotary_tables(seq):
    half = ROPE_DIM // 2
    inv_freq = ROPE_THETA ** (-jnp.arange(half, dtype=F32) / half)
    ang = jnp.arange(seq, dtype=F32)[:, None] * inv_freq[None, :]
    cos, sin = jnp.cos(ang), jnp.sin(ang)
    ones = jnp.ones((seq, HEAD_DIM - ROPE_DIM), F32)
    zeros = jnp.zeros((seq, HEAD_DIM - ROPE_DIM), F32)
    zh = jnp.zeros((seq, half), F32)
    cosm = jnp.concatenate([cos, cos, ones], axis=1)
    sina = jnp.concatenate([-sin, zh, zeros], axis=1)
    sinb = jnp.concatenate([zh, sin, zeros], axis=1)
    tile = lambda a: jnp.tile(a, (1, HEADS_PER_BLOCK))
    return tile(cosm), tile(sina), tile(sinb)


def _pad_lanes(a):
    return jnp.pad(a, ((0, 0), (0, LANES - a.shape[1])))


def kernel(x, norm_mix_g, w_in, b_forget, fox_out_g, moba_out_g, w_out, norm_ffn_g, w_router_group,
           b_router_group, w_router_expert, b_router_expert, w_gate, w_up, w_down, norm_final_g):
    b, s, d = x.shape
    t = b * s
    assert w_in.shape[0] == 1, "the closing RMSNorm is fused into the only layer's combine step"
    cosm, sina, sinb = _rotary_tables(s)
    n_tiles = (2 * t) // TM_EXPERT + N_EXPERTS
    fw3 = 3 * FOX_WIDTH
    m0 = fw3 + N_FOX_HEADS
    wl = w_in[0]
    w_main = jnp.concatenate([wl[:, :2 * FOX_WIDTH], wl[:, m0:m0 + 2 * MOBA_WIDTH]],
                             axis=1).astype(BF16)
    w_vt = jnp.stack([wl[:, 2 * FOX_WIDTH:fw3].T, wl[:, m0 + 2 * MOBA_WIDTH:].T]).astype(BF16)
    w_logit = _pad_lanes(wl[:, fw3:m0]).astype(BF16)
    b_logit = _pad_lanes(b_forget[0][None, :])
    fq, fqa, fk, fka, fvt, mq, mk, mvt, kmean = _inproj(
        x, norm_mix_g[0][None, :], w_main, w_vt, w_logit, b_logit, cosm, sina, sinb)
    fox = _fox(fq, fqa, fk, fka, fvt)
    moba = _moba(mq, mk, mvt, kmean)

    w_router = _pad_lanes(jnp.concatenate(
        [w_router_group[0], w_router_expert[0].reshape(d, N_EXPERTS)], axis=1))
    b_router = _pad_lanes(jnp.concatenate(
        [b_router_group[0], b_router_expert[0].reshape(N_EXPERTS)])[None, :])
    x2, h2, idx, wts, rank, counts = _postattn(
        x.reshape(t, d), fox.reshape(t, FOX_WIDTH), moba.reshape(t, MOBA_WIDTH),
        fox_out_g[0][None, :], moba_out_g[0][None, :], w_out[0].astype(BF16),
        norm_ffn_g[0][None, :], w_router, b_router)

    counts = counts[0, :N_EXPERTS]
    padded = (counts + TM_EXPERT - 1) // TM_EXPERT * TM_EXPERT
    ends = jnp.cumsum(padded)
    dest = (ends - padded)[idx] + rank
    n_valid = ends[-1] // TM_EXPERT
    tile_src = jnp.minimum(jnp.arange(n_tiles, dtype=I32), n_valid - 1)
    tile_expert = jnp.sum(ends[None, :] <= (tile_src * TM_EXPERT)[:, None], axis=1).astype(I32)
    xs = _dispatch(dest, h2, n_tiles * TM_EXPERT)
    ys = _experts(tile_expert, tile_src, n_valid.reshape(1).astype(I32), xs,
                  w_gate[0].astype(BF16), w_up[0].astype(BF16), w_down[0].astype(BF16))
    return _combine(dest, x2, wts, norm_final_g[None, :], ys).reshape(b, s, d)
```

```python
import math

import jax
import jax.numpy as jnp
from jax import lax
from jax.experimental import pallas as pl
from jax.experimental.pallas import tpu as pltpu

F32 = jnp.float32
BF16 = jnp.bfloat16
I32 = jnp.int32

HEAD_DIM = 64
N_FOX_HEADS = 8
N_MOBA_HEADS = 8
FOX_WIDTH = N_FOX_HEADS * HEAD_DIM
MOBA_WIDTH = N_MOBA_HEADS * HEAD_DIM
MOBA_BLOCK = 256
MOBA_TOPK = 3
ROPE_THETA = 500000.0
ROPE_DIM = HEAD_DIM // 4
N_GROUPS = 4
EXPERTS_PER_GROUP = 8
N_EXPERTS = N_GROUPS * EXPERTS_PER_GROUP
EPS = 1e-6

LANES = 128
SUBLANES = 8
BF16_SUBLANES = 16
LOG2_E = math.log2(math.e)
HEADS_PER_BLOCK = LANES // HEAD_DIM
BLOCKS_PER_STEP = 4
HEADS_PER_STEP = HEADS_PER_BLOCK * BLOCKS_PER_STEP
STEP_LANES = LANES * BLOCKS_PER_STEP
VMEM_LIMIT = 56 * 1024 * 1024
AUX_PER_HEAD = 6

TM_PROJ = 512
TQ = 256
TM_EXPERT = 256
TM_ROWS = 256
ISSUE_UNROLL = 8

NEG_INF = float("-inf")


def _params(sem):
    return pltpu.CompilerParams(dimension_semantics=sem, vmem_limit_bytes=VMEM_LIMIT)


def _rms(x, g):
    return x * lax.rsqrt(jnp.mean(x * x, axis=-1, keepdims=True) + EPS) * g


def _split3(x):
    hi = x.astype(BF16)
    r = x - hi.astype(F32)
    mid = r.astype(BF16)
    lo = (r - mid.astype(F32)).astype(BF16)
    return hi, mid, lo


def _dot(a, b):
    return jnp.dot(a, b, preferred_element_type=F32)


def _dot_nt(a, b):
    return lax.dot_general(a, b, (((1,), (1,)), ((), ())), preferred_element_type=F32)


def _inproj_kernel(x_ref, g_ref, w_ref, wvt_ref, wl_ref, bf_ref, eq_ref, ek_ref, cq_ref, ck_ref,
                   cosm_ref, sina_ref, sinb_ref,
                   fq_ref, fqa_ref, fk_ref, fka_ref, fvt_ref, mq_ref, mk_ref, mvt_ref, kmean_ref,
                   carry_ref):
    j = pl.program_id(1)
    tm = x_ref.shape[1]
    tk = fvt_ref.shape[3]
    h = _rms(x_ref[0], g_ref[...]).astype(BF16)
    scale = HEAD_DIM ** -0.5 * LOG2_E

    def proj(seg):
        return _dot(h, w_ref[:, seg * FOX_WIDTH:(seg + 1) * FOX_WIDTH])

    fq_ref[0] = (proj(0) * scale).astype(BF16)
    fk_ref[0] = proj(1).astype(BF16)

    for vt_ref, seg in ((fvt_ref, 0), (mvt_ref, 1)):
        vt = _dot_nt(wvt_ref[seg], h).astype(BF16)
        for r in range(tm // tk):
            vt_ref[0, r] = vt[:, r * tk:(r + 1) * tk]

    cosm, sina, sinb = cosm_ref[...], sina_ref[...], sinb_ref[...]

    def rotary(t):
        outs = []
        for g in range(MOBA_WIDTH // LANES):
            tg = t[:, g * LANES:(g + 1) * LANES]
            outs.append(tg * cosm + pltpu.roll(tg, LANES - ROPE_DIM // 2, 1) * sina
                        + pltpu.roll(tg, ROPE_DIM // 2, 1) * sinb)
        return jnp.concatenate(outs, axis=1)

    mq_ref[0] = (rotary(proj(2)) * scale).astype(BF16)
    mk = rotary(proj(3))
    mk_ref[0] = mk.astype(BF16)
    nblk_tile = tm // MOBA_BLOCK
    means = [jnp.mean(mk[r * MOBA_BLOCK:(r + 1) * MOBA_BLOCK], axis=0, keepdims=True)
             for r in range(nblk_tile)]
    means += [jnp.zeros_like(means[0])] * (kmean_ref.shape[2] - nblk_tile)
    kmean_ref[0, 0] = jnp.concatenate(means, axis=0)

    z = _dot(h, wl_ref[...]) + bf_ref[...]
    log_f = jnp.minimum(z, 0.0) - jnp.log1p(jnp.exp(-jnp.abs(z)))
    row = lax.broadcasted_iota(I32, (tm, tm), 0)
    col = lax.broadcasted_iota(I32, (tm, tm), 1)
    tri = jnp.where(row >= col, 1.0, 0.0).astype(BF16)
    local = sum(_dot(tri, part) for part in _split3(log_f))

    @pl.when(j == 0)
    def _():
        carry_ref[...] = jnp.zeros_like(carry_ref)

    c = local + carry_ref[...]
    carry_ref[...] = c[tm - 1:tm, :]
    parts = _split3(c * LOG2_E)
    fqa_ref[0] = (sum(_dot(p, eq_ref[n]) for n, p in enumerate(parts)) + cq_ref[...]).astype(BF16)
    fka_ref[0] = (sum(_dot(p, ek_ref[n]) for n, p in enumerate(parts)) + ck_ref[...]).astype(BF16)


def _aux_tables():
    nhb = N_FOX_HEADS // HEADS_PER_BLOCK
    width = nhb * LANES
    head = jnp.arange(N_FOX_HEADS)
    base = LANES * (head // HEADS_PER_BLOCK) + AUX_PER_HEAD * (head % HEADS_PER_BLOCK)
    eq = jnp.zeros((3, LANES, width), F32)
    ek = jnp.zeros((3, LANES, width), F32)
    cq = jnp.zeros((1, width), F32)
    ck = jnp.zeros((1, width), F32)
    for n in range(3):
        eq = eq.at[n, head, base + 3 + n].set(1.0)
        ek = ek.at[n, head, base + n].set(-1.0)
        cq = cq.at[0, base + n].set(1.0)
        ck = ck.at[0, base + 3 + n].set(1.0)
    return eq.astype(BF16), ek.astype(BF16), cq, ck


def _inproj(x, g, w_main, w_vt, w_logit, b_logit, cosm, sina, sinb):
    b, s, d = x.shape
    tm, tk = TM_PROJ, TQ
    eq, ek, cq, ck = _aux_tables()
    act = jax.ShapeDtypeStruct((b, s, FOX_WIDTH), BF16)
    vt = jax.ShapeDtypeStruct((b, s // tk, FOX_WIDTH, tk), BF16)
    out_shape = [act, act, act, act, vt, act, act, vt,
                 jax.ShapeDtypeStruct((b, s // tm, SUBLANES, MOBA_WIDTH), F32)]
    act_spec = pl.BlockSpec((1, tm, FOX_WIDTH), lambda bi, j: (bi, j, 0))
    vt_spec = pl.BlockSpec((1, tm // tk, FOX_WIDTH, tk), lambda bi, j: (bi, j, 0, 0))
    tab_spec = pl.BlockSpec((tm, LANES), lambda bi, j: (j, 0))
    const2 = lambda bi, j: (0, 0)
    const3 = lambda bi, j: (0, 0, 0)
    *acts, kmean = pl.pallas_call(
        _inproj_kernel,
        grid=(b, s // tm),
        in_specs=[pl.BlockSpec((1, tm, d), lambda bi, j: (bi, j, 0)),
                  pl.BlockSpec((1, d), const2),
                  pl.BlockSpec(w_main.shape, const2),
                  pl.BlockSpec(w_vt.shape, const3),
                  pl.BlockSpec(w_logit.shape, const2),
                  pl.BlockSpec((1, LANES), const2),
                  pl.BlockSpec(eq.shape, const3), pl.BlockSpec(ek.shape, const3),
                  pl.BlockSpec(cq.shape, const2), pl.BlockSpec(ck.shape, const2),
                  tab_spec, tab_spec, tab_spec],
        out_specs=[act_spec, act_spec, act_spec, act_spec, vt_spec, act_spec, act_spec, vt_spec,
                   pl.BlockSpec((1, 1, SUBLANES, MOBA_WIDTH), lambda bi, j: (bi, j, 0, 0))],
        out_shape=out_shape,
        scratch_shapes=[pltpu.VMEM((1, LANES), F32)],
        compiler_params=_params(("arbitrary", "arbitrary")),
        name="inproj",
    )(x, g, w_main, w_vt, w_logit, b_logit, eq, ek, cq, ck, cosm, sina, sinb)
    kmean = kmean[:, :, :tm // MOBA_BLOCK].reshape(b, s // MOBA_BLOCK, MOBA_WIDTH)
    return (*acts, kmean)


def _attend(scores, vt, m_ref, l_ref, acc_ref):
    ones = jnp.ones((BF16_SUBLANES, vt.shape[1]), BF16)
    stats = []
    for hh, s in enumerate(scores):
        m_prev = m_ref[hh]
        m_new = jnp.maximum(m_prev, jnp.max(s, axis=0, keepdims=True))
        m_ref[hh] = m_new
        stats.append((jnp.exp2(m_prev - m_new), m_new))
    probs = [jnp.exp2(s - stats[hh][1]).astype(BF16) for hh, s in enumerate(scores)]
    for hh, p in enumerate(probs):
        rows = slice(hh * HEAD_DIM, (hh + 1) * HEAD_DIM)
        alpha = stats[hh][0]
        pv = _dot(jnp.concatenate([vt[rows, :], ones], axis=0), p)
        acc_ref[rows, :] = alpha * acc_ref[rows, :] + pv[:HEAD_DIM]
        l_ref[hh] = alpha * l_ref[hh] + pv[HEAD_DIM:HEAD_DIM + 1]


def _attend_tiles(i, scores_of, vt_ref, m_ref, l_ref, acc_ref):
    def body(kt, c):
        _attend(scores_of(kt, False), vt_ref[0, kt], m_ref, l_ref, acc_ref)
        return c

    _attend(scores_of(i, True), vt_ref[0, i], m_ref, l_ref, acc_ref)
    lax.fori_loop(0, i, body, 0)


def _attn_init(m_ref, l_ref, acc_ref):
    m_ref[...] = jnp.full(m_ref.shape, NEG_INF, F32)
    l_ref[...] = jnp.zeros_like(l_ref)
    acc_ref[...] = jnp.zeros_like(acc_ref)


def _attn_finish(o_ref, l_ref, acc_ref):
    out_t = jnp.concatenate(
        [acc_ref[hh * HEAD_DIM:(hh + 1) * HEAD_DIM, :] / l_ref[hh] for hh in range(HEADS_PER_STEP)],
        axis=0)
    o_ref[0] = out_t.T


def _block(a, g):
    return a[:, g * LANES:(g + 1) * LANES]


def _per_head(a, width):
    first = lax.broadcasted_iota(I32, (a.shape[0], LANES), 1) < width
    zero = jnp.zeros((a.shape[0], LANES), a.dtype)
    out = []
    for g in range(BLOCKS_PER_STEP):
        blk = _block(a, g)
        out += [jnp.where(first, blk, zero), jnp.where(first, zero, blk)]
    return out


def _key_le_query(tq):
    return lax.broadcasted_iota(I32, (tq, tq), 0) <= lax.broadcasted_iota(I32, (tq, tq), 1)


def _fox_kernel(q_ref, qa_ref, k_ref, ka_ref, vt_ref, o_ref, m_ref, l_ref, acc_ref):
    i = pl.program_id(2)
    tq = q_ref.shape[1]
    qq = [jnp.concatenate([qm, am], axis=1)
          for qm, am in zip(_per_head(q_ref[0], HEAD_DIM), _per_head(qa_ref[0], AUX_PER_HEAD))]
    causal = _key_le_query(tq)
    _attn_init(m_ref, l_ref, acc_ref)

    def scores_of(kt, diag):
        ks = pl.multiple_of(kt * tq, tq)
        k, ka = k_ref[0, pl.ds(ks, tq), :], ka_ref[0, pl.ds(ks, tq), :]
        kk = [jnp.concatenate([_block(k, g), _block(ka, g)], axis=1) for g in range(BLOCKS_PER_STEP)]
        scores = [_dot_nt(kk[hh // HEADS_PER_BLOCK], qq[hh])
                  for hh in range(HEADS_PER_STEP)]
        if diag:
            scores = [jnp.where(causal, s, NEG_INF) for s in scores]
        return tuple(scores)

    _attend_tiles(i, scores_of, vt_ref, m_ref, l_ref, acc_ref)
    _attn_finish(o_ref, l_ref, acc_ref)


def _moba_kernel(q_ref, k_ref, vt_ref, kmean_ref, o_ref, m_ref, l_ref, acc_ref, bias_ref):
    i = pl.program_id(2)
    tq = q_ref.shape[1]
    nblk = kmean_ref.shape[1]
    qs = _per_head(q_ref[0], HEAD_DIM)
    causal = _key_le_query(tq)
    _attn_init(m_ref, l_ref, acc_ref)

    km_parts = _split3(kmean_ref[0])
    blk = lax.broadcasted_iota(I32, (nblk, tq), 0).astype(F32)
    past = blk < i.astype(F32)
    for hh in range(HEADS_PER_STEP):
        gate = sum(_dot_nt(_block(part, hh // HEADS_PER_BLOCK), qs[hh]) for part in km_parts)
        sel = jnp.zeros((nblk, tq), jnp.bool_)
        for _ in range(MOBA_TOPK):
            remaining = jnp.logical_and(past, jnp.logical_not(sel))
            g = jnp.where(remaining, gate, NEG_INF)
            first = jnp.min(jnp.where(g == jnp.max(g, axis=0, keepdims=True), blk, float(nblk)),
                            axis=0, keepdims=True)
            sel = jnp.logical_or(sel, jnp.logical_and(blk == first, remaining))
        bias_ref[hh] = jnp.where(sel, 0.0, NEG_INF)

    def scores_of(kt, diag):
        ks = pl.multiple_of(kt * tq, tq)
        k = k_ref[0, pl.ds(ks, tq), :]
        scores = [_dot_nt(_block(k, hh // HEADS_PER_BLOCK), qs[hh]) for hh in range(HEADS_PER_STEP)]
        if diag:
            scores = [jnp.where(causal, s, NEG_INF) for s in scores]
        else:
            scores = [s + bias_ref[hh, pl.ds(kt, 1), :] for hh, s in enumerate(scores)]
        return tuple(scores)

    _attend_tiles(i, scores_of, vt_ref, m_ref, l_ref, acc_ref)
    _attn_finish(o_ref, l_ref, acc_ref)


def _attn_scratch(tq):
    return [pltpu.VMEM((HEADS_PER_STEP, 1, tq), F32), pltpu.VMEM((HEADS_PER_STEP, 1, tq), F32),
            pltpu.VMEM((STEP_LANES, tq), F32)]


def _attn_specs(s, tq):
    q_spec = pl.BlockSpec((1, tq, STEP_LANES), lambda bi, hb, i: (bi, i, hb))
    k_spec = pl.BlockSpec((1, s, STEP_LANES), lambda bi, hb, i: (bi, 0, hb))
    vt_spec = pl.BlockSpec((1, s // tq, STEP_LANES, tq), lambda bi, hb, i: (bi, 0, hb, 0))
    return q_spec, k_spec, vt_spec


def _fox(q, qa, k, ka, vt):
    b, s, width = q.shape
    tq = TQ
    q_spec, k_spec, vt_spec = _attn_specs(s, tq)
    return pl.pallas_call(
        _fox_kernel,
        grid=(b, width // STEP_LANES, s // tq),
        in_specs=[q_spec, q_spec, k_spec, k_spec, vt_spec],
        out_specs=q_spec,
        out_shape=jax.ShapeDtypeStruct((b, s, width), F32),
        scratch_shapes=_attn_scratch(tq),
        compiler_params=_params(("arbitrary", "arbitrary", "arbitrary")),
        name="fox",
    )(q, qa, k, ka, vt)


def _moba(q, k, vt, kmean):
    b, s, width = q.shape
    tq = TQ
    nblk = kmean.shape[1]
    q_spec, k_spec, vt_spec = _attn_specs(s, tq)
    return pl.pallas_call(
        _moba_kernel,
        grid=(b, width // STEP_LANES, s // tq),
        in_specs=[q_spec, k_spec, vt_spec,
                  pl.BlockSpec((1, nblk, STEP_LANES), lambda bi, hb, i: (bi, 0, hb))],
        out_specs=q_spec,
        out_shape=jax.ShapeDtypeStruct((b, s, width), F32),
        scratch_shapes=_attn_scratch(tq) + [pltpu.VMEM((HEADS_PER_STEP, nblk, tq), F32)],
        compiler_params=_params(("arbitrary", "arbitrary", "arbitrary")),
        name="moba",
    )(q, k, vt, kmean)


def _postattn_kernel(x_ref, fox_ref, moba_ref, gf_ref, gm_ref, wo_ref, gn_ref, wr_ref, br_ref,
                     x2_ref, h2_ref, idx_ref, wts_ref, rank_ref, cnt_ref, carry_ref):
    t = pl.program_id(0)
    tm = x_ref.shape[0]
    fw = fox_ref.shape[1]
    mixed_f = _rms(fox_ref[...], gf_ref[...]).astype(BF16)
    mixed_m = _rms(moba_ref[...], gm_ref[...]).astype(BF16)
    x2 = x_ref[...] + _dot(mixed_f, wo_ref[:fw, :]) + _dot(mixed_m, wo_ref[fw:, :])
    x2_ref[...] = x2
    h2 = _rms(x2, gn_ref[...])
    _store_rows(h2_ref, h2)

    h_hi = h2.astype(BF16)
    h_lo = (h2 - h_hi.astype(F32)).astype(BF16)
    wr = wr_ref[...]
    w_hi = wr.astype(BF16)
    w_lo = (wr - w_hi.astype(F32)).astype(BF16)
    logits = _dot(h_hi, w_hi) + _dot(h_hi, w_lo) + _dot(h_lo, w_hi) + br_ref[...]
    lane = lax.broadcasted_iota(I32, (tm, LANES), 1).astype(F32)

    def first_max(vals):
        mx = jnp.max(vals, axis=1, keepdims=True)
        return mx, jnp.min(jnp.where(vals == mx, lane, float(LANES)), axis=1, keepdims=True)

    gl = jnp.where(lane < N_GROUPS, logits, NEG_INF)
    gmax, g_idx = first_max(gl)
    g_top = 1.0 / jnp.sum(jnp.exp(gl - gmax), axis=1, keepdims=True)
    e_lo = N_GROUPS + EXPERTS_PER_GROUP * g_idx
    el = jnp.where(jnp.logical_and(lane >= e_lo, lane < e_lo + EXPERTS_PER_GROUP), logits, NEG_INF)
    emax, i1 = first_max(el)
    esum = jnp.sum(jnp.exp(el - emax), axis=1, keepdims=True)
    e2max, i2 = first_max(jnp.where(lane == i1, NEG_INF, el))
    p1 = 1.0 / esum
    p2 = jnp.exp(e2max - emax) / esum
    w1 = p1 / (p1 + p2) * g_top
    w2 = p2 / (p1 + p2) * g_top
    e1 = i1 - N_GROUPS
    e2 = i2 - N_GROUPS
    idx_ref[...] = jnp.where(lane == 0.0, e1, e2)[:, :2].astype(I32)
    wts_ref[...] = jnp.where(lane == 0.0, w1, w2)[:, :2]

    @pl.when(t == 0)
    def _():
        carry_ref[...] = jnp.zeros_like(carry_ref)

    row = lax.broadcasted_iota(I32, (tm, tm), 0)
    col = lax.broadcasted_iota(I32, (tm, tm), 1)
    strict = jnp.where(row > col, 1.0, 0.0).astype(BF16)
    hit1 = lane == e1
    hit2 = lane == e2
    oh1 = jnp.where(hit1, 1.0, 0.0)
    oh2 = jnp.where(hit2, 1.0, 0.0)
    tot1 = jnp.sum(oh1, axis=0, keepdims=True)
    tot2 = jnp.sum(oh2, axis=0, keepdims=True)
    base = carry_ref[...]
    before1 = _dot(strict, oh1.astype(BF16)) + base
    before2 = _dot(strict, oh2.astype(BF16)) + (base + tot1)
    r1 = jnp.sum(jnp.where(hit1, before1, 0.0), axis=1, keepdims=True)
    r2 = jnp.sum(jnp.where(hit2, before2, 0.0), axis=1, keepdims=True)
    rank_ref[...] = jnp.where(lane == 0.0, r1, r2)[:, :2].astype(I32)
    total = base + tot1 + tot2
    carry_ref[...] = total
    cnt_ref[...] = total.astype(I32)


def _postattn(x, fox, moba, gf, gm, wo, gn, wr, br):
    t, d = x.shape
    tm = TM_PROJ
    fw = fox.shape[1]
    const = lambda i: (0, 0)
    rows = lambda i: (i, 0)
    pair = pl.BlockSpec((tm, 2), rows)
    return pl.pallas_call(
        _postattn_kernel,
        grid=(t // tm,),
        in_specs=[pl.BlockSpec((tm, d), rows), pl.BlockSpec((tm, fw), rows),
                  pl.BlockSpec((tm, moba.shape[1]), rows),
                  pl.BlockSpec((1, fw), const), pl.BlockSpec((1, moba.shape[1]), const),
                  pl.BlockSpec(wo.shape, const), pl.BlockSpec((1, d), const),
                  pl.BlockSpec(wr.shape, const), pl.BlockSpec((1, LANES), const)],
        out_specs=[pl.BlockSpec((tm, d), rows), pl.BlockSpec((tm * SUBLANES, LANES), rows),
                   pair, pair, pair, pl.BlockSpec((1, LANES), const)],
        out_shape=[jax.ShapeDtypeStruct((t, d), F32), jax.ShapeDtypeStruct((t * SUBLANES, LANES), F32),
                   jax.ShapeDtypeStruct((t, 2), I32), jax.ShapeDtypeStruct((t, 2), F32),
                   jax.ShapeDtypeStruct((t, 2), I32), jax.ShapeDtypeStruct((1, LANES), I32)],
        scratch_shapes=[pltpu.VMEM((1, LANES), F32)],
        compiler_params=_params(("arbitrary",)),
        name="postattn",
    )(x, fox, moba, gf, gm, wo, gn, wr, br)


def _store_rows(ref, val):
    for g in range(SUBLANES):
        ref[pl.ds(g, val.shape[0], stride=SUBLANES), :] = val[:, g * LANES:(g + 1) * LANES]


def _load_rows(ref):
    tokens = ref.shape[0] // SUBLANES
    return jnp.concatenate([ref[pl.ds(g, tokens, stride=SUBLANES), :] for g in range(SUBLANES)], axis=1)


def _row_copy(src, src_row, dst, dst_row, sem):
    rows = lambda r: pl.ds(pl.multiple_of(r * SUBLANES, SUBLANES), SUBLANES)
    return pltpu.make_async_copy(src.at[rows(src_row)], dst.at[rows(dst_row)], sem)


def _dispatch_kernel(dest_ref, h_ref, init_ref, xs_ref, sem):
    del init_ref
    n = dest_ref.shape[2]

    def issue(a, c):
        _row_copy(h_ref, a // 2, xs_ref, dest_ref[0, 0, a], sem).start()
        return c

    lax.fori_loop(0, n, issue, 0, unroll=ISSUE_UNROLL)

    def drain(a, c):
        _row_copy(h_ref, 0, xs_ref, 0, sem).wait()
        return c

    lax.fori_loop(0, n, drain, 0, unroll=ISSUE_UNROLL)


def _dispatch(dest, h2, n_rows):
    t = h2.shape[0] // SUBLANES
    tm = TM_ROWS
    dest3 = dest.reshape(t // tm, 1, 2 * tm)
    return pl.pallas_call(
        _dispatch_kernel,
        grid=(t // tm,),
        in_specs=[pl.BlockSpec((1, 1, 2 * tm), lambda i: (i, 0, 0), memory_space=pltpu.SMEM),
                  pl.BlockSpec((tm * SUBLANES, LANES), lambda i: (i, 0)),
                  pl.BlockSpec(memory_space=pl.ANY)],
        out_specs=pl.BlockSpec(memory_space=pl.ANY),
        out_shape=jax.ShapeDtypeStruct((n_rows * SUBLANES, LANES), F32),
        scratch_shapes=[pltpu.SemaphoreType.DMA(())],
        input_output_aliases={2: 0},
        compiler_params=_params(("arbitrary",)),
        name="dispatch",
    )(dest3, h2, jnp.zeros((n_rows * SUBLANES, LANES), F32))


def _experts_kernel(te_ref, ts_ref, nv_ref, xs_ref, wg_ref, wu_ref, wd_ref, ys_ref):
    del te_ref, ts_ref
    t = pl.program_id(0)

    @pl.when(t < nv_ref[0])
    def _():
        xb = _load_rows(xs_ref).astype(BF16)
        a = _dot(xb, wg_ref[0])
        u = _dot(xb, wu_ref[0])
        act = (a * jax.nn.sigmoid(a) * u).astype(BF16)
        _store_rows(ys_ref, _dot(act, wd_ref[0]))

    @pl.when(t >= nv_ref[0])
    def _():
        ys_ref[...] = jnp.zeros_like(ys_ref)


def _experts(tile_expert, tile_src, n_valid, xs, wg, wu, wd):
    tm = TM_EXPERT
    n_tiles = xs.shape[0] // (tm * SUBLANES)
    _, d, f = wg.shape
    row_block = (tm * SUBLANES, LANES)
    grid_spec = pltpu.PrefetchScalarGridSpec(
        num_scalar_prefetch=3,
        grid=(n_tiles,),
        in_specs=[pl.BlockSpec(row_block, lambda t, te, ts, nv: (ts[t], 0)),
                  pl.BlockSpec((1, d, f), lambda t, te, ts, nv: (te[t], 0, 0)),
                  pl.BlockSpec((1, d, f), lambda t, te, ts, nv: (te[t], 0, 0)),
                  pl.BlockSpec((1, f, d), lambda t, te, ts, nv: (te[t], 0, 0))],
        out_specs=pl.BlockSpec(row_block, lambda t, te, ts, nv: (t, 0)),
    )
    return pl.pallas_call(
        _experts_kernel,
        grid_spec=grid_spec,
        out_shape=jax.ShapeDtypeStruct(xs.shape, F32),
        compiler_params=_params(("arbitrary",)),
        name="experts",
    )(tile_expert, tile_src, n_valid, xs, wg, wu, wd)


def _combine_kernel(dest_ref, x2_ref, wts_ref, g_ref, ys_ref, o_ref, buf_ref, sem):
    n = dest_ref.shape[2]

    def issue(a, c):
        _row_copy(ys_ref, dest_ref[0, 0, a], buf_ref.at[a % 2], a // 2, sem).start()
        return c

    lax.fori_loop(0, n, issue, 0, unroll=ISSUE_UNROLL)

    def drain(a, c):
        _row_copy(ys_ref, 0, buf_ref.at[0], 0, sem).wait()
        return c

    lax.fori_loop(0, n, drain, 0, unroll=ISSUE_UNROLL)
    w = wts_ref[...]
    y = x2_ref[...] + w[:, 0:1] * _load_rows(buf_ref.at[0]) + w[:, 1:2] * _load_rows(buf_ref.at[1])
    o_ref[...] = _rms(y, g_ref[...])


def _combine(dest, x2, wts, g, ys):
    t, d = x2.shape
    tm = TM_ROWS
    dest3 = dest.reshape(t // tm, 1, 2 * tm)
    rows = lambda i: (i, 0)
    return pl.pallas_call(
        _combine_kernel,
        grid=(t // tm,),
        in_specs=[pl.BlockSpec((1, 1, 2 * tm), lambda i: (i, 0, 0), memory_space=pltpu.SMEM),
                  pl.BlockSpec((tm, d), rows), pl.BlockSpec((tm, 2), rows),
                  pl.BlockSpec((1, d), lambda i: (0, 0)),
                  pl.BlockSpec(memory_space=pl.ANY)],
        out_specs=pl.BlockSpec((tm, d), rows),
        out_shape=jax.ShapeDtypeStruct((t, d), F32),
        scratch_shapes=[pltpu.VMEM((2, tm * SUBLANES, LANES), F32), pltpu.SemaphoreType.DMA(())],
        compiler_params=_params(("arbitrary",)),
        name="combine",
    )(dest3, x2, wts, g, ys)


def _rotary_tables(seq):
    half = ROPE_DIM // 2
    inv_freq = ROPE_THETA ** (-jnp.arange(half, dtype=F32) / half)
    ang = jnp.arange(seq, dtype=F32)[:, None] * inv_freq[None, :]
    cos, sin = jnp.cos(ang), jnp.sin(ang)
    ones = jnp.ones((seq, HEAD_DIM - ROPE_DIM), F32)
    zeros = jnp.zeros((seq, HEAD_DIM - ROPE_DIM), F32)
    zh = jnp.zeros((seq, half), F32)
    cosm = jnp.concatenate([cos, cos, ones], axis=1)
    sina = jnp.concatenate([-sin, zh, zeros], axis=1)
    sinb = jnp.concatenate([zh, sin, zeros], axis=1)
    tile = lambda a: jnp.tile(a, (1, HEADS_PER_BLOCK))
    return tile(cosm), tile(sina), tile(sinb)


def _pad_lanes(a):
    return jnp.pad(a, ((0, 0), (0, LANES - a.shape[1])))


def kernel(x, norm_mix_g, w_in, b_forget, fox_out_g, moba_out_g, w_out, norm_ffn_g, w_router_group,
           b_router_group, w_router_expert, b_router_expert, w_gate, w_up, w_down, norm_final_g):
    b, s, d = x.shape
    t = b * s
    assert w_in.shape[0] == 1, "the closing RMSNorm is fused into the only layer's combine step"
    cosm, sina, sinb = _rotary_tables(s)
    n_tiles = (2 * t) // TM_EXPERT + N_EXPERTS
    fw3 = 3 * FOX_WIDTH
    m0 = fw3 + N_FOX_HEADS
    wl = w_in[0]
    w_main = jnp.concatenate([wl[:, :2 * FOX_WIDTH], wl[:, m0:m0 + 2 * MOBA_WIDTH]],
                             axis=1).astype(BF16)
    w_vt = jnp.stack([wl[:, 2 * FOX_WIDTH:fw3].T, wl[:, m0 + 2 * MOBA_WIDTH:].T]).astype(BF16)
    w_logit = _pad_lanes(wl[:, fw3:m0]).astype(BF16)
    b_logit = _pad_lanes(b_forget[0][None, :])
    fq, fqa, fk, fka, fvt, mq, mk, mvt, kmean = _inproj(
        x, norm_mix_g[0][None, :], w_main, w_vt, w_logit, b_logit, cosm, sina, sinb)
    fox = _fox(fq, fqa, fk, fka, fvt)
    moba = _moba(mq, mk, mvt, kmean)

    w_router = _pad_lanes(jnp.concatenate(
        [w_router_group[0], w_router_expert[0].reshape(d, N_EXPERTS)], axis=1))
    b_router = _pad_lanes(jnp.concatenate(
        [b_router_group[0], b_router_expert[0].reshape(N_EXPERTS)])[None, :])
    x2, h2, idx, wts, rank, counts = _postattn(
        x.reshape(t, d), fox.reshape(t, FOX_WIDTH), moba.reshape(t, MOBA_WIDTH),
        fox_out_g[0][None, :], moba_out_g[0][None, :], w_out[0].astype(BF16),
        norm_ffn_g[0][None, :], w_router, b_router)

    counts = counts[0, :N_EXPERTS]
    padded = (counts + TM_EXPERT - 1) // TM_EXPERT * TM_EXPERT
    ends = jnp.cumsum(padded)
    dest = (ends - padded)[idx] + rank
    n_valid = ends[-1] // TM_EXPERT
    tile_src = jnp.minimum(jnp.arange(n_tiles, dtype=I32), n_valid - 1)
    tile_expert = jnp.sum(ends[None, :] <= (tile_src * TM_EXPERT)[:, None], axis=1).astype(I32)
    xs = _dispatch(dest, h2, n_tiles * TM_EXPERT)
    ys = _experts(tile_expert, tile_src, n_valid.reshape(1).astype(I32), xs,
                  w_gate[0].astype(BF16), w_up[0].astype(BF16), w_down[0].astype(BF16))
    return _combine(dest, x2, wts, norm_final_g[None, :], ys).reshape(b, s, d)
```

```python
import math

import jax
import jax.numpy as jnp
from jax import lax
from jax.experimental import pallas as pl
from jax.experimental.pallas import tpu as pltpu

F32 = jnp.float32
BF16 = jnp.bfloat16
I32 = jnp.int32

HEAD_DIM = 64
N_FOX_HEADS = 8
N_MOBA_HEADS = 8
FOX_WIDTH = N_FOX_HEADS * HEAD_DIM
MOBA_WIDTH = N_MOBA_HEADS * HEAD_DIM
MOBA_BLOCK = 256
MOBA_TOPK = 3
ROPE_THETA = 500000.0
ROPE_DIM = HEAD_DIM // 4
N_GROUPS = 4
EXPERTS_PER_GROUP = 8
N_EXPERTS = N_GROUPS * EXPERTS_PER_GROUP
EPS = 1e-6

LANES = 128
SUBLANES = 8
BF16_SUBLANES = 16
LOG2_E = math.log2(math.e)
HEADS_PER_BLOCK = LANES // HEAD_DIM
BLOCKS_PER_STEP = 4
HEADS_PER_STEP = HEADS_PER_BLOCK * BLOCKS_PER_STEP
STEP_LANES = LANES * BLOCKS_PER_STEP
VMEM_LIMIT = 56 * 1024 * 1024
AUX_PER_HEAD = 6

TM_PROJ = 512
TQ = 256
TM_EXPERT = 256
TM_ROWS = 256
ISSUE_UNROLL = 8

NEG_INF = float("-inf")


def _params(sem):
    return pltpu.CompilerParams(dimension_semantics=sem, vmem_limit_bytes=VMEM_LIMIT)


def _rms(x, g):
    return x * lax.rsqrt(jnp.mean(x * x, axis=-1, keepdims=True) + EPS) * g


def _split3(x):
    hi = x.astype(BF16)
    r = x - hi.astype(F32)
    mid = r.astype(BF16)
    lo = (r - mid.astype(F32)).astype(BF16)
    return hi, mid, lo


def _dot(a, b):
    return jnp.dot(a, b, preferred_element_type=F32)


def _dot_nt(a, b):
    return lax.dot_general(a, b, (((1,), (1,)), ((), ())), preferred_element_type=F32)


def _inproj_kernel(x_ref, g_ref, w_ref, wvt_ref, wl_ref, bf_ref, kind_ref, cosm_ref, sina_ref, sinb_ref,
                   fq_ref, fqa_ref, fk_ref, fka_ref, fvt_ref, mq_ref, mk_ref, mvt_ref, kmean_ref,
                   carry_ref):
    j = pl.program_id(1)
    tm = x_ref.shape[1]
    tk = fvt_ref.shape[3]
    h = _rms(x_ref[0], g_ref[...]).astype(BF16)
    scale = HEAD_DIM ** -0.5 * LOG2_E

    def proj(seg):
        return _dot(h, w_ref[:, seg * FOX_WIDTH:(seg + 1) * FOX_WIDTH])

    fq_ref[0] = (proj(0) * scale).astype(BF16)
    fk_ref[0] = proj(1).astype(BF16)

    for vt_ref, seg in ((fvt_ref, 0), (mvt_ref, 1)):
        vt = _dot_nt(wvt_ref[seg], h).astype(BF16)
        for r in range(tm // tk):
            vt_ref[0, r] = vt[:, r * tk:(r + 1) * tk]

    cosm, sina, sinb = cosm_ref[...], sina_ref[...], sinb_ref[...]

    def rotary(t):
        outs = []
        for g in range(MOBA_WIDTH // LANES):
            tg = t[:, g * LANES:(g + 1) * LANES]
            outs.append(tg * cosm + pltpu.roll(tg, LANES - ROPE_DIM // 2, 1) * sina
                        + pltpu.roll(tg, ROPE_DIM // 2, 1) * sinb)
        return jnp.concatenate(outs, axis=1)

    mq_ref[0] = (rotary(proj(2)) * scale).astype(BF16)
    mk = rotary(proj(3))
    mk_ref[0] = mk.astype(BF16)
    nblk_tile = tm // MOBA_BLOCK
    means = [jnp.mean(mk[r * MOBA_BLOCK:(r + 1) * MOBA_BLOCK], axis=0, keepdims=True)
             for r in range(nblk_tile)]
    means += [jnp.zeros_like(means[0])] * (kmean_ref.shape[2] - nblk_tile)
    kmean_ref[0, 0] = jnp.concatenate(means, axis=0)

    z = _dot(h, wl_ref[...]) + bf_ref[...]
    log_f = jnp.minimum(z, 0.0) - jnp.log1p(jnp.exp(-jnp.abs(z)))

    @pl.when(j == 0)
    def _():
        carry_ref[...] = jnp.zeros_like(carry_ref)

    half = tm // 2
    row = lax.broadcasted_iota(I32, (half, half), 0)
    col = lax.broadcasted_iota(I32, (half, half), 1)
    tri = jnp.where(row >= col, 1.0, 0.0).astype(BF16)
    pieces = jnp.concatenate(_split3(log_f), axis=1)
    carry = carry_ref[...]
    cs = []
    for r in range(2):
        local = _dot(tri, pieces[r * half:(r + 1) * half])
        cs.append(local[:, :LANES] + local[:, LANES:2 * LANES] + local[:, 2 * LANES:] + carry)
        carry = cs[-1][half - 1:half, :]
    carry_ref[...] = carry
    c = jnp.concatenate(cs, axis=0) * LOG2_E

    hi = c.astype(BF16).astype(F32)
    mid = (c - hi).astype(BF16).astype(F32)
    lo = c - hi - mid
    kind = kind_ref[...]
    one = jnp.where(kind < AUX_PER_HEAD, 1.0, 0.0)
    pick = lambda base: jnp.where(kind == base, hi, jnp.where(kind == base + 1, mid,
                                  jnp.where(kind == base + 2, lo, 0.0)))
    fqa_ref[0] = (pick(3) + jnp.where(kind < 3, one, 0.0)).astype(BF16)
    fka_ref[0] = (jnp.where(kind >= 3, one, 0.0) - pick(0)).astype(BF16)


def _inproj(x, g, w_main, w_vt, w_logit, b_logit, cosm, sina, sinb):
    b, s, d = x.shape
    tm, tk = TM_PROJ, TQ
    lane = jnp.arange(LANES, dtype=I32)
    kind = jnp.where(lane < AUX_PER_HEAD * N_FOX_HEADS, lane % AUX_PER_HEAD, AUX_PER_HEAD)[None, :]
    act = jax.ShapeDtypeStruct((b, s, FOX_WIDTH), BF16)
    aux = jax.ShapeDtypeStruct((b, s, LANES), BF16)
    vt = jax.ShapeDtypeStruct((b, s // tk, FOX_WIDTH, tk), BF16)
    out_shape = [act, aux, act, aux, vt, act, act, vt,
                 jax.ShapeDtypeStruct((b, s // tm, SUBLANES, MOBA_WIDTH), F32)]
    act_spec = pl.BlockSpec((1, tm, FOX_WIDTH), lambda bi, j: (bi, j, 0))
    aux_spec = pl.BlockSpec((1, tm, LANES), lambda bi, j: (bi, j, 0))
    vt_spec = pl.BlockSpec((1, tm // tk, FOX_WIDTH, tk), lambda bi, j: (bi, j, 0, 0))
    tab_spec = pl.BlockSpec((tm, LANES), lambda bi, j: (j, 0))
    const2 = lambda bi, j: (0, 0)
    const3 = lambda bi, j: (0, 0, 0)
    *acts, kmean = pl.pallas_call(
        _inproj_kernel,
        grid=(b, s // tm),
        in_specs=[pl.BlockSpec((1, tm, d), lambda bi, j: (bi, j, 0)),
                  pl.BlockSpec((1, d), const2),
                  pl.BlockSpec(w_main.shape, const2),
                  pl.BlockSpec(w_vt.shape, const3),
                  pl.BlockSpec(w_logit.shape, const2),
                  pl.BlockSpec((1, LANES), const2), pl.BlockSpec((1, LANES), const2),
                  tab_spec, tab_spec, tab_spec],
        out_specs=[act_spec, aux_spec, act_spec, aux_spec, vt_spec, act_spec, act_spec, vt_spec,
                   pl.BlockSpec((1, 1, SUBLANES, MOBA_WIDTH), lambda bi, j: (bi, j, 0, 0))],
        out_shape=out_shape,
        scratch_shapes=[pltpu.VMEM((1, LANES), F32)],
        compiler_params=_params(("arbitrary", "arbitrary")),
        name="inproj",
    )(x, g, w_main, w_vt, w_logit, b_logit, kind, cosm, sina, sinb)
    kmean = kmean[:, :, :tm // MOBA_BLOCK].reshape(b, s // MOBA_BLOCK, MOBA_WIDTH)
    return (*acts, kmean)


def _attend(scores, vt, m_ref, l_ref, acc_ref):
    ones = jnp.ones((BF16_SUBLANES, vt.shape[1]), BF16)
    stats = []
    for hh, s in enumerate(scores):
        m_prev = m_ref[hh]
        m_new = jnp.maximum(m_prev, jnp.max(s, axis=0, keepdims=True))
        m_ref[hh] = m_new
        stats.append((jnp.exp2(m_prev - m_new), m_new))
    probs = [jnp.exp2(s - stats[hh][1]).astype(BF16) for hh, s in enumerate(scores)]
    for hh, p in enumerate(probs):
        rows = slice(hh * HEAD_DIM, (hh + 1) * HEAD_DIM)
        alpha = stats[hh][0]
        pv = _dot(jnp.concatenate([vt[rows, :], ones], axis=0), p)
        acc_ref[rows, :] = alpha * acc_ref[rows, :] + pv[:HEAD_DIM]
        l_ref[hh] = alpha * l_ref[hh] + pv[HEAD_DIM:HEAD_DIM + 1]


def _attend_tiles(i, scores_of, vt_ref, m_ref, l_ref, acc_ref):
    def body(kt, c):
        _attend(scores_of(kt, False), vt_ref[0, kt], m_ref, l_ref, acc_ref)
        return c

    _attend(scores_of(i, True), vt_ref[0, i], m_ref, l_ref, acc_ref)
    lax.fori_loop(0, i, body, 0)


def _attn_init(m_ref, l_ref, acc_ref):
    m_ref[...] = jnp.full(m_ref.shape, NEG_INF, F32)
    l_ref[...] = jnp.zeros_like(l_ref)
    acc_ref[...] = jnp.zeros_like(acc_ref)


def _attn_finish(o_ref, l_ref, acc_ref):
    out_t = jnp.concatenate(
        [acc_ref[hh * HEAD_DIM:(hh + 1) * HEAD_DIM, :] / l_ref[hh] for hh in range(HEADS_PER_STEP)],
        axis=0)
    o_ref[0] = out_t.T


def _block(a, g):
    return a[:, g * LANES:(g + 1) * LANES]


def _per_head(a, width):
    first = lax.broadcasted_iota(I32, (a.shape[0], LANES), 1) < width
    zero = jnp.zeros((a.shape[0], LANES), a.dtype)
    out = []
    for g in range(BLOCKS_PER_STEP):
        blk = _block(a, g)
        out += [jnp.where(first, blk, zero), jnp.where(first, zero, blk)]
    return out


def _key_le_query(tq):
    return lax.broadcasted_iota(I32, (tq, tq), 0) <= lax.broadcasted_iota(I32, (tq, tq), 1)


def _fox_kernel(q_ref, qa_ref, k_ref, ka_ref, vt_ref, o_ref, m_ref, l_ref, acc_ref):
    i = pl.program_id(2)
    tq = q_ref.shape[1]
    qa = qa_ref[0]
    lane = lax.broadcasted_iota(I32, qa.shape, 1) - pl.program_id(1) * (HEADS_PER_STEP * AUX_PER_HEAD)
    own_aux = lambda hh: jnp.logical_and(lane >= hh * AUX_PER_HEAD, lane < (hh + 1) * AUX_PER_HEAD)
    qq = [jnp.concatenate([qm, jnp.where(own_aux(hh), qa, jnp.zeros_like(qa))], axis=1)
          for hh, qm in enumerate(_per_head(q_ref[0], HEAD_DIM))]
    causal = _key_le_query(tq)
    _attn_init(m_ref, l_ref, acc_ref)

    def scores_of(kt, diag):
        ks = pl.multiple_of(kt * tq, tq)
        k, ka = k_ref[0, pl.ds(ks, tq), :], ka_ref[0, pl.ds(ks, tq), :]
        kk = [jnp.concatenate([_block(k, g), ka], axis=1) for g in range(BLOCKS_PER_STEP)]
        scores = [_dot_nt(kk[hh // HEADS_PER_BLOCK], qq[hh])
                  for hh in range(HEADS_PER_STEP)]
        if diag:
            scores = [jnp.where(causal, s, NEG_INF) for s in scores]
        return tuple(scores)

    _attend_tiles(i, scores_of, vt_ref, m_ref, l_ref, acc_ref)
    _attn_finish(o_ref, l_ref, acc_ref)


def _moba_kernel(q_ref, k_ref, vt_ref, kmean_ref, o_ref, m_ref, l_ref, acc_ref, bias_ref):
    i = pl.program_id(2)
    tq = q_ref.shape[1]
    nblk = kmean_ref.shape[1]
    qs = _per_head(q_ref[0], HEAD_DIM)
    causal = _key_le_query(tq)
    _attn_init(m_ref, l_ref, acc_ref)

    km_parts = _split3(kmean_ref[0])
    blk = lax.broadcasted_iota(I32, (nblk, tq), 0).astype(F32)
    past = blk < i.astype(F32)
    for hh in range(HEADS_PER_STEP):
        gate = sum(_dot_nt(_block(part, hh // HEADS_PER_BLOCK), qs[hh]) for part in km_parts)
        sel = jnp.zeros((nblk, tq), jnp.bool_)
        for _ in range(MOBA_TOPK):
            remaining = jnp.logical_and(past, jnp.logical_not(sel))
            g = jnp.where(remaining, gate, NEG_INF)
            first = jnp.min(jnp.where(g == jnp.max(g, axis=0, keepdims=True), blk, float(nblk)),
                            axis=0, keepdims=True)
            sel = jnp.logical_or(sel, jnp.logical_and(blk == first, remaining))
        bias_ref[hh] = jnp.where(sel, 0.0, NEG_INF)

    def scores_of(kt, diag):
        ks = pl.multiple_of(kt * tq, tq)
        k = k_ref[0, pl.ds(ks, tq), :]
        scores = [_dot_nt(_block(k, hh // HEADS_PER_BLOCK), qs[hh]) for hh in range(HEADS_PER_STEP)]
        if diag:
            scores = [jnp.where(causal, s, NEG_INF) for s in scores]
        else:
            scores = [s + bias_ref[hh, pl.ds(kt, 1), :] for hh, s in enumerate(scores)]
        return tuple(scores)

    _attend_tiles(i, scores_of, vt_ref, m_ref, l_ref, acc_ref)
    _attn_finish(o_ref, l_ref, acc_ref)


def _attn_scratch(tq):
    return [pltpu.VMEM((HEADS_PER_STEP, 1, tq), F32), pltpu.VMEM((HEADS_PER_STEP, 1, tq), F32),
            pltpu.VMEM((STEP_LANES, tq), F32)]


def _attn_specs(s, tq):
    q_spec = pl.BlockSpec((1, tq, STEP_LANES), lambda bi, hb, i: (bi, i, hb))
    k_spec = pl.BlockSpec((1, s, STEP_LANES), lambda bi, hb, i: (bi, 0, hb))
    vt_spec = pl.BlockSpec((1, s // tq, STEP_LANES, tq), lambda bi, hb, i: (bi, 0, hb, 0))
    return q_spec, k_spec, vt_spec


def _fox(q, qa, k, ka, vt):
    b, s, width = q.shape
    tq = TQ
    q_spec, k_spec, vt_spec = _attn_specs(s, tq)
    return pl.pallas_call(
        _fox_kernel,
        grid=(b, width // STEP_LANES, s // tq),
        in_specs=[q_spec, pl.BlockSpec((1, tq, LANES), lambda bi, hb, i: (bi, i, 0)),
                  k_spec, pl.BlockSpec((1, s, LANES), lambda bi, hb, i: (bi, 0, 0)), vt_spec],
        out_specs=q_spec,
        out_shape=jax.ShapeDtypeStruct((b, s, width), F32),
        scratch_shapes=_attn_scratch(tq),
        compiler_params=_params(("arbitrary", "arbitrary", "arbitrary")),
        name="fox",
    )(q, qa, k, ka, vt)


def _moba(q, k, vt, kmean):
    b, s, width = q.shape
    tq = TQ
    nblk = kmean.shape[1]
    q_spec, k_spec, vt_spec = _attn_specs(s, tq)
    return pl.pallas_call(
        _moba_kernel,
        grid=(b, width // STEP_LANES, s // tq),
        in_specs=[q_spec, k_spec, vt_spec,
                  pl.BlockSpec((1, nblk, STEP_LANES), lambda bi, hb, i: (bi, 0, hb))],
        out_specs=q_spec,
        out_shape=jax.ShapeDtypeStruct((b, s, width), F32),
        scratch_shapes=_attn_scratch(tq) + [pltpu.VMEM((HEADS_PER_STEP, nblk, tq), F32)],
        compiler_params=_params(("arbitrary", "arbitrary", "arbitrary")),
        name="moba",
    )(q, k, vt, kmean)


def _postattn_kernel(x_ref, fox_ref, moba_ref, gf_ref, gm_ref, wo_ref, gn_ref, wr_ref, br_ref,
                     x2_ref, h2_ref, wts_ref, route_ref, cnt_ref, carry_ref):
    t = pl.program_id(0)
    tm = x_ref.shape[0]
    fw = fox_ref.shape[1]
    mixed_f = _rms(fox_ref[...], gf_ref[...]).astype(BF16)
    mixed_m = _rms(moba_ref[...], gm_ref[...]).astype(BF16)
    x2 = x_ref[...] + _dot(mixed_f, wo_ref[:fw, :]) + _dot(mixed_m, wo_ref[fw:, :])
    x2_ref[...] = x2
    h2 = _rms(x2, gn_ref[...])
    _store_rows(h2_ref, h2)

    h_hi = h2.astype(BF16)
    h_lo = (h2 - h_hi.astype(F32)).astype(BF16)
    wr = wr_ref[...]
    w_hi = wr.astype(BF16)
    w_lo = (wr - w_hi.astype(F32)).astype(BF16)
    logits = _dot(h_hi, w_hi) + _dot(h_hi, w_lo) + _dot(h_lo, w_hi) + br_ref[...]
    lane = lax.broadcasted_iota(I32, (tm, LANES), 1).astype(F32)

    def first_max(vals):
        mx = jnp.max(vals, axis=1, keepdims=True)
        return mx, jnp.min(jnp.where(vals == mx, lane, float(LANES)), axis=1, keepdims=True)

    gl = jnp.where(lane < N_GROUPS, logits, NEG_INF)
    gmax, g_idx = first_max(gl)
    g_top = 1.0 / jnp.sum(jnp.exp(gl - gmax), axis=1, keepdims=True)
    e_lo = N_GROUPS + EXPERTS_PER_GROUP * g_idx
    el = jnp.where(jnp.logical_and(lane >= e_lo, lane < e_lo + EXPERTS_PER_GROUP), logits, NEG_INF)
    emax, i1 = first_max(el)
    esum = jnp.sum(jnp.exp(el - emax), axis=1, keepdims=True)
    e2max, i2 = first_max(jnp.where(lane == i1, NEG_INF, el))
    p1 = 1.0 / esum
    p2 = jnp.exp(e2max - emax) / esum
    w1 = p1 / (p1 + p2) * g_top
    w2 = p2 / (p1 + p2) * g_top
    e1 = i1 - N_GROUPS
    e2 = i2 - N_GROUPS
    wts_ref[...] = jnp.where(lane == 0.0, w1, w2)[:, :2]

    @pl.when(t == 0)
    def _():
        carry_ref[...] = jnp.zeros_like(carry_ref)

    row = lax.broadcasted_iota(I32, (tm, tm), 0)
    col = lax.broadcasted_iota(I32, (tm, tm), 1)
    strict = jnp.where(row > col, 1.0, 0.0).astype(BF16)
    hit1 = lane == e1
    hit2 = lane == e2
    oh1 = jnp.where(hit1, 1.0, 0.0)
    oh2 = jnp.where(hit2, 1.0, 0.0)
    tot1 = jnp.sum(oh1, axis=0, keepdims=True)
    tot2 = jnp.sum(oh2, axis=0, keepdims=True)
    base = carry_ref[...]
    before1 = _dot(strict, oh1.astype(BF16)) + base
    before2 = _dot(strict, oh2.astype(BF16)) + (base + tot1)
    r1 = jnp.sum(jnp.where(hit1, before1, 0.0), axis=1, keepdims=True)
    r2 = jnp.sum(jnp.where(hit2, before2, 0.0), axis=1, keepdims=True)
    record = jnp.where(lane == 0.0, e1, jnp.where(lane == 1.0, e2, jnp.where(lane == 2.0, r1, r2)))
    route_ref[...] = record.T[:SUBLANES, :].astype(I32)
    total = base + tot1 + tot2
    carry_ref[...] = total
    cnt_ref[...] = total.astype(I32)


def _postattn(x, fox, moba, gf, gm, wo, gn, wr, br):
    t, d = x.shape
    tm = TM_PROJ
    fw = fox.shape[1]
    const = lambda i: (0, 0)
    rows = lambda i: (i, 0)
    return pl.pallas_call(
        _postattn_kernel,
        grid=(t // tm,),
        in_specs=[pl.BlockSpec((tm, d), rows), pl.BlockSpec((tm, fw), rows),
                  pl.BlockSpec((tm, moba.shape[1]), rows),
                  pl.BlockSpec((1, fw), const), pl.BlockSpec((1, moba.shape[1]), const),
                  pl.BlockSpec(wo.shape, const), pl.BlockSpec((1, d), const),
                  pl.BlockSpec(wr.shape, const), pl.BlockSpec((1, LANES), const)],
        out_specs=[pl.BlockSpec((tm, d), rows), pl.BlockSpec((tm * SUBLANES, LANES), rows),
                   pl.BlockSpec((tm, 2), rows), pl.BlockSpec((SUBLANES, tm), lambda i: (0, i)),
                   pl.BlockSpec((1, LANES), const)],
        out_shape=[jax.ShapeDtypeStruct((t, d), F32), jax.ShapeDtypeStruct((t * SUBLANES, LANES), F32),
                   jax.ShapeDtypeStruct((t, 2), F32), jax.ShapeDtypeStruct((SUBLANES, t), I32),
                   jax.ShapeDtypeStruct((1, LANES), I32)],
        scratch_shapes=[pltpu.VMEM((1, LANES), F32)],
        compiler_params=_params(("arbitrary",)),
        name="postattn",
    )(x, fox, moba, gf, gm, wo, gn, wr, br)


def _store_rows(ref, val):
    for g in range(SUBLANES):
        ref[pl.ds(g, val.shape[0], stride=SUBLANES), :] = val[:, g * LANES:(g + 1) * LANES]


def _load_rows(ref):
    tokens = ref.shape[0] // SUBLANES
    return jnp.concatenate([ref[pl.ds(g, tokens, stride=SUBLANES), :] for g in range(SUBLANES)], axis=1)


def _row_copy(src, src_row, dst, dst_row, sem):
    rows = lambda r: pl.ds(pl.multiple_of(r * SUBLANES, SUBLANES), SUBLANES)
    return pltpu.make_async_copy(src.at[rows(src_row)], dst.at[rows(dst_row)], sem)


def _dispatch_kernel(pad_start_ref, pad_len_ref, dest_ref, h_ref, xs_ref, zero_ref, sem):
    tm = dest_ref.shape[2] // 2

    @pl.when(pl.program_id(0) == 0)
    def _():
        zero_ref[...] = jnp.zeros_like(zero_ref)

        def fill(e, c):
            def put(j, c2):
                _row_copy(zero_ref, 0, xs_ref, pad_start_ref[e] + j, sem).start()
                return c2

            def got(j, c2):
                _row_copy(zero_ref, 0, xs_ref, 0, sem).wait()
                return c2

            lax.fori_loop(0, pad_len_ref[e], put, 0)
            lax.fori_loop(0, pad_len_ref[e], got, 0)
            return c

        lax.fori_loop(0, pad_start_ref.shape[0], fill, 0)

    def issue(r, c):
        for k in range(2):
            _row_copy(h_ref, r, xs_ref, dest_ref[0, 0, k * tm + r], sem).start()
        return c

    lax.fori_loop(0, tm, issue, 0, unroll=ISSUE_UNROLL)

    def drain(r, c):
        _row_copy(h_ref, 0, xs_ref, 0, sem).wait()
        return c

    lax.fori_loop(0, 2 * tm, drain, 0, unroll=ISSUE_UNROLL)


def _dispatch(pad_start, pad_len, dest3, h2, n_rows):
    tm = dest3.shape[2] // 2
    grid_spec = pltpu.PrefetchScalarGridSpec(
        num_scalar_prefetch=2,
        grid=(dest3.shape[0],),
        in_specs=[pl.BlockSpec((1, 1, 2 * tm), lambda i, ps, pn: (i, 0, 0), memory_space=pltpu.SMEM),
                  pl.BlockSpec((tm * SUBLANES, LANES), lambda i, ps, pn: (i, 0))],
        out_specs=pl.BlockSpec(memory_space=pl.ANY),
        scratch_shapes=[pltpu.VMEM((SUBLANES, LANES), F32), pltpu.SemaphoreType.DMA(())],
    )
    return pl.pallas_call(
        _dispatch_kernel,
        grid_spec=grid_spec,
        out_shape=jax.ShapeDtypeStruct((n_rows * SUBLANES, LANES), F32),
        compiler_params=_params(("arbitrary",)),
        name="dispatch",
    )(pad_start, pad_len, dest3, h2)


def _experts_kernel(te_ref, ts_ref, nv_ref, xs_ref, wg_ref, wu_ref, wd_ref, ys_ref,
                    wgb_ref, wub_ref, wdb_ref):
    del ts_ref
    t = pl.program_id(0)

    @pl.when(jnp.logical_or(t == 0, te_ref[t] != te_ref[jnp.maximum(t - 1, 0)]))
    def _():
        wgb_ref[...] = wg_ref[0].astype(BF16)
        wub_ref[...] = wu_ref[0].astype(BF16)
        wdb_ref[...] = wd_ref[0].astype(BF16)

    @pl.when(t < nv_ref[0])
    def _():
        xb = _load_rows(xs_ref).astype(BF16)
        a = _dot(xb, wgb_ref[...])
        u = _dot(xb, wub_ref[...])
        act = (a * jax.nn.sigmoid(a) * u).astype(BF16)
        _store_rows(ys_ref, _dot(act, wdb_ref[...]))

    @pl.when(t >= nv_ref[0])
    def _():
        ys_ref[...] = jnp.zeros_like(ys_ref)


def _experts(tile_expert, tile_src, n_valid, xs, wg, wu, wd):
    tm = TM_EXPERT
    n_tiles = xs.shape[0] // (tm * SUBLANES)
    _, d, f = wg.shape
    row_block = (tm * SUBLANES, LANES)
    grid_spec = pltpu.PrefetchScalarGridSpec(
        num_scalar_prefetch=3,
        grid=(n_tiles,),
        in_specs=[pl.BlockSpec(row_block, lambda t, te, ts, nv: (ts[t], 0)),
                  pl.BlockSpec((1, d, f), lambda t, te, ts, nv: (te[t], 0, 0)),
                  pl.BlockSpec((1, d, f), lambda t, te, ts, nv: (te[t], 0, 0)),
                  pl.BlockSpec((1, f, d), lambda t, te, ts, nv: (te[t], 0, 0))],
        out_specs=pl.BlockSpec(row_block, lambda t, te, ts, nv: (t, 0)),
        scratch_shapes=[pltpu.VMEM((d, f), BF16), pltpu.VMEM((d, f), BF16), pltpu.VMEM((f, d), BF16)],
    )
    return pl.pallas_call(
        _experts_kernel,
        grid_spec=grid_spec,
        out_shape=jax.ShapeDtypeStruct(xs.shape, F32),
        compiler_params=_params(("arbitrary",)),
        name="experts",
    )(tile_expert, tile_src, n_valid, xs, wg, wu, wd)


def _combine_kernel(dest_ref, x2_ref, wts_ref, g_ref, ys_ref, o_ref, buf_ref, sem):
    tm = dest_ref.shape[2] // 2

    def issue(r, c):
        for k in range(2):
            _row_copy(ys_ref, dest_ref[0, 0, k * tm + r], buf_ref.at[k], r, sem).start()
        return c

    lax.fori_loop(0, tm, issue, 0, unroll=ISSUE_UNROLL)

    def drain(r, c):
        _row_copy(ys_ref, 0, buf_ref.at[0], 0, sem).wait()
        return c

    lax.fori_loop(0, 2 * tm, drain, 0, unroll=ISSUE_UNROLL)
    w = wts_ref[...]
    y = x2_ref[...] + w[:, 0:1] * _load_rows(buf_ref.at[0]) + w[:, 1:2] * _load_rows(buf_ref.at[1])
    o_ref[...] = _rms(y, g_ref[...])


def _combine(dest3, x2, wts, g, ys):
    t, d = x2.shape
    tm = dest3.shape[2] // 2
    rows = lambda i: (i, 0)
    return pl.pallas_call(
        _combine_kernel,
        grid=(t // tm,),
        in_specs=[pl.BlockSpec((1, 1, 2 * tm), lambda i: (i, 0, 0), memory_space=pltpu.SMEM),
                  pl.BlockSpec((tm, d), rows), pl.BlockSpec((tm, 2), rows),
                  pl.BlockSpec((1, d), lambda i: (0, 0)),
                  pl.BlockSpec(memory_space=pl.ANY)],
        out_specs=pl.BlockSpec((tm, d), rows),
        out_shape=jax.ShapeDtypeStruct((t, d), F32),
        scratch_shapes=[pltpu.VMEM((2, tm * SUBLANES, LANES), F32), pltpu.SemaphoreType.DMA(())],
        compiler_params=_params(("arbitrary",)),
        name="combine",
    )(dest3, x2, wts, g, ys)


def _rotary_tables(seq):
    half = ROPE_DIM // 2
    inv_freq = ROPE_THETA ** (-jnp.arange(half, dtype=F32) / half)
    ang = jnp.arange(seq, dtype=F32)[:, None] * inv_freq[None, :]
    cos, sin = jnp.cos(ang), jnp.sin(ang)
    ones = jnp.ones((seq, HEAD_DIM - ROPE_DIM), F32)
    zeros = jnp.zeros((seq, HEAD_DIM - ROPE_DIM), F32)
    zh = jnp.zeros((seq, half), F32)
    cosm = jnp.concatenate([cos, cos, ones], axis=1)
    sina = jnp.concatenate([-sin, zh, zeros], axis=1)
    sinb = jnp.concatenate([zh, sin, zeros], axis=1)
    tile = lambda a: jnp.tile(a, (1, HEADS_PER_BLOCK))
    return tile(cosm), tile(sina), tile(sinb)


def _pad_lanes(a):
    return jnp.pad(a, ((0, 0), (0, LANES - a.shape[1])))


def kernel(x, norm_mix_g, w_in, b_forget, fox_out_g, moba_out_g, w_out, norm_ffn_g, w_router_group,
           b_router_group, w_router_expert, b_router_expert, w_gate, w_up, w_down, norm_final_g):
    b, s, d = x.shape
    t = b * s
    assert w_in.shape[0] == 1, "the closing RMSNorm is fused into the only layer's combine step"
    cosm, sina, sinb = _rotary_tables(s)
    n_tiles = (2 * t) // TM_EXPERT + N_EXPERTS
    fw3 = 3 * FOX_WIDTH
    m0 = fw3 + N_FOX_HEADS
    wl = w_in[0]
    w_main = jnp.concatenate([wl[:, :2 * FOX_WIDTH], wl[:, m0:m0 + 2 * MOBA_WIDTH]],
                             axis=1).astype(BF16)
    w_vt = jnp.stack([wl[:, 2 * FOX_WIDTH:fw3].T, wl[:, m0 + 2 * MOBA_WIDTH:].T]).astype(BF16)
    aux_head = jnp.arange(AUX_PER_HEAD * N_FOX_HEADS) // AUX_PER_HEAD
    w_logit = _pad_lanes(wl[:, fw3:m0][:, aux_head]).astype(BF16)
    b_logit = _pad_lanes(b_forget[0][None, aux_head])
    fq, fqa, fk, fka, fvt, mq, mk, mvt, kmean = _inproj(
        x, norm_mix_g[0][None, :], w_main, w_vt, w_logit, b_logit, cosm, sina, sinb)
    fox = _fox(fq, fqa, fk, fka, fvt)
    moba = _moba(mq, mk, mvt, kmean)

    w_router = _pad_lanes(jnp.concatenate(
        [w_router_group[0], w_router_expert[0].reshape(d, N_EXPERTS)], axis=1))
    b_router = _pad_lanes(jnp.concatenate(
        [b_router_group[0], b_router_expert[0].reshape(N_EXPERTS)])[None, :])
    x2, h2, wts, route, counts = _postattn(
        x.reshape(t, d), fox.reshape(t, FOX_WIDTH), moba.reshape(t, MOBA_WIDTH),
        fox_out_g[0][None, :], moba_out_g[0][None, :], w_out[0].astype(BF16),
        norm_ffn_g[0][None, :], w_router, b_router)

    counts = counts[0, :N_EXPERTS]
    padded = (counts + TM_EXPERT - 1) // TM_EXPERT * TM_EXPERT
    ends = jnp.cumsum(padded)
    starts = ends - padded
    dest = starts[route[0:2]] + route[2:4]
    dest3 = dest.reshape(2, t // TM_ROWS, TM_ROWS).transpose(1, 0, 2).reshape(t // TM_ROWS, 1, 2 * TM_ROWS)
    n_rows = n_tiles * TM_EXPERT
    pad_start = jnp.concatenate([starts + counts, ends[-1:]]).astype(I32)
    pad_len = jnp.concatenate([padded - counts, n_rows - ends[-1:]]).astype(I32)
    n_valid = ends[-1] // TM_EXPERT
    tile_src = jnp.minimum(jnp.arange(n_tiles, dtype=I32), n_valid - 1)
    tile_expert = jnp.sum(ends[None, :] <= (tile_src * TM_EXPERT)[:, None], axis=1).astype(I32)
    xs = _dispatch(pad_start, pad_len, dest3, h2, n_rows)
    ys = _experts(tile_expert, tile_src, n_valid.reshape(1).astype(I32), xs,
                  w_gate[0], w_up[0], w_down[0])
    return _combine(dest3, x2, wts, norm_final_g[None, :], ys).reshape(b, s, d)
```

```python
import math

import jax
import jax.numpy as jnp
from jax import lax
from jax.experimental import pallas as pl
from jax.experimental.pallas import tpu as pltpu

F32 = jnp.float32
BF16 = jnp.bfloat16
I32 = jnp.int32

HEAD_DIM = 64
N_FOX_HEADS = 8
N_MOBA_HEADS = 8
FOX_WIDTH = N_FOX_HEADS * HEAD_DIM
MOBA_WIDTH = N_MOBA_HEADS * HEAD_DIM
MOBA_BLOCK = 256
MOBA_TOPK = 3
ROPE_THETA = 500000.0
ROPE_DIM = HEAD_DIM // 4
N_GROUPS = 4
EXPERTS_PER_GROUP = 8
N_EXPERTS = N_GROUPS * EXPERTS_PER_GROUP
EPS = 1e-6

LANES = 128
SUBLANES = 8
BF16_SUBLANES = 16
LOG2_E = math.log2(math.e)
HEADS_PER_BLOCK = LANES // HEAD_DIM
BLOCKS_PER_STEP = 4
HEADS_PER_STEP = HEADS_PER_BLOCK * BLOCKS_PER_STEP
STEP_LANES = LANES * BLOCKS_PER_STEP
TILES_PER_TRIP = 2
VMEM_LIMIT = 56 * 1024 * 1024
AUX_PER_HEAD = 6

TM_PROJ = 512
TQ = 256
TM_EXPERT = 256
TM_ROWS = 256
ISSUE_UNROLL = 8

NEG_INF = float("-inf")


def _params(sem):
    return pltpu.CompilerParams(dimension_semantics=sem, vmem_limit_bytes=VMEM_LIMIT)


def _rms(x, g):
    return x * lax.rsqrt(jnp.mean(x * x, axis=-1, keepdims=True) + EPS) * g


def _split3(x):
    hi = x.astype(BF16)
    r = x - hi.astype(F32)
    mid = r.astype(BF16)
    lo = (r - mid.astype(F32)).astype(BF16)
    return hi, mid, lo


def _dot(a, b):
    return jnp.dot(a, b, preferred_element_type=F32)


def _dot_nt(a, b):
    return lax.dot_general(a, b, (((1,), (1,)), ((), ())), preferred_element_type=F32)


def _inproj_kernel(x_ref, g_ref, w_ref, wvt_ref, wl_ref, bf_ref, kind_ref, cosm_ref, sina_ref, sinb_ref,
                   fq_ref, fqa_ref, fk_ref, fka_ref, fvt_ref, mq_ref, mk_ref, mvt_ref, kmean_ref,
                   carry_ref):
    j = pl.program_id(1)
    tm = x_ref.shape[1]
    tk = fvt_ref.shape[3]
    h = _rms(x_ref[0], g_ref[...]).astype(BF16)
    scale = HEAD_DIM ** -0.5 * LOG2_E

    def proj(seg):
        return _dot(h, w_ref[:, seg * FOX_WIDTH:(seg + 1) * FOX_WIDTH])

    fq_ref[0] = (proj(0) * scale).astype(BF16)
    fk_ref[0] = proj(1).astype(BF16)

    for vt_ref, seg in ((fvt_ref, 0), (mvt_ref, 1)):
        vt = _dot_nt(wvt_ref[seg], h).astype(BF16)
        for r in range(tm // tk):
            vt_ref[0, r] = vt[:, r * tk:(r + 1) * tk]

    cosm, sina, sinb = cosm_ref[...], sina_ref[...], sinb_ref[...]

    def rotary(t):
        outs = []
        for g in range(MOBA_WIDTH // LANES):
            tg = t[:, g * LANES:(g + 1) * LANES]
            outs.append(tg * cosm + pltpu.roll(tg, LANES - ROPE_DIM // 2, 1) * sina
                        + pltpu.roll(tg, ROPE_DIM // 2, 1) * sinb)
        return jnp.concatenate(outs, axis=1)

    mq_ref[0] = (rotary(proj(2)) * scale).astype(BF16)
    mk = rotary(proj(3))
    mk_ref[0] = mk.astype(BF16)
    nblk_tile = tm // MOBA_BLOCK
    means = [jnp.mean(mk[r * MOBA_BLOCK:(r + 1) * MOBA_BLOCK], axis=0, keepdims=True)
             for r in range(nblk_tile)]
    means += [jnp.zeros_like(means[0])] * (kmean_ref.shape[2] - nblk_tile)
    kmean_ref[0, 0] = jnp.concatenate(means, axis=0)

    z = _dot(h, wl_ref[...]) + bf_ref[...]
    log_f = jnp.minimum(z, 0.0) - jnp.log1p(jnp.exp(-jnp.abs(z)))

    @pl.when(j == 0)
    def _():
        carry_ref[...] = jnp.zeros_like(carry_ref)

    half = tm // 2
    row = lax.broadcasted_iota(I32, (half, half), 0)
    col = lax.broadcasted_iota(I32, (half, half), 1)
    tri = jnp.where(row >= col, 1.0, 0.0).astype(BF16)
    pieces = jnp.concatenate(_split3(log_f), axis=1)
    carry = carry_ref[...]
    cs = []
    for r in range(2):
        local = _dot(tri, pieces[r * half:(r + 1) * half])
        cs.append(local[:, :LANES] + local[:, LANES:2 * LANES] + local[:, 2 * LANES:] + carry)
        carry = cs[-1][half - 1:half, :]
    carry_ref[...] = carry
    c = jnp.concatenate(cs, axis=0) * LOG2_E

    hi = c.astype(BF16).astype(F32)
    mid = (c - hi).astype(BF16).astype(F32)
    lo = c - hi - mid
    kind = kind_ref[...]
    one = jnp.where(kind < AUX_PER_HEAD, 1.0, 0.0)
    pick = lambda base: jnp.where(kind == base, hi, jnp.where(kind == base + 1, mid,
                                  jnp.where(kind == base + 2, lo, 0.0)))
    fqa_ref[0] = (pick(3) + jnp.where(kind < 3, one, 0.0)).astype(BF16)
    fka_ref[0] = (jnp.where(kind >= 3, one, 0.0) - pick(0)).astype(BF16)


def _inproj(x, g, w_main, w_vt, w_logit, b_logit, cosm, sina, sinb):
    b, s, d = x.shape
    tm, tk = TM_PROJ, TQ
    lane = jnp.arange(LANES, dtype=I32)
    kind = jnp.where(lane < AUX_PER_HEAD * N_FOX_HEADS, lane % AUX_PER_HEAD, AUX_PER_HEAD)[None, :]
    act = jax.ShapeDtypeStruct((b, s, FOX_WIDTH), BF16)
    aux = jax.ShapeDtypeStruct((b, s, LANES), BF16)
    vt = jax.ShapeDtypeStruct((b, s // tk, FOX_WIDTH, tk), BF16)
    out_shape = [act, aux, act, aux, vt, act, act, vt,
                 jax.ShapeDtypeStruct((b, s // tm, SUBLANES, MOBA_WIDTH), F32)]
    act_spec = pl.BlockSpec((1, tm, FOX_WIDTH), lambda bi, j: (bi, j, 0))
    aux_spec = pl.BlockSpec((1, tm, LANES), lambda bi, j: (bi, j, 0))
    vt_spec = pl.BlockSpec((1, tm // tk, FOX_WIDTH, tk), lambda bi, j: (bi, j, 0, 0))
    tab_spec = pl.BlockSpec((tm, LANES), lambda bi, j: (j, 0))
    const2 = lambda bi, j: (0, 0)
    const3 = lambda bi, j: (0, 0, 0)
    *acts, kmean = pl.pallas_call(
        _inproj_kernel,
        grid=(b, s // tm),
        in_specs=[pl.BlockSpec((1, tm, d), lambda bi, j: (bi, j, 0)),
                  pl.BlockSpec((1, d), const2),
                  pl.BlockSpec(w_main.shape, const2),
                  pl.BlockSpec(w_vt.shape, const3),
                  pl.BlockSpec(w_logit.shape, const2),
                  pl.BlockSpec((1, LANES), const2), pl.BlockSpec((1, LANES), const2),
                  tab_spec, tab_spec, tab_spec],
        out_specs=[act_spec, aux_spec, act_spec, aux_spec, vt_spec, act_spec, act_spec, vt_spec,
                   pl.BlockSpec((1, 1, SUBLANES, MOBA_WIDTH), lambda bi, j: (bi, j, 0, 0))],
        out_shape=out_shape,
        scratch_shapes=[pltpu.VMEM((1, LANES), F32)],
        compiler_params=_params(("arbitrary", "arbitrary")),
        name="inproj",
    )(x, g, w_main, w_vt, w_logit, b_logit, kind, cosm, sina, sinb)
    kmean = kmean[:, :, :tm // MOBA_BLOCK].reshape(b, s // MOBA_BLOCK, MOBA_WIDTH)
    return (*acts, kmean)


def _softmax(heads, scores, m_ref):
    stats = []
    for hh, s in zip(heads, scores):
        m_prev = m_ref[hh]
        m_new = jnp.maximum(m_prev, jnp.max(s, axis=0, keepdims=True))
        m_ref[hh] = m_new
        stats.append((jnp.exp2(m_prev - m_new), m_new))
    return [(alpha, jnp.exp2(s - m_new).astype(BF16)) for (alpha, m_new), s in zip(stats, scores)]


def _values(weighted, vt, l_ref, acc_ref):
    ones = jnp.ones((BF16_SUBLANES, vt.shape[1]), BF16)
    for hh, (alpha, p) in enumerate(weighted):
        rows = slice(hh * HEAD_DIM, (hh + 1) * HEAD_DIM)
        pv = _dot(jnp.concatenate([vt[rows, :], ones], axis=0), p)
        acc_ref[rows, :] = alpha * acc_ref[rows, :] + pv[:HEAD_DIM]
        l_ref[hh] = alpha * l_ref[hh] + pv[HEAD_DIM:HEAD_DIM + 1]


def _attend_tiles(i, scores_of, vt_ref, m_ref, l_ref, acc_ref):
    heads = tuple(range(HEADS_PER_STEP))

    def tile(kt, diag):
        _values(_softmax(heads, scores_of(kt, diag, heads), m_ref), vt_ref[0, kt], l_ref, acc_ref)

    def trip(j, c):
        for r in range(TILES_PER_TRIP):
            tile(TILES_PER_TRIP * j + r, False)
        return c

    tile(i, True)
    full = i // TILES_PER_TRIP
    lax.fori_loop(0, full, trip, 0)
    for r in range(TILES_PER_TRIP - 1):
        @pl.when(i % TILES_PER_TRIP > r)
        def _():
            tile(TILES_PER_TRIP * full + r, False)


def _attn_init(m_ref, l_ref, acc_ref):
    m_ref[...] = jnp.full(m_ref.shape, NEG_INF, F32)
    l_ref[...] = jnp.zeros_like(l_ref)
    acc_ref[...] = jnp.zeros_like(acc_ref)


def _attn_finish(o_ref, l_ref, acc_ref):
    out_t = jnp.concatenate(
        [acc_ref[hh * HEAD_DIM:(hh + 1) * HEAD_DIM, :] / l_ref[hh] for hh in range(HEADS_PER_STEP)],
        axis=0)
    o_ref[0] = out_t.T


def _block(a, g):
    return a[:, g * LANES:(g + 1) * LANES]


def _per_head(a, width):
    first = lax.broadcasted_iota(I32, (a.shape[0], LANES), 1) < width
    zero = jnp.zeros((a.shape[0], LANES), a.dtype)
    out = []
    for g in range(BLOCKS_PER_STEP):
        blk = _block(a, g)
        out += [jnp.where(first, blk, zero), jnp.where(first, zero, blk)]
    return out


def _key_le_query(tq):
    return lax.broadcasted_iota(I32, (tq, tq), 0) <= lax.broadcasted_iota(I32, (tq, tq), 1)


def _fox_kernel(q_ref, qa_ref, k_ref, ka_ref, vt_ref, o_ref, m_ref, l_ref, acc_ref):
    i = pl.program_id(2)
    tq = q_ref.shape[1]
    qa = qa_ref[0]
    lane = lax.broadcasted_iota(I32, qa.shape, 1) - pl.program_id(1) * (HEADS_PER_STEP * AUX_PER_HEAD)
    own_aux = lambda hh: jnp.logical_and(lane >= hh * AUX_PER_HEAD, lane < (hh + 1) * AUX_PER_HEAD)
    qq = [jnp.concatenate([qm, jnp.where(own_aux(hh), qa, jnp.zeros_like(qa))], axis=1)
          for hh, qm in enumerate(_per_head(q_ref[0], HEAD_DIM))]
    causal = _key_le_query(tq)
    _attn_init(m_ref, l_ref, acc_ref)

    def scores_of(kt, diag, heads):
        ks = pl.multiple_of(kt * tq, tq)
        ka = ka_ref[0, pl.ds(ks, tq), :]
        kk = {g: jnp.concatenate([k_ref[0, pl.ds(ks, tq), g * LANES:(g + 1) * LANES], ka], axis=1)
              for g in sorted({hh // HEADS_PER_BLOCK for hh in heads})}
        scores = [_dot_nt(kk[hh // HEADS_PER_BLOCK], qq[hh]) for hh in heads]
        if diag:
            scores = [jnp.where(causal, s, NEG_INF) for s in scores]
        return tuple(scores)

    _attend_tiles(i, scores_of, vt_ref, m_ref, l_ref, acc_ref)
    _attn_finish(o_ref, l_ref, acc_ref)


def _moba_kernel(q_ref, k_ref, vt_ref, kmean_ref, o_ref, m_ref, l_ref, acc_ref, bias_ref):
    i = pl.program_id(2)
    tq = q_ref.shape[1]
    nblk = kmean_ref.shape[1]
    qs = _per_head(q_ref[0], HEAD_DIM)
    causal = _key_le_query(tq)
    _attn_init(m_ref, l_ref, acc_ref)

    km_parts = _split3(kmean_ref[0])
    blk = lax.broadcasted_iota(I32, (nblk, tq), 0).astype(F32)
    past = blk < i.astype(F32)
    for hh in range(HEADS_PER_STEP):
        gate = sum(_dot_nt(_block(part, hh // HEADS_PER_BLOCK), qs[hh]) for part in km_parts)
        sel = jnp.zeros((nblk, tq), jnp.bool_)
        for _ in range(MOBA_TOPK):
            remaining = jnp.logical_and(past, jnp.logical_not(sel))
            g = jnp.where(remaining, gate, NEG_INF)
            first = jnp.min(jnp.where(g == jnp.max(g, axis=0, keepdims=True), blk, float(nblk)),
                            axis=0, keepdims=True)
            sel = jnp.logical_or(sel, jnp.logical_and(blk == first, remaining))
        bias_ref[hh] = jnp.where(sel, 0.0, NEG_INF)

    def scores_of(kt, diag, heads):
        ks = pl.multiple_of(kt * tq, tq)
        scores = [_dot_nt(k_ref[0, pl.ds(ks, tq), (hh // HEADS_PER_BLOCK) * LANES:
                                (hh // HEADS_PER_BLOCK + 1) * LANES], qs[hh]) for hh in heads]
        if diag:
            scores = [jnp.where(causal, s, NEG_INF) for s in scores]
        else:
            scores = [s + bias_ref[hh, pl.ds(kt, 1), :] for hh, s in zip(heads, scores)]
        return tuple(scores)

    _attend_tiles(i, scores_of, vt_ref, m_ref, l_ref, acc_ref)
    _attn_finish(o_ref, l_ref, acc_ref)


def _attn_scratch(tq):
    return [pltpu.VMEM((HEADS_PER_STEP, 1, tq), F32), pltpu.VMEM((HEADS_PER_STEP, 1, tq), F32),
            pltpu.VMEM((STEP_LANES, tq), F32)]


def _attn_specs(s, tq):
    q_spec = pl.BlockSpec((1, tq, STEP_LANES), lambda bi, hb, i: (bi, i, hb))
    k_spec = pl.BlockSpec((1, s, STEP_LANES), lambda bi, hb, i: (bi, 0, hb))
    vt_spec = pl.BlockSpec((1, s // tq, STEP_LANES, tq), lambda bi, hb, i: (bi, 0, hb, 0))
    return q_spec, k_spec, vt_spec


def _fox(q, qa, k, ka, vt):
    b, s, width = q.shape
    tq = TQ
    q_spec, k_spec, vt_spec = _attn_specs(s, tq)
    return pl.pallas_call(
        _fox_kernel,
        grid=(b, width // STEP_LANES, s // tq),
        in_specs=[q_spec, pl.BlockSpec((1, tq, LANES), lambda bi, hb, i: (bi, i, 0)),
                  k_spec, pl.BlockSpec((1, s, LANES), lambda bi, hb, i: (bi, 0, 0)), vt_spec],
        out_specs=q_spec,
        out_shape=jax.ShapeDtypeStruct((b, s, width), F32),
        scratch_shapes=_attn_scratch(tq),
        compiler_params=_params(("arbitrary", "arbitrary", "arbitrary")),
        name="fox",
    )(q, qa, k, ka, vt)


def _moba(q, k, vt, kmean):
    b, s, width = q.shape
    tq = TQ
    nblk = kmean.shape[1]
    q_spec, k_spec, vt_spec = _attn_specs(s, tq)
    return pl.pallas_call(
        _moba_kernel,
        grid=(b, width // STEP_LANES, s // tq),
        in_specs=[q_spec, k_spec, vt_spec,
                  pl.BlockSpec((1, nblk, STEP_LANES), lambda bi, hb, i: (bi, 0, hb))],
        out_specs=q_spec,
        out_shape=jax.ShapeDtypeStruct((b, s, width), F32),
        scratch_shapes=_attn_scratch(tq) + [pltpu.VMEM((HEADS_PER_STEP, nblk, tq), F32)],
        compiler_params=_params(("arbitrary", "arbitrary", "arbitrary")),
        name="moba",
    )(q, k, vt, kmean)


def _postattn_kernel(x_ref, fox_ref, moba_ref, gf_ref, gm_ref, wo_ref, gn_ref, wr_ref, br_ref,
                     x2_ref, h2_ref, wts_ref, route_ref, cnt_ref, carry_ref):
    t = pl.program_id(0)
    tm = x_ref.shape[0]
    fw = fox_ref.shape[1]
    mixed_f = _rms(fox_ref[...], gf_ref[...]).astype(BF16)
    mixed_m = _rms(moba_ref[...], gm_ref[...]).astype(BF16)
    x2 = x_ref[...] + _dot(mixed_f, wo_ref[:fw, :]) + _dot(mixed_m, wo_ref[fw:, :])
    x2_ref[...] = x2
    h2 = _rms(x2, gn_ref[...])
    _store_rows(h2_ref, h2)

    h_hi = h2.astype(BF16)
    h_lo = (h2 - h_hi.astype(F32)).astype(BF16)
    wr = wr_ref[...]
    w_hi = wr.astype(BF16)
    w_lo = (wr - w_hi.astype(F32)).astype(BF16)
    logits = _dot(h_hi, w_hi) + _dot(h_hi, w_lo) + _dot(h_lo, w_hi) + br_ref[...]
    lane = lax.broadcasted_iota(I32, (tm, LANES), 1).astype(F32)

    def first_max(vals):
        mx = jnp.max(vals, axis=1, keepdims=True)
        return mx, jnp.min(jnp.where(vals == mx, lane, float(LANES)), axis=1, keepdims=True)

    gl = jnp.where(lane < N_GROUPS, logits, NEG_INF)
    gmax, g_idx = first_max(gl)
    g_top = 1.0 / jnp.sum(jnp.exp(gl - gmax), axis=1, keepdims=True)
    e_lo = N_GROUPS + EXPERTS_PER_GROUP * g_idx
    el = jnp.where(jnp.logical_and(lane >= e_lo, lane < e_lo + EXPERTS_PER_GROUP), logits, NEG_INF)
    emax, i1 = first_max(el)
    esum = jnp.sum(jnp.exp(el - emax), axis=1, keepdims=True)
    e2max, i2 = first_max(jnp.where(lane == i1, NEG_INF, el))
    p1 = 1.0 / esum
    p2 = jnp.exp(e2max - emax) / esum
    w1 = p1 / (p1 + p2) * g_top
    w2 = p2 / (p1 + p2) * g_top
    e1 = i1 - N_GROUPS
    e2 = i2 - N_GROUPS
    wts_ref[...] = jnp.where(lane == 0.0, w1, w2)[:, :2]

    @pl.when(t == 0)
    def _():
        carry_ref[...] = jnp.zeros_like(carry_ref)

    row = lax.broadcasted_iota(I32, (tm, tm), 0)
    col = lax.broadcasted_iota(I32, (tm, tm), 1)
    strict = jnp.where(row > col, 1.0, 0.0).astype(BF16)
    hit1 = lane == e1
    hit2 = lane == e2
    oh1 = jnp.where(hit1, 1.0, 0.0)
    oh2 = jnp.where(hit2, 1.0, 0.0)
    tot1 = jnp.sum(oh1, axis=0, keepdims=True)
    tot2 = jnp.sum(oh2, axis=0, keepdims=True)
    base = carry_ref[...]
    before1 = _dot(strict, oh1.astype(BF16)) + base
    before2 = _dot(strict, oh2.astype(BF16)) + (base + tot1)
    r1 = jnp.sum(jnp.where(hit1, before1, 0.0), axis=1, keepdims=True)
    r2 = jnp.sum(jnp.where(hit2, before2, 0.0), axis=1, keepdims=True)
    record = jnp.where(lane == 0.0, e1, jnp.where(lane == 1.0, e2, jnp.where(lane == 2.0, r1, r2)))
    route_ref[...] = record.T[:SUBLANES, :].astype(I32)
    total = base + tot1 + tot2
    carry_ref[...] = total
    cnt_ref[...] = total.astype(I32)


def _postattn(x, fox, moba, gf, gm, wo, gn, wr, br):
    t, d = x.shape
    tm = TM_PROJ
    fw = fox.shape[1]
    const = lambda i: (0, 0)
    rows = lambda i: (i, 0)
    return pl.pallas_call(
        _postattn_kernel,
        grid=(t // tm,),
        in_specs=[pl.BlockSpec((tm, d), rows), pl.BlockSpec((tm, fw), rows),
                  pl.BlockSpec((tm, moba.shape[1]), rows),
                  pl.BlockSpec((1, fw), const), pl.BlockSpec((1, moba.shape[1]), const),
                  pl.BlockSpec(wo.shape, const), pl.BlockSpec((1, d), const),
                  pl.BlockSpec(wr.shape, const), pl.BlockSpec((1, LANES), const)],
        out_specs=[pl.BlockSpec((tm, d), rows), pl.BlockSpec((tm * SUBLANES, LANES), rows),
                   pl.BlockSpec((tm, 2), rows), pl.BlockSpec((SUBLANES, tm), lambda i: (0, i)),
                   pl.BlockSpec((1, LANES), const)],
        out_shape=[jax.ShapeDtypeStruct((t, d), F32), jax.ShapeDtypeStruct((t * SUBLANES, LANES), F32),
                   jax.ShapeDtypeStruct((t, 2), F32), jax.ShapeDtypeStruct((SUBLANES, t), I32),
                   jax.ShapeDtypeStruct((1, LANES), I32)],
        scratch_shapes=[pltpu.VMEM((1, LANES), F32)],
        compiler_params=_params(("arbitrary",)),
        name="postattn",
    )(x, fox, moba, gf, gm, wo, gn, wr, br)


def _store_rows(ref, val):
    for g in range(SUBLANES):
        ref[pl.ds(g, val.shape[0], stride=SUBLANES), :] = val[:, g * LANES:(g + 1) * LANES]


def _load_rows(ref):
    tokens = ref.shape[0] // SUBLANES
    return jnp.concatenate([ref[pl.ds(g, tokens, stride=SUBLANES), :] for g in range(SUBLANES)], axis=1)


def _row_copy(src, src_row, dst, dst_row, sem):
    rows = lambda r: pl.ds(pl.multiple_of(r * SUBLANES, SUBLANES), SUBLANES)
    return pltpu.make_async_copy(src.at[rows(src_row)], dst.at[rows(dst_row)], sem)


RING = 3


def _dispatch_kernel(pad_start_ref, pad_len_ref, dest_ref, h_ref, xs_ref,
                     ring_ref, zero_ref, fetch_sems, scatter_sems, pad_sem):
    i = pl.program_id(0)
    last = pl.num_programs(0) - 1
    tm = dest_ref.shape[2] // 2
    tile_rows = tm * SUBLANES

    def fetch(tile):
        start = pl.multiple_of(tile * tile_rows, tile_rows)
        slot = lax.rem(tile, RING)
        return pltpu.make_async_copy(h_ref.at[pl.ds(start, tile_rows)], ring_ref.at[slot],
                                     fetch_sems.at[slot])

    @pl.when(i == 0)
    def _():
        fetch(0).start()
        zero_ref[...] = jnp.zeros_like(zero_ref)

        def fill(e, c):
            def put(j, c2):
                _row_copy(zero_ref, 0, xs_ref, pad_start_ref[e] + j, pad_sem).start()
                return c2

            def got(j, c2):
                _row_copy(zero_ref, 0, xs_ref, 0, pad_sem).wait()
                return c2

            lax.fori_loop(0, pad_len_ref[e], put, 0)
            lax.fori_loop(0, pad_len_ref[e], got, 0)
            return c

        lax.fori_loop(0, pad_start_ref.shape[0], fill, 0)

    @pl.when(i < last)
    def _():
        fetch(i + 1).start()

    fetch(i).wait()
    src = ring_ref.at[lax.rem(i, RING)]

    def issue(r, c):
        for k in range(2):
            _row_copy(src, r, xs_ref, dest_ref[0, 0, k * tm + r], scatter_sems.at[i % 2]).start(priority=k)
        return c

    lax.fori_loop(0, tm, issue, 0, unroll=ISSUE_UNROLL)

    def drain(parity):
        def one(r, c):
            _row_copy(src, 0, xs_ref, 0, scatter_sems.at[parity]).wait()
            return c

        lax.fori_loop(0, 2 * tm, one, 0, unroll=ISSUE_UNROLL)

    @pl.when(i > 0)
    def _():
        drain((i - 1) % 2)

    @pl.when(i == last)
    def _():
        drain(i % 2)


def _dispatch(pad_start, pad_len, dest3, h2, n_rows):
    tm = dest3.shape[2] // 2
    grid_spec = pltpu.PrefetchScalarGridSpec(
        num_scalar_prefetch=2,
        grid=(dest3.shape[0],),
        in_specs=[pl.BlockSpec((1, 1, 2 * tm), lambda i, ps, pn: (i, 0, 0), memory_space=pltpu.SMEM),
                  pl.BlockSpec(memory_space=pl.ANY)],
        out_specs=pl.BlockSpec(memory_space=pl.ANY),
        scratch_shapes=[pltpu.VMEM((RING, tm * SUBLANES, LANES), F32), pltpu.VMEM((SUBLANES, LANES), F32),
                        pltpu.SemaphoreType.DMA((RING,)), pltpu.SemaphoreType.DMA((2,)),
                        pltpu.SemaphoreType.DMA(())],
    )
    return pl.pallas_call(
        _dispatch_kernel,
        grid_spec=grid_spec,
        out_shape=jax.ShapeDtypeStruct((n_rows * SUBLANES, LANES), F32),
        compiler_params=_params(("arbitrary",)),
        name="dispatch",
    )(pad_start, pad_len, dest3, h2)


def _experts_kernel(te_ref, ts_ref, nv_ref, xs_ref, wg_ref, wu_ref, wd_ref, ys_ref,
                    wgb_ref, wub_ref, wdb_ref):
    del ts_ref
    t = pl.program_id(0)

    @pl.when(jnp.logical_or(t == 0, te_ref[t] != te_ref[jnp.maximum(t - 1, 0)]))
    def _():
        wgb_ref[...] = wg_ref[0].astype(BF16)
        wub_ref[...] = wu_ref[0].astype(BF16)
        wdb_ref[...] = wd_ref[0].astype(BF16)

    @pl.when(t < nv_ref[0])
    def _():
        xb = _load_rows(xs_ref).astype(BF16)
        a = _dot(xb, wgb_ref[...])
        u = _dot(xb, wub_ref[...])
        act = (a * jax.nn.sigmoid(a) * u).astype(BF16)
        _store_rows(ys_ref, _dot(act, wdb_ref[...]))

    @pl.when(t >= nv_ref[0])
    def _():
        ys_ref[...] = jnp.zeros_like(ys_ref)


def _experts(tile_expert, tile_src, n_valid, xs, wg, wu, wd):
    tm = TM_EXPERT
    n_tiles = xs.shape[0] // (tm * SUBLANES)
    _, d, f = wg.shape
    row_block = (tm * SUBLANES, LANES)
    grid_spec = pltpu.PrefetchScalarGridSpec(
        num_scalar_prefetch=3,
        grid=(n_tiles,),
        in_specs=[pl.BlockSpec(row_block, lambda t, te, ts, nv: (ts[t], 0)),
                  pl.BlockSpec((1, d, f), lambda t, te, ts, nv: (te[t], 0, 0)),
                  pl.BlockSpec((1, d, f), lambda t, te, ts, nv: (te[t], 0, 0)),
                  pl.BlockSpec((1, f, d), lambda t, te, ts, nv: (te[t], 0, 0))],
        out_specs=pl.BlockSpec(row_block, lambda t, te, ts, nv: (t, 0)),
        scratch_shapes=[pltpu.VMEM((d, f), BF16), pltpu.VMEM((d, f), BF16), pltpu.VMEM((f, d), BF16)],
    )
    return pl.pallas_call(
        _experts_kernel,
        grid_spec=grid_spec,
        out_shape=jax.ShapeDtypeStruct(xs.shape, F32),
        compiler_params=_params(("arbitrary",)),
        name="experts",
    )(tile_expert, tile_src, n_valid, xs, wg, wu, wd)


def _combine_kernel(dest_ref, next_ref, x2_ref, wts_ref, g_ref, ys_ref, o_ref, buf_ref, sems):
    i = pl.program_id(0)
    tm = dest_ref.shape[2] // 2
    slot = i % 2

    def gather(d_ref, to):
        def issue(r, c):
            for k in range(2):
                _row_copy(ys_ref, d_ref[0, 0, k * tm + r], buf_ref.at[to, k], r,
                          sems.at[to]).start(priority=k)
            return c

        lax.fori_loop(0, tm, issue, 0, unroll=ISSUE_UNROLL)

    @pl.when(i == 0)
    def _():
        gather(dest_ref, 0)

    @pl.when(i + 1 < pl.num_programs(0))
    def _():
        gather(next_ref, 1 - slot)

    def drain(r, c):
        _row_copy(ys_ref, 0, buf_ref.at[slot, 0], 0, sems.at[slot]).wait()
        return c

    lax.fori_loop(0, 2 * tm, drain, 0, unroll=ISSUE_UNROLL)
    w = wts_ref[...]
    y = (x2_ref[...] + w[:, 0:1] * _load_rows(buf_ref.at[slot, 0])
         + w[:, 1:2] * _load_rows(buf_ref.at[slot, 1]))
    o_ref[...] = _rms(y, g_ref[...])


def _combine(dest3, x2, wts, g, ys):
    t, d = x2.shape
    tm = dest3.shape[2] // 2
    rows = lambda i: (i, 0)
    n = t // tm
    return pl.pallas_call(
        _combine_kernel,
        grid=(n,),
        in_specs=[pl.BlockSpec((1, 1, 2 * tm), lambda i: (i, 0, 0), memory_space=pltpu.SMEM),
                  pl.BlockSpec((1, 1, 2 * tm), lambda i: (jnp.minimum(i + 1, n - 1), 0, 0),
                               memory_space=pltpu.SMEM),
                  pl.BlockSpec((tm, d), rows), pl.BlockSpec((tm, 2), rows),
                  pl.BlockSpec((1, d), lambda i: (0, 0)),
                  pl.BlockSpec(memory_space=pl.ANY)],
        out_specs=pl.BlockSpec((tm, d), rows),
        out_shape=jax.ShapeDtypeStruct((t, d), F32),
        scratch_shapes=[pltpu.VMEM((2, 2, tm * SUBLANES, LANES), F32), pltpu.SemaphoreType.DMA((2,))],
        compiler_params=_params(("arbitrary",)),
        name="combine",
    )(dest3, dest3, x2, wts, g, ys)


def _rotary_tables(seq):
    half = ROPE_DIM // 2
    inv_freq = ROPE_THETA ** (-jnp.arange(half, dtype=F32) / half)
    ang = jnp.arange(seq, dtype=F32)[:, None] * inv_freq[None, :]
    cos, sin = jnp.cos(ang), jnp.sin(ang)
    ones = jnp.ones((seq, HEAD_DIM - ROPE_DIM), F32)
    zeros = jnp.zeros((seq, HEAD_DIM - ROPE_DIM), F32)
    zh = jnp.zeros((seq, half), F32)
    cosm = jnp.concatenate([cos, cos, ones], axis=1)
    sina = jnp.concatenate([-sin, zh, zeros], axis=1)
    sinb = jnp.concatenate([zh, sin, zeros], axis=1)
    tile = lambda a: jnp.tile(a, (1, HEADS_PER_BLOCK))
    return tile(cosm), tile(sina), tile(sinb)


def _pad_lanes(a):
    return jnp.pad(a, ((0, 0), (0, LANES - a.shape[1])))


def kernel(x, norm_mix_g, w_in, b_forget, fox_out_g, moba_out_g, w_out, norm_ffn_g, w_router_group,
           b_router_group, w_router_expert, b_router_expert, w_gate, w_up, w_down, norm_final_g):
    b, s, d = x.shape
    t = b * s
    assert w_in.shape[0] == 1, "the closing RMSNorm is fused into the only layer's combine step"
    cosm, sina, sinb = _rotary_tables(s)
    n_tiles = (2 * t) // TM_EXPERT + N_EXPERTS
    fw3 = 3 * FOX_WIDTH
    m0 = fw3 + N_FOX_HEADS
    wl = w_in[0]
    w_main = jnp.concatenate([wl[:, :2 * FOX_WIDTH], wl[:, m0:m0 + 2 * MOBA_WIDTH]],
                             axis=1).astype(BF16)
    w_vt = jnp.stack([wl[:, 2 * FOX_WIDTH:fw3].T, wl[:, m0 + 2 * MOBA_WIDTH:].T]).astype(BF16)
    aux_head = jnp.arange(AUX_PER_HEAD * N_FOX_HEADS) // AUX_PER_HEAD
    w_logit = _pad_lanes(wl[:, fw3:m0][:, aux_head]).astype(BF16)
    b_logit = _pad_lanes(b_forget[0][None, aux_head])
    fq, fqa, fk, fka, fvt, mq, mk, mvt, kmean = _inproj(
        x, norm_mix_g[0][None, :], w_main, w_vt, w_logit, b_logit, cosm, sina, sinb)
    fox = _fox(fq, fqa, fk, fka, fvt)
    moba = _moba(mq, mk, mvt, kmean)

    w_router = _pad_lanes(jnp.concatenate(
        [w_router_group[0], w_router_expert[0].reshape(d, N_EXPERTS)], axis=1))
    b_router = _pad_lanes(jnp.concatenate(
        [b_router_group[0], b_router_expert[0].reshape(N_EXPERTS)])[None, :])
    x2, h2, wts, route, counts = _postattn(
        x.reshape(t, d), fox.reshape(t, FOX_WIDTH), moba.reshape(t, MOBA_WIDTH),
        fox_out_g[0][None, :], moba_out_g[0][None, :], w_out[0].astype(BF16),
        norm_ffn_g[0][None, :], w_router, b_router)

    counts = counts[0, :N_EXPERTS]
    padded = (counts + TM_EXPERT - 1) // TM_EXPERT * TM_EXPERT
    ends = jnp.cumsum(padded)
    starts = ends - padded
    expert_ids = jnp.arange(N_EXPERTS, dtype=I32)[:, None, None]
    dest = route[2:4] + jnp.sum(jnp.where(route[None, 0:2] == expert_ids, starts[:, None, None], 0),
                                axis=0)
    dest3 = dest.reshape(2, t // TM_ROWS, TM_ROWS).transpose(1, 0, 2).reshape(t // TM_ROWS, 1, 2 * TM_ROWS)
    n_rows = n_tiles * TM_EXPERT
    pad_start = jnp.concatenate([starts + counts, ends[-1:]]).astype(I32)
    pad_len = jnp.concatenate([padded - counts, n_rows - ends[-1:]]).astype(I32)
    n_valid = ends[-1] // TM_EXPERT
    tile_src = jnp.minimum(jnp.arange(n_tiles, dtype=I32), n_valid - 1)
    tile_expert = jnp.sum(ends[None, :] <= (tile_src * TM_EXPERT)[:, None], axis=1).astype(I32)
    xs = _dispatch(pad_start, pad_len, dest3, h2, n_rows)
    ys = _experts(tile_expert, tile_src, n_valid.reshape(1).astype(I32), xs,
                  w_gate[0], w_up[0], w_down[0])
    return _combine(dest3, x2, wts, norm_final_g[None, :], ys).reshape(b, s, d)
```

```python
import math

import jax
import jax.numpy as jnp
from jax import lax
from jax.experimental import pallas as pl
from jax.experimental.pallas import tpu as pltpu

F32 = jnp.float32
BF16 = jnp.bfloat16
I32 = jnp.int32

HEAD_DIM = 64
N_FOX_HEADS = 8
N_MOBA_HEADS = 8
FOX_WIDTH = N_FOX_HEADS * HEAD_DIM
MOBA_WIDTH = N_MOBA_HEADS * HEAD_DIM
MOBA_BLOCK = 256
MOBA_TOPK = 3
ROPE_THETA = 500000.0
ROPE_DIM = HEAD_DIM // 4
N_GROUPS = 4
EXPERTS_PER_GROUP = 8
N_EXPERTS = N_GROUPS * EXPERTS_PER_GROUP
EPS = 1e-6

LANES = 128
SUBLANES = 8
BF16_SUBLANES = 16
LOG2_E = math.log2(math.e)
HEADS_PER_BLOCK = LANES // HEAD_DIM
BLOCKS_PER_STEP = 4
HEADS_PER_STEP = HEADS_PER_BLOCK * BLOCKS_PER_STEP
STEP_LANES = LANES * BLOCKS_PER_STEP
VMEM_LIMIT = 56 * 1024 * 1024
AUX_PER_HEAD = 6

TM_PROJ = 512
TQ = 256
TM_EXPERT = 256
TM_ROWS = 256
ISSUE_UNROLL = 8

NEG_INF = float("-inf")
MASKED = -1e30


def _params(sem):
    return pltpu.CompilerParams(dimension_semantics=sem, vmem_limit_bytes=VMEM_LIMIT)


def _rms(x, g):
    return x * lax.rsqrt(jnp.mean(x * x, axis=-1, keepdims=True) + EPS) * g


def _split3(x):
    hi = x.astype(BF16)
    r = x - hi.astype(F32)
    mid = r.astype(BF16)
    lo = (r - mid.astype(F32)).astype(BF16)
    return hi, mid, lo


def _dot(a, b):
    return jnp.dot(a, b, preferred_element_type=F32)


def _dot_nt(a, b):
    return lax.dot_general(a, b, (((1,), (1,)), ((), ())), preferred_element_type=F32)


def _inproj_kernel(x_ref, g_ref, w_ref, wvt_ref, wl_ref, bf_ref, kind_ref, cosm_ref, sina_ref, sinb_ref,
                   fq_ref, fqa_ref, fk_ref, fka_ref, fvt_ref, mq_ref, mk_ref, mvt_ref, kmean_ref,
                   carry_ref):
    j = pl.program_id(1)
    tm = x_ref.shape[1]
    tk = fvt_ref.shape[3]
    h = _rms(x_ref[0], g_ref[...]).astype(BF16)
    scale = HEAD_DIM ** -0.5 * LOG2_E

    def proj(seg):
        return _dot(h, w_ref[:, seg * FOX_WIDTH:(seg + 1) * FOX_WIDTH])

    fq_ref[0] = (proj(0) * scale).astype(BF16)
    fk_ref[0] = proj(1).astype(BF16)

    for vt_ref, seg in ((fvt_ref, 0), (mvt_ref, 1)):
        vt = _dot_nt(wvt_ref[seg], h).astype(BF16)
        for r in range(tm // tk):
            vt_ref[0, r] = vt[:, r * tk:(r + 1) * tk]

    cosm, sina, sinb = cosm_ref[...], sina_ref[...], sinb_ref[...]

    def rotary(t):
        outs = []
        for g in range(MOBA_WIDTH // LANES):
            tg = t[:, g * LANES:(g + 1) * LANES]
            outs.append(tg * cosm + pltpu.roll(tg, LANES - ROPE_DIM // 2, 1) * sina
                        + pltpu.roll(tg, ROPE_DIM // 2, 1) * sinb)
        return jnp.concatenate(outs, axis=1)

    mq_ref[0] = (rotary(proj(2)) * scale).astype(BF16)
    mk = rotary(proj(3))
    mk_ref[0] = mk.astype(BF16)
    nblk_tile = tm // MOBA_BLOCK
    means = [jnp.mean(mk[r * MOBA_BLOCK:(r + 1) * MOBA_BLOCK], axis=0, keepdims=True)
             for r in range(nblk_tile)]
    means += [jnp.zeros_like(means[0])] * (kmean_ref.shape[2] - nblk_tile)
    kmean_ref[0, 0] = jnp.concatenate(means, axis=0)

    z = _dot(h, wl_ref[...]) + bf_ref[...]
    log_f = jnp.minimum(z, 0.0) - jnp.log1p(jnp.exp(-jnp.abs(z)))

    @pl.when(j == 0)
    def _():
        carry_ref[...] = jnp.zeros_like(carry_ref)

    half = tm // 2
    row = lax.broadcasted_iota(I32, (half, half), 0)
    col = lax.broadcasted_iota(I32, (half, half), 1)
    tri = jnp.where(row >= col, 1.0, 0.0).astype(BF16)
    pieces = jnp.concatenate(_split3(log_f), axis=1)
    carry = carry_ref[...]
    cs = []
    for r in range(2):
        local = _dot(tri, pieces[r * half:(r + 1) * half])
        cs.append(local[:, :LANES] + local[:, LANES:2 * LANES] + local[:, 2 * LANES:] + carry)
        carry = cs[-1][half - 1:half, :]
    carry_ref[...] = carry
    c = jnp.concatenate(cs, axis=0) * LOG2_E

    hi = c.astype(BF16).astype(F32)
    mid = (c - hi).astype(BF16).astype(F32)
    lo = c - hi - mid
    kind = kind_ref[...]
    one = jnp.where(kind < AUX_PER_HEAD, 1.0, 0.0)
    pick = lambda base: jnp.where(kind == base, hi, jnp.where(kind == base + 1, mid,
                                  jnp.where(kind == base + 2, lo, 0.0)))
    fqa_ref[0] = (pick(3) + jnp.where(kind < 3, one, 0.0)).astype(BF16)
    fka_ref[0] = (jnp.where(kind >= 3, one, 0.0) - pick(0)).astype(BF16)


def _inproj(x, g, w_main, w_vt, w_logit, b_logit, cosm, sina, sinb):
    b, s, d = x.shape
    tm, tk = TM_PROJ, TQ
    lane = jnp.arange(LANES, dtype=I32)
    kind = jnp.where(lane < AUX_PER_HEAD * N_FOX_HEADS, lane % AUX_PER_HEAD, AUX_PER_HEAD)[None, :]
    act = jax.ShapeDtypeStruct((b, s, FOX_WIDTH), BF16)
    aux = jax.ShapeDtypeStruct((b, s, LANES), BF16)
    vt = jax.ShapeDtypeStruct((b, s // tk, FOX_WIDTH, tk), BF16)
    out_shape = [act, aux, act, aux, vt, act, act, vt,
                 jax.ShapeDtypeStruct((b, s // tm, SUBLANES, MOBA_WIDTH), F32)]
    act_spec = pl.BlockSpec((1, tm, FOX_WIDTH), lambda bi, j: (bi, j, 0))
    aux_spec = pl.BlockSpec((1, tm, LANES), lambda bi, j: (bi, j, 0))
    vt_spec = pl.BlockSpec((1, tm // tk, FOX_WIDTH, tk), lambda bi, j: (bi, j, 0, 0))
    tab_spec = pl.BlockSpec((tm, LANES), lambda bi, j: (j, 0))
    const2 = lambda bi, j: (0, 0)
    const3 = lambda bi, j: (0, 0, 0)
    *acts, kmean = pl.pallas_call(
        _inproj_kernel,
        grid=(b, s // tm),
        in_specs=[pl.BlockSpec((1, tm, d), lambda bi, j: (bi, j, 0)),
                  pl.BlockSpec((1, d), const2),
                  pl.BlockSpec(w_main.shape, const2),
                  pl.BlockSpec(w_vt.shape, const3),
                  pl.BlockSpec(w_logit.shape, const2),
                  pl.BlockSpec((1, LANES), const2), pl.BlockSpec((1, LANES), const2),
                  tab_spec, tab_spec, tab_spec],
        out_specs=[act_spec, aux_spec, act_spec, aux_spec, vt_spec, act_spec, act_spec, vt_spec,
                   pl.BlockSpec((1, 1, SUBLANES, MOBA_WIDTH), lambda bi, j: (bi, j, 0, 0))],
        out_shape=out_shape,
        scratch_shapes=[pltpu.VMEM((1, LANES), F32)],
        compiler_params=_params(("arbitrary", "arbitrary")),
        name="inproj",
    )(x, g, w_main, w_vt, w_logit, b_logit, kind, cosm, sina, sinb)
    kmean = kmean[:, :, :tm // MOBA_BLOCK].reshape(b, s // MOBA_BLOCK, MOBA_WIDTH)
    return (*acts, kmean)


def _softmax(heads, scores, m_ref):
    stats = []
    for hh, s in zip(heads, scores):
        m_prev = m_ref[hh]
        m_new = jnp.maximum(m_prev, jnp.max(s, axis=0, keepdims=True))
        m_ref[hh] = m_new
        stats.append((jnp.exp2(m_prev - m_new), m_new))
    return [(alpha, jnp.exp2(s - m_new).astype(BF16)) for (alpha, m_new), s in zip(stats, scores)]


def _values(weighted, vt, l_ref, acc_ref):
    ones = jnp.ones((BF16_SUBLANES, vt.shape[1]), BF16)
    for hh, (alpha, p) in enumerate(weighted):
        rows = slice(hh * HEAD_DIM, (hh + 1) * HEAD_DIM)
        pv = _dot(jnp.concatenate([vt[rows, :], ones], axis=0), p)
        acc_ref[rows, :] = alpha * acc_ref[rows, :] + pv[:HEAD_DIM]
        l_ref[hh] = alpha * l_ref[hh] + pv[HEAD_DIM:HEAD_DIM + 1]


def _attend_tiles(i, scores_of, vt_ref, m_ref, l_ref, acc_ref):
    heads = tuple(range(HEADS_PER_STEP))

    def block(tiles):
        scores = [scores_of(kt, diag, heads) for kt, diag in tiles]
        for (kt, _), s in zip(tiles, scores):
            _values(_softmax(heads, s, m_ref), vt_ref[0, kt], l_ref, acc_ref)

    @pl.when(i % 2 == 1)
    def _():
        block([(i, True), (i - 1, False)])

    @pl.when(i % 2 == 0)
    def _():
        block([(i, True)])

    def trip(j, c):
        block([(2 * j, False), (2 * j + 1, False)])
        return c

    lax.fori_loop(0, i // 2, trip, 0)


def _attn_init(m_ref, l_ref, acc_ref):
    m_ref[...] = jnp.full(m_ref.shape, NEG_INF, F32)
    l_ref[...] = jnp.zeros_like(l_ref)
    acc_ref[...] = jnp.zeros_like(acc_ref)


def _attn_finish(o_ref, l_ref, acc_ref):
    out_t = jnp.concatenate(
        [acc_ref[hh * HEAD_DIM:(hh + 1) * HEAD_DIM, :] / l_ref[hh] for hh in range(HEADS_PER_STEP)],
        axis=0)
    o_ref[0] = out_t.T


def _block(a, g):
    return a[:, g * LANES:(g + 1) * LANES]


def _per_head(a, width):
    first = lax.broadcasted_iota(I32, (a.shape[0], LANES), 1) < width
    zero = jnp.zeros((a.shape[0], LANES), a.dtype)
    out = []
    for g in range(BLOCKS_PER_STEP):
        blk = _block(a, g)
        out += [jnp.where(first, blk, zero), jnp.where(first, zero, blk)]
    return out


def _key_le_query(tq):
    return lax.broadcasted_iota(I32, (tq, tq), 0) <= lax.broadcasted_iota(I32, (tq, tq), 1)


def _fox_kernel(q_ref, qa_ref, k_ref, ka_ref, vt_ref, o_ref, m_ref, l_ref, acc_ref):
    i = pl.program_id(2)
    tq = q_ref.shape[1]
    qa = qa_ref[0]
    lane = lax.broadcasted_iota(I32, qa.shape, 1) - pl.program_id(1) * (HEADS_PER_STEP * AUX_PER_HEAD)
    own_aux = lambda hh: jnp.logical_and(lane >= hh * AUX_PER_HEAD, lane < (hh + 1) * AUX_PER_HEAD)
    qq = [jnp.concatenate([qm, jnp.where(own_aux(hh), qa, jnp.zeros_like(qa))], axis=1)
          for hh, qm in enumerate(_per_head(q_ref[0], HEAD_DIM))]
    causal = _key_le_query(tq)
    _attn_init(m_ref, l_ref, acc_ref)

    def scores_of(kt, diag, heads):
        ks = pl.multiple_of(kt * tq, tq)
        ka = ka_ref[0, pl.ds(ks, tq), :]
        kk = {g: jnp.concatenate([k_ref[0, pl.ds(ks, tq), g * LANES:(g + 1) * LANES], ka], axis=1)
              for g in sorted({hh // HEADS_PER_BLOCK for hh in heads})}
        scores = [_dot_nt(kk[hh // HEADS_PER_BLOCK], qq[hh]) for hh in heads]
        if diag:
            scores = [jnp.where(causal, s, NEG_INF) for s in scores]
        return tuple(scores)

    _attend_tiles(i, scores_of, vt_ref, m_ref, l_ref, acc_ref)
    _attn_finish(o_ref, l_ref, acc_ref)


def _moba_kernel(q_ref, k_ref, hot_ref, vt_ref, kmean_ref, o_ref, m_ref, l_ref, acc_ref):
    i = pl.program_id(2)
    tq = q_ref.shape[1]
    nblk = kmean_ref.shape[1]
    qs = _per_head(q_ref[0], HEAD_DIM)
    causal = _key_le_query(tq)
    _attn_init(m_ref, l_ref, acc_ref)

    km_parts = _split3(kmean_ref[0])
    blk = lax.broadcasted_iota(I32, (nblk, tq), 0).astype(F32)
    past = blk < i.astype(F32)
    masks = []
    for hh in range(HEADS_PER_STEP):
        gate = sum(_dot_nt(_block(part, hh // HEADS_PER_BLOCK), qs[hh]) for part in km_parts)
        sel = jnp.zeros((nblk, tq), jnp.bool_)
        for _ in range(MOBA_TOPK):
            remaining = jnp.logical_and(past, jnp.logical_not(sel))
            g = jnp.where(remaining, gate, NEG_INF)
            first = jnp.min(jnp.where(g == jnp.max(g, axis=0, keepdims=True), blk, float(nblk)),
                            axis=0, keepdims=True)
            sel = jnp.logical_or(sel, jnp.logical_and(blk == first, remaining))
        masks.append(jnp.where(sel, 0.0, MASKED))
    unused = LANES - HEADS_PER_STEP * nblk
    masks += [jnp.zeros((unused, tq), F32)] if unused else []
    qa = jnp.concatenate(masks, axis=0).T.astype(BF16)
    lane = lax.broadcasted_iota(I32, qa.shape, 1)
    own = lambda hh: jnp.logical_and(lane >= hh * nblk, lane < (hh + 1) * nblk)
    qq = [jnp.concatenate([qs[hh], jnp.where(own(hh), qa, jnp.zeros_like(qa))], axis=1)
          for hh in range(HEADS_PER_STEP)]

    def scores_of(kt, diag, heads):
        ks = pl.multiple_of(kt * tq, tq)
        k = lambda hh: k_ref[0, pl.ds(ks, tq), (hh // HEADS_PER_BLOCK) * LANES:(hh // HEADS_PER_BLOCK + 1) * LANES]
        if diag:
            return tuple(jnp.where(causal, _dot_nt(k(hh), qs[hh]), NEG_INF) for hh in heads)
        hot = hot_ref[pl.ds(ks, tq), :]
        return tuple(_dot_nt(jnp.concatenate([k(hh), hot], axis=1), qq[hh]) for hh in heads)

    _attend_tiles(i, scores_of, vt_ref, m_ref, l_ref, acc_ref)
    _attn_finish(o_ref, l_ref, acc_ref)


def _attn_scratch(tq):
    return [pltpu.VMEM((HEADS_PER_STEP, 1, tq), F32), pltpu.VMEM((HEADS_PER_STEP, 1, tq), F32),
            pltpu.VMEM((STEP_LANES, tq), F32)]


def _attn_specs(s, tq):
    q_spec = pl.BlockSpec((1, tq, STEP_LANES), lambda bi, hb, i: (bi, i, hb))
    k_spec = pl.BlockSpec((1, s, STEP_LANES), lambda bi, hb, i: (bi, 0, hb))
    vt_spec = pl.BlockSpec((1, s // tq, STEP_LANES, tq), lambda bi, hb, i: (bi, 0, hb, 0))
    return q_spec, k_spec, vt_spec


def _fox(q, qa, k, ka, vt):
    b, s, width = q.shape
    tq = TQ
    q_spec, k_spec, vt_spec = _attn_specs(s, tq)
    return pl.pallas_call(
        _fox_kernel,
        grid=(b, width // STEP_LANES, s // tq),
        in_specs=[q_spec, pl.BlockSpec((1, tq, LANES), lambda bi, hb, i: (bi, i, 0)),
                  k_spec, pl.BlockSpec((1, s, LANES), lambda bi, hb, i: (bi, 0, 0)), vt_spec],
        out_specs=q_spec,
        out_shape=jax.ShapeDtypeStruct((b, s, width), F32),
        scratch_shapes=_attn_scratch(tq),
        compiler_params=_params(("arbitrary", "arbitrary", "arbitrary")),
        name="fox",
    )(q, qa, k, ka, vt)


def _moba(q, k, vt, kmean):
    b, s, width = q.shape
    tq = TQ
    nblk = kmean.shape[1]
    assert HEADS_PER_STEP * nblk <= LANES, "one aux lane per (head, key block)"
    lane = jnp.arange(LANES)
    hot = jnp.logical_and(lane[None, :] < HEADS_PER_STEP * nblk,
                          lane[None, :] % nblk == jnp.arange(s)[:, None] // MOBA_BLOCK).astype(BF16)
    q_spec, k_spec, vt_spec = _attn_specs(s, tq)
    return pl.pallas_call(
        _moba_kernel,
        grid=(b, width // STEP_LANES, s // tq),
        in_specs=[q_spec, k_spec, pl.BlockSpec((s, LANES), lambda bi, hb, i: (0, 0)), vt_spec,
                  pl.BlockSpec((1, nblk, STEP_LANES), lambda bi, hb, i: (bi, 0, hb))],
        out_specs=q_spec,
        out_shape=jax.ShapeDtypeStruct((b, s, width), F32),
        scratch_shapes=_attn_scratch(tq),
        compiler_params=_params(("arbitrary", "arbitrary", "arbitrary")),
        name="moba",
    )(q, k, hot, vt, kmean)


def _postattn_kernel(x_ref, fox_ref, moba_ref, gf_ref, gm_ref, wo_ref, gn_ref, wr_ref, br_ref,
                     x2_ref, h2_ref, wts_ref, route_ref, cnt_ref, carry_ref):
    t = pl.program_id(0)
    tm = x_ref.shape[0]
    fw = fox_ref.shape[1]
    mixed_f = _rms(fox_ref[...], gf_ref[...]).astype(BF16)
    mixed_m = _rms(moba_ref[...], gm_ref[...]).astype(BF16)
    x2 = x_ref[...] + _dot(mixed_f, wo_ref[:fw, :]) + _dot(mixed_m, wo_ref[fw:, :])
    x2_ref[...] = x2
    h2 = _rms(x2, gn_ref[...])
    _store_packed(h2_ref, h2)

    h_hi = h2.astype(BF16)
    h_lo = (h2 - h_hi.astype(F32)).astype(BF16)
    by_hi = _dot(h_hi, wr_ref[...])
    logits = by_hi[:, :LANES] + by_hi[:, LANES:] + _dot(h_lo, wr_ref[:, :LANES]) + br_ref[...]
    lane = lax.broadcasted_iota(I32, (tm, LANES), 1).astype(F32)

    def first_max(vals):
        mx = jnp.max(vals, axis=1, keepdims=True)
        return mx, jnp.min(jnp.where(vals == mx, lane, float(LANES)), axis=1, keepdims=True)

    gl = jnp.where(lane < N_GROUPS, logits, NEG_INF)
    gmax, g_idx = first_max(gl)
    g_top = 1.0 / jnp.sum(jnp.exp(gl - gmax), axis=1, keepdims=True)
    e_lo = N_GROUPS + EXPERTS_PER_GROUP * g_idx
    el = jnp.where(jnp.logical_and(lane >= e_lo, lane < e_lo + EXPERTS_PER_GROUP), logits, NEG_INF)
    emax, i1 = first_max(el)
    esum = jnp.sum(jnp.exp(el - emax), axis=1, keepdims=True)
    e2max, i2 = first_max(jnp.where(lane == i1, NEG_INF, el))
    p1 = 1.0 / esum
    p2 = jnp.exp(e2max - emax) / esum
    w1 = p1 / (p1 + p2) * g_top
    w2 = p2 / (p1 + p2) * g_top
    e1 = i1 - N_GROUPS
    e2 = i2 - N_GROUPS
    wts_ref[...] = jnp.where(lane == 0.0, w1, w2)[:, :2]

    @pl.when(t == 0)
    def _():
        carry_ref[...] = jnp.zeros_like(carry_ref)

    row = lax.broadcasted_iota(I32, (tm, tm), 0)
    col = lax.broadcasted_iota(I32, (tm, tm), 1)
    strict = jnp.where(row > col, 1.0, 0.0).astype(BF16)
    hit1 = lane == e1
    hit2 = lane == e2
    oh1 = jnp.where(hit1, 1.0, 0.0)
    oh2 = jnp.where(hit2, 1.0, 0.0)
    tot1 = jnp.sum(oh1, axis=0, keepdims=True)
    tot2 = jnp.sum(oh2, axis=0, keepdims=True)
    base = carry_ref[...]
    before = _dot(strict, jnp.concatenate([oh1, oh2], axis=1).astype(BF16))
    before1 = before[:, :LANES] + base
    before2 = before[:, LANES:] + (base + tot1)
    r1 = jnp.sum(jnp.where(hit1, before1, 0.0), axis=1, keepdims=True)
    r2 = jnp.sum(jnp.where(hit2, before2, 0.0), axis=1, keepdims=True)
    record = jnp.where(lane == 0.0, e1, jnp.where(lane == 1.0, e2, jnp.where(lane == 2.0, r1, r2)))
    route_ref[...] = record.T[:SUBLANES, :].astype(I32)
    total = base + tot1 + tot2
    carry_ref[...] = total
    cnt_ref[...] = total.astype(I32)


def _postattn(x, fox, moba, gf, gm, wo, gn, wr, br):
    t, d = x.shape
    tm = TM_PROJ
    fw = fox.shape[1]
    const = lambda i: (0, 0)
    rows = lambda i: (i, 0)
    return pl.pallas_call(
        _postattn_kernel,
        grid=(t // tm,),
        in_specs=[pl.BlockSpec((tm, d), rows), pl.BlockSpec((tm, fw), rows),
                  pl.BlockSpec((tm, moba.shape[1]), rows),
                  pl.BlockSpec((1, fw), const), pl.BlockSpec((1, moba.shape[1]), const),
                  pl.BlockSpec(wo.shape, const), pl.BlockSpec((1, d), const),
                  pl.BlockSpec(wr.shape, const), pl.BlockSpec((1, LANES), const)],
        out_specs=[pl.BlockSpec((tm, d), rows), pl.BlockSpec((tm * PACKED_ROWS, LANES), rows),
                   pl.BlockSpec((tm, 2), rows), pl.BlockSpec((SUBLANES, tm), lambda i: (0, i)),
                   pl.BlockSpec((1, LANES), const)],
        out_shape=[jax.ShapeDtypeStruct((t, d), F32), jax.ShapeDtypeStruct((t * PACKED_ROWS, LANES), I32),
                   jax.ShapeDtypeStruct((t, 2), F32), jax.ShapeDtypeStruct((SUBLANES, t), I32),
                   jax.ShapeDtypeStruct((1, LANES), I32)],
        scratch_shapes=[pltpu.VMEM((1, LANES), F32)],
        compiler_params=_params(("arbitrary",)),
        name="postattn",
    )(x, fox, moba, gf, gm, wo, gn, wr, br)


def _store_rows(ref, val):
    for g in range(SUBLANES):
        ref[pl.ds(g, val.shape[0], stride=SUBLANES), :] = val[:, g * LANES:(g + 1) * LANES]


def _load_rows(ref):
    tokens = ref.shape[0] // SUBLANES
    return jnp.concatenate([ref[pl.ds(g, tokens, stride=SUBLANES), :] for g in range(SUBLANES)], axis=1)


PACKED_ROWS = 4


def _store_packed(ref, val):
    half = val.shape[1] // 2
    bits = lambda v: lax.bitcast_convert_type(v.astype(BF16).astype(F32), I32)
    word = jnp.bitwise_or(lax.shift_right_logical(bits(val[:, :half]), 16),
                          jnp.bitwise_and(bits(val[:, half:]), -65536))
    for g in range(PACKED_ROWS):
        ref[pl.ds(g, val.shape[0], stride=PACKED_ROWS), :] = word[:, g * LANES:(g + 1) * LANES]


def _load_packed(ref):
    tokens = ref.shape[0] // PACKED_ROWS
    words = [ref[pl.ds(g, tokens, stride=PACKED_ROWS), :] for g in range(PACKED_ROWS)]
    low = [lax.bitcast_convert_type(lax.shift_left(w, 16), F32) for w in words]
    high = [lax.bitcast_convert_type(jnp.bitwise_and(w, -65536), F32) for w in words]
    return jnp.concatenate(low + high, axis=1).astype(BF16)


def _row_copy(src, src_row, dst, dst_row, sem, rows=SUBLANES):
    window = lambda r: pl.ds(pl.multiple_of(r * rows, rows), rows)
    return pltpu.make_async_copy(src.at[window(src_row)], dst.at[window(dst_row)], sem)


RING = 3


def _dispatch_kernel(pad_start_ref, pad_len_ref, dest_ref, h_ref, xs_ref,
                     ring_ref, zero_ref, fetch_sems, scatter_sems, pad_sem):
    i = pl.program_id(0)
    last = pl.num_programs(0) - 1
    tm = dest_ref.shape[2] // 2
    tile_rows = tm * PACKED_ROWS
    row_copy = lambda *a: _row_copy(*a, rows=PACKED_ROWS)

    def fetch(tile):
        start = pl.multiple_of(tile * tile_rows, tile_rows)
        slot = lax.rem(tile, RING)
        return pltpu.make_async_copy(h_ref.at[pl.ds(start, tile_rows)], ring_ref.at[slot],
                                     fetch_sems.at[slot])

    @pl.when(i == 0)
    def _():
        fetch(0).start()
        zero_ref[...] = jnp.zeros_like(zero_ref)

        def fill(e, c):
            def put(j, c2):
                row_copy(zero_ref, 0, xs_ref, pad_start_ref[e] + j, pad_sem).start()
                return c2

            def got(j, c2):
                row_copy(zero_ref, 0, xs_ref, 0, pad_sem).wait()
                return c2

            lax.fori_loop(0, pad_len_ref[e], put, 0)
            lax.fori_loop(0, pad_len_ref[e], got, 0)
            return c

        lax.fori_loop(0, pad_start_ref.shape[0], fill, 0)

    @pl.when(i < last)
    def _():
        fetch(i + 1).start()

    fetch(i).wait()
    src = ring_ref.at[lax.rem(i, RING)]

    def issue(r, c):
        for k in range(2):
            row_copy(src, r, xs_ref, dest_ref[0, 0, k * tm + r], scatter_sems.at[i % 2]).start(priority=k)
        return c

    lax.fori_loop(0, tm, issue, 0, unroll=ISSUE_UNROLL)

    def drain(parity):
        def one(r, c):
            row_copy(src, 0, xs_ref, 0, scatter_sems.at[parity]).wait()
            return c

        lax.fori_loop(0, 2 * tm, one, 0, unroll=ISSUE_UNROLL)

    @pl.when(i > 0)
    def _():
        drain((i - 1) % 2)

    @pl.when(i == last)
    def _():
        drain(i % 2)


def _dispatch(pad_start, pad_len, dest3, h2, n_rows):
    tm = dest3.shape[2] // 2
    grid_spec = pltpu.PrefetchScalarGridSpec(
        num_scalar_prefetch=2,
        grid=(dest3.shape[0],),
        in_specs=[pl.BlockSpec((1, 1, 2 * tm), lambda i, ps, pn: (i, 0, 0), memory_space=pltpu.SMEM),
                  pl.BlockSpec(memory_space=pl.ANY)],
        out_specs=pl.BlockSpec(memory_space=pl.ANY),
        scratch_shapes=[pltpu.VMEM((RING, tm * PACKED_ROWS, LANES), I32), pltpu.VMEM((SUBLANES, LANES), I32),
                        pltpu.SemaphoreType.DMA((RING,)), pltpu.SemaphoreType.DMA((2,)),
                        pltpu.SemaphoreType.DMA(())],
    )
    return pl.pallas_call(
        _dispatch_kernel,
        grid_spec=grid_spec,
        out_shape=jax.ShapeDtypeStruct((n_rows * PACKED_ROWS, LANES), I32),
        compiler_params=_params(("arbitrary",)),
        name="dispatch",
    )(pad_start, pad_len, dest3, h2)


def _experts_kernel(te_ref, ts_ref, nv_ref, xs_ref, wg_ref, wu_ref, wd_ref, ys_ref,
                    wgb_ref, wub_ref, wdb_ref):
    del ts_ref
    t = pl.program_id(0)

    @pl.when(jnp.logical_or(t == 0, te_ref[t] != te_ref[jnp.maximum(t - 1, 0)]))
    def _():
        wgb_ref[...] = wg_ref[0].astype(BF16)
        wub_ref[...] = wu_ref[0].astype(BF16)
        wdb_ref[...] = wd_ref[0].astype(BF16)

    @pl.when(t < nv_ref[0])
    def _():
        xb = _load_packed(xs_ref)
        a = _dot(xb, wgb_ref[...])
        u = _dot(xb, wub_ref[...])
        act = (a * jax.nn.sigmoid(a) * u).astype(BF16)
        _store_rows(ys_ref, _dot(act, wdb_ref[...]))

    @pl.when(t >= nv_ref[0])
    def _():
        ys_ref[...] = jnp.zeros_like(ys_ref)


def _experts(tile_expert, tile_src, n_valid, xs, wg, wu, wd):
    tm = TM_EXPERT
    n_tiles = xs.shape[0] // (tm * PACKED_ROWS)
    _, d, f = wg.shape
    row_block = (tm * SUBLANES, LANES)
    grid_spec = pltpu.PrefetchScalarGridSpec(
        num_scalar_prefetch=3,
        grid=(n_tiles,),
        in_specs=[pl.BlockSpec((tm * PACKED_ROWS, LANES), lambda t, te, ts, nv: (ts[t], 0)),
                  pl.BlockSpec((1, d, f), lambda t, te, ts, nv: (te[t], 0, 0)),
                  pl.BlockSpec((1, d, f), lambda t, te, ts, nv: (te[t], 0, 0)),
                  pl.BlockSpec((1, f, d), lambda t, te, ts, nv: (te[t], 0, 0))],
        out_specs=pl.BlockSpec(row_block, lambda t, te, ts, nv: (t, 0)),
        scratch_shapes=[pltpu.VMEM((d, f), BF16), pltpu.VMEM((d, f), BF16), pltpu.VMEM((f, d), BF16)],
    )
    return pl.pallas_call(
        _experts_kernel,
        grid_spec=grid_spec,
        out_shape=jax.ShapeDtypeStruct((n_tiles * tm * SUBLANES, LANES), F32),
        compiler_params=_params(("arbitrary",)),
        name="experts",
    )(tile_expert, tile_src, n_valid, xs, wg, wu, wd)


def _combine_kernel(dest_ref, next_ref, x2_ref, wts_ref, g_ref, ys_ref, o_ref, buf_ref, sems):
    i = pl.program_id(0)
    tm = dest_ref.shape[2] // 2
    slot = i % 2

    def gather(d_ref, to):
        def issue(r, c):
            for k in range(2):
                _row_copy(ys_ref, d_ref[0, 0, k * tm + r], buf_ref.at[to, k], r,
                          sems.at[to]).start(priority=k)
            return c

        lax.fori_loop(0, tm, issue, 0, unroll=ISSUE_UNROLL)

    @pl.when(i == 0)
    def _():
        gather(dest_ref, 0)

    @pl.when(i + 1 < pl.num_programs(0))
    def _():
        gather(next_ref, 1 - slot)

    def drain(r, c):
        _row_copy(ys_ref, 0, buf_ref.at[slot, 0], 0, sems.at[slot]).wait()
        return c

    lax.fori_loop(0, 2 * tm, drain, 0, unroll=ISSUE_UNROLL)
    w = wts_ref[...]
    y = (x2_ref[...] + w[:, 0:1] * _load_rows(buf_ref.at[slot, 0])
         + w[:, 1:2] * _load_rows(buf_ref.at[slot, 1]))
    o_ref[...] = _rms(y, g_ref[...])


def _combine(dest3, x2, wts, g, ys):
    t, d = x2.shape
    tm = dest3.shape[2] // 2
    rows = lambda i: (i, 0)
    n = t // tm
    return pl.pallas_call(
        _combine_kernel,
        grid=(n,),
        in_specs=[pl.BlockSpec((1, 1, 2 * tm), lambda i: (i, 0, 0), memory_space=pltpu.SMEM),
                  pl.BlockSpec((1, 1, 2 * tm), lambda i: (jnp.minimum(i + 1, n - 1), 0, 0),
                               memory_space=pltpu.SMEM),
                  pl.BlockSpec((tm, d), rows), pl.BlockSpec((tm, 2), rows),
                  pl.BlockSpec((1, d), lambda i: (0, 0)),
                  pl.BlockSpec(memory_space=pl.ANY)],
        out_specs=pl.BlockSpec((tm, d), rows),
        out_shape=jax.ShapeDtypeStruct((t, d), F32),
        scratch_shapes=[pltpu.VMEM((2, 2, tm * SUBLANES, LANES), F32), pltpu.SemaphoreType.DMA((2,))],
        compiler_params=_params(("arbitrary",)),
        name="combine",
    )(dest3, dest3, x2, wts, g, ys)


def _rotary_tables(seq):
    half = ROPE_DIM // 2
    inv_freq = ROPE_THETA ** (-jnp.arange(half, dtype=F32) / half)
    ang = jnp.arange(seq, dtype=F32)[:, None] * inv_freq[None, :]
    cos, sin = jnp.cos(ang), jnp.sin(ang)
    ones = jnp.ones((seq, HEAD_DIM - ROPE_DIM), F32)
    zeros = jnp.zeros((seq, HEAD_DIM - ROPE_DIM), F32)
    zh = jnp.zeros((seq, half), F32)
    cosm = jnp.concatenate([cos, cos, ones], axis=1)
    sina = jnp.concatenate([-sin, zh, zeros], axis=1)
    sinb = jnp.concatenate([zh, sin, zeros], axis=1)
    tile = lambda a: jnp.tile(a, (1, HEADS_PER_BLOCK))
    return tile(cosm), tile(sina), tile(sinb)


def _pad_lanes(a):
    return jnp.pad(a, ((0, 0), (0, LANES - a.shape[1])))


def kernel(x, norm_mix_g, w_in, b_forget, fox_out_g, moba_out_g, w_out, norm_ffn_g, w_router_group,
           b_router_group, w_router_expert, b_router_expert, w_gate, w_up, w_down, norm_final_g):
    b, s, d = x.shape
    t = b * s
    assert w_in.shape[0] == 1, "the closing RMSNorm is fused into the only layer's combine step"
    cosm, sina, sinb = _rotary_tables(s)
    n_tiles = (2 * t) // TM_EXPERT + N_EXPERTS
    fw3 = 3 * FOX_WIDTH
    m0 = fw3 + N_FOX_HEADS
    wl = w_in[0]
    w_main = jnp.concatenate([wl[:, :2 * FOX_WIDTH], wl[:, m0:m0 + 2 * MOBA_WIDTH]],
                             axis=1).astype(BF16)
    w_vt = jnp.stack([wl[:, 2 * FOX_WIDTH:fw3].T, wl[:, m0 + 2 * MOBA_WIDTH:].T]).astype(BF16)
    aux_head = jnp.arange(AUX_PER_HEAD * N_FOX_HEADS) // AUX_PER_HEAD
    w_logit = _pad_lanes(wl[:, fw3:m0][:, aux_head]).astype(BF16)
    b_logit = _pad_lanes(b_forget[0][None, aux_head])
    fq, fqa, fk, fka, fvt, mq, mk, mvt, kmean = _inproj(
        x, norm_mix_g[0][None, :], w_main, w_vt, w_logit, b_logit, cosm, sina, sinb)
    fox = _fox(fq, fqa, fk, fka, fvt)
    moba = _moba(mq, mk, mvt, kmean)

    w_router = _pad_lanes(jnp.concatenate(
        [w_router_group[0], w_router_expert[0].reshape(d, N_EXPERTS)], axis=1))
    w_router_hi = w_router.astype(BF16)
    w_router = jnp.concatenate([w_router_hi, (w_router - w_router_hi.astype(F32)).astype(BF16)], axis=1)
    b_router =_pad_lanes(jnp.concatenate(
        [b_router_group[0], b_router_expert[0].reshape(N_EXPERTS)])[None, :])
    x2, h2, wts, route, counts = _postattn(
        x.reshape(t, d), fox.reshape(t, FOX_WIDTH), moba.reshape(t, MOBA_WIDTH),
        fox_out_g[0][None, :], moba_out_g[0][None, :], w_out[0].astype(BF16),
        norm_ffn_g[0][None, :], w_router, b_router)

    counts = counts[0, :N_EXPERTS]
    padded = (counts + TM_EXPERT - 1) // TM_EXPERT * TM_EXPERT
    ends = jnp.cumsum(padded)
    starts = ends - padded
    expert_ids = jnp.arange(N_EXPERTS, dtype=I32)[:, None, None]
    dest = route[2:4] + jnp.sum(jnp.where(route[None, 0:2] == expert_ids, starts[:, None, None], 0),
                                axis=0)
    dest3 = dest.reshape(2, t // TM_ROWS, TM_ROWS).transpose(1, 0, 2).reshape(t // TM_ROWS, 1, 2 * TM_ROWS)
    n_rows = n_tiles * TM_EXPERT
    pad_start = jnp.concatenate([starts + counts, ends[-1:]]).astype(I32)
    pad_len = jnp.concatenate([padded - counts, n_rows - ends[-1:]]).astype(I32)
    n_valid = ends[-1] // TM_EXPERT
    tile_src = jnp.minimum(jnp.arange(n_tiles, dtype=I32), n_valid - 1)
    tile_expert = jnp.sum(ends[None, :] <= (tile_src * TM_EXPERT)[:, None], axis=1).astype(I32)
    xs = _dispatch(pad_start, pad_len, dest3, h2, n_rows)
    ys = _experts(tile_expert, tile_src, n_valid.reshape(1).astype(I32), xs,
                  w_gate[0], w_up[0], w_down[0])
    return _combine(dest3, x2, wts, norm_final_g[None, :], ys).reshape(b, s, d)
```

```python
import math

import jax
import jax.numpy as jnp
from jax import lax
from jax.experimental import pallas as pl
from jax.experimental.pallas import tpu as pltpu

F32 = jnp.float32
BF16 = jnp.bfloat16
I32 = jnp.int32

HEAD_DIM = 64
N_FOX_HEADS = 8
N_MOBA_HEADS = 8
FOX_WIDTH = N_FOX_HEADS * HEAD_DIM
MOBA_WIDTH = N_MOBA_HEADS * HEAD_DIM
MOBA_BLOCK = 256
MOBA_TOPK = 3
ROPE_THETA = 500000.0
ROPE_DIM = HEAD_DIM // 4
N_GROUPS = 4
EXPERTS_PER_GROUP = 8
N_EXPERTS = N_GROUPS * EXPERTS_PER_GROUP
EPS = 1e-6

LANES = 128
SUBLANES = 8
BF16_SUBLANES = 16
LOG2_E = math.log2(math.e)
HEADS_PER_BLOCK = LANES // HEAD_DIM
BLOCKS_PER_STEP = 4
HEADS_PER_STEP = HEADS_PER_BLOCK * BLOCKS_PER_STEP
STEP_LANES = LANES * BLOCKS_PER_STEP
VMEM_LIMIT = 56 * 1024 * 1024
AUX_PER_HEAD = 6

TM_PROJ = 512
TQ = 256
TM_EXPERT = 256
TM_ROWS = 256
ISSUE_UNROLL = 8

NEG_INF = float("-inf")
MASKED = -1e30


def _params(sem):
    return pltpu.CompilerParams(dimension_semantics=sem, vmem_limit_bytes=VMEM_LIMIT)


def _rms(x, g):
    return x * lax.rsqrt(jnp.mean(x * x, axis=-1, keepdims=True) + EPS) * g


def _split3(x):
    hi = x.astype(BF16)
    r = x - hi.astype(F32)
    mid = r.astype(BF16)
    lo = (r - mid.astype(F32)).astype(BF16)
    return hi, mid, lo


def _dot(a, b):
    return jnp.dot(a, b, preferred_element_type=F32)


def _dot_nt(a, b):
    return lax.dot_general(a, b, (((1,), (1,)), ((), ())), preferred_element_type=F32)


def _inproj_kernel(x_ref, g_ref, w_ref, wvt_ref, wl_ref, bf_ref, kind_ref, cosm_ref, sina_ref, sinb_ref,
                   fq_ref, fqa_ref, fk_ref, fka_ref, fvt_ref, mq_ref, mk_ref, mvt_ref, kmean_ref,
                   carry_ref):
    j = pl.program_id(1)
    tm = x_ref.shape[1]
    tk = fvt_ref.shape[3]
    h = _rms(x_ref[0], g_ref[...]).astype(BF16)
    scale = HEAD_DIM ** -0.5 * LOG2_E

    def proj(seg):
        return _dot(h, w_ref[:, seg * FOX_WIDTH:(seg + 1) * FOX_WIDTH])

    fq_ref[0] = (proj(0) * scale).astype(BF16)
    fk_ref[0] = proj(1).astype(BF16)

    for vt_ref, seg in ((fvt_ref, 0), (mvt_ref, 1)):
        vt = _dot_nt(wvt_ref[seg], h).astype(BF16)
        for r in range(tm // tk):
            vt_ref[0, r] = vt[:, r * tk:(r + 1) * tk]

    cosm, sina, sinb = cosm_ref[...], sina_ref[...], sinb_ref[...]

    def rotary(t):
        outs = []
        for g in range(MOBA_WIDTH // LANES):
            tg = t[:, g * LANES:(g + 1) * LANES]
            outs.append(tg * cosm + pltpu.roll(tg, LANES - ROPE_DIM // 2, 1) * sina
                        + pltpu.roll(tg, ROPE_DIM // 2, 1) * sinb)
        return jnp.concatenate(outs, axis=1)

    mq_ref[0] = (rotary(proj(2)) * scale).astype(BF16)
    mk = rotary(proj(3))
    mk_ref[0] = mk.astype(BF16)
    nblk_tile = tm // MOBA_BLOCK
    means = [jnp.mean(mk[r * MOBA_BLOCK:(r + 1) * MOBA_BLOCK], axis=0, keepdims=True)
             for r in range(nblk_tile)]
    means += [jnp.zeros_like(means[0])] * (kmean_ref.shape[2] - nblk_tile)
    kmean_ref[0, 0] = jnp.concatenate(means, axis=0)

    z = _dot(h, wl_ref[...]) + bf_ref[...]
    log_f = jnp.minimum(z, 0.0) - jnp.log1p(jnp.exp(-jnp.abs(z)))

    @pl.when(j == 0)
    def _():
        carry_ref[...] = jnp.zeros_like(carry_ref)

    half = tm // 2
    row = lax.broadcasted_iota(I32, (half, half), 0)
    col = lax.broadcasted_iota(I32, (half, half), 1)
    tri = jnp.where(row >= col, 1.0, 0.0).astype(BF16)
    pieces = jnp.concatenate(_split3(log_f), axis=1)
    carry = carry_ref[...]
    cs = []
    for r in range(2):
        local = _dot(tri, pieces[r * half:(r + 1) * half])
        cs.append(local[:, :LANES] + local[:, LANES:2 * LANES] + local[:, 2 * LANES:] + carry)
        carry = cs[-1][half - 1:half, :]
    carry_ref[...] = carry
    c = jnp.concatenate(cs, axis=0) * LOG2_E

    hi = c.astype(BF16).astype(F32)
    mid = (c - hi).astype(BF16).astype(F32)
    lo = c - hi - mid
    kind = kind_ref[...]
    one = jnp.where(kind < AUX_PER_HEAD, 1.0, 0.0)
    pick = lambda base: jnp.where(kind == base, hi, jnp.where(kind == base + 1, mid,
                                  jnp.where(kind == base + 2, lo, 0.0)))
    fqa_ref[0] = (pick(3) + jnp.where(kind < 3, one, 0.0)).astype(BF16)
    fka_ref[0] = (jnp.where(kind >= 3, one, 0.0) - pick(0)).astype(BF16)


def _inproj(x, g, w_main, w_vt, w_logit, b_logit, cosm, sina, sinb):
    b, s, d = x.shape
    tm, tk = TM_PROJ, TQ
    lane = jnp.arange(LANES, dtype=I32)
    kind = jnp.where(lane < AUX_PER_HEAD * N_FOX_HEADS, lane % AUX_PER_HEAD, AUX_PER_HEAD)[None, :]
    act = jax.ShapeDtypeStruct((b, s, FOX_WIDTH), BF16)
    aux = jax.ShapeDtypeStruct((b, s, LANES), BF16)
    vt = jax.ShapeDtypeStruct((b, s // tk, FOX_WIDTH, tk), BF16)
    out_shape = [act, aux, act, aux, vt, act, act, vt,
                 jax.ShapeDtypeStruct((b, s // tm, SUBLANES, MOBA_WIDTH), F32)]
    act_spec = pl.BlockSpec((1, tm, FOX_WIDTH), lambda bi, j: (bi, j, 0))
    aux_spec = pl.BlockSpec((1, tm, LANES), lambda bi, j: (bi, j, 0))
    vt_spec = pl.BlockSpec((1, tm // tk, FOX_WIDTH, tk), lambda bi, j: (bi, j, 0, 0))
    tab_spec = pl.BlockSpec((tm, LANES), lambda bi, j: (j, 0))
    const2 = lambda bi, j: (0, 0)
    const3 = lambda bi, j: (0, 0, 0)
    *acts, kmean = pl.pallas_call(
        _inproj_kernel,
        grid=(b, s // tm),
        in_specs=[pl.BlockSpec((1, tm, d), lambda bi, j: (bi, j, 0)),
                  pl.BlockSpec((1, d), const2),
                  pl.BlockSpec(w_main.shape, const2),
                  pl.BlockSpec(w_vt.shape, const3),
                  pl.BlockSpec(w_logit.shape, const2),
                  pl.BlockSpec((1, LANES), const2), pl.BlockSpec((1, LANES), const2),
                  tab_spec, tab_spec, tab_spec],
        out_specs=[act_spec, aux_spec, act_spec, aux_spec, vt_spec, act_spec, act_spec, vt_spec,
                   pl.BlockSpec((1, 1, SUBLANES, MOBA_WIDTH), lambda bi, j: (bi, j, 0, 0))],
        out_shape=out_shape,
        scratch_shapes=[pltpu.VMEM((1, LANES), F32)],
        compiler_params=_params(("arbitrary", "arbitrary")),
        name="inproj",
    )(x, g, w_main, w_vt, w_logit, b_logit, kind, cosm, sina, sinb)
    kmean = kmean[:, :, :tm // MOBA_BLOCK].reshape(b, s // MOBA_BLOCK, MOBA_WIDTH)
    return (*acts, kmean)


def _softmax(heads, scores, m_ref):
    stats = []
    for hh, s in zip(heads, scores):
        m_prev = m_ref[hh]
        m_new = jnp.maximum(m_prev, jnp.max(s, axis=0, keepdims=True))
        m_ref[hh] = m_new
        stats.append((jnp.exp2(m_prev - m_new), m_new))
    return [(alpha, jnp.exp2(s - m_new).astype(BF16)) for (alpha, m_new), s in zip(stats, scores)]


def _values(weighted, vt, l_ref, acc_ref):
    ones = jnp.ones((BF16_SUBLANES, vt.shape[1]), BF16)
    for hh, (alpha, p) in enumerate(weighted):
        rows = slice(hh * HEAD_DIM, (hh + 1) * HEAD_DIM)
        pv = _dot(jnp.concatenate([vt[rows, :], ones], axis=0), p)
        acc_ref[rows, :] = alpha * acc_ref[rows, :] + pv[:HEAD_DIM]
        l_ref[hh] = alpha * l_ref[hh] + pv[HEAD_DIM:HEAD_DIM + 1]


def _attend_tiles(i, scores_of, vt_ref, m_ref, l_ref, acc_ref):
    heads = tuple(range(HEADS_PER_STEP))

    def block(tiles):
        scores = [scores_of(kt, diag, heads) for kt, diag in tiles]
        for (kt, _), s in zip(tiles, scores):
            _values(_softmax(heads, s, m_ref), vt_ref[0, kt], l_ref, acc_ref)

    @pl.when(i % 2 == 1)
    def _():
        block([(i, True), (i - 1, False)])

    @pl.when(i % 2 == 0)
    def _():
        block([(i, True)])

    def trip(j, c):
        block([(2 * j, False), (2 * j + 1, False)])
        return c

    lax.fori_loop(0, i // 2, trip, 0)


def _attn_init(m_ref, l_ref, acc_ref):
    m_ref[...] = jnp.full(m_ref.shape, NEG_INF, F32)
    l_ref[...] = jnp.zeros_like(l_ref)
    acc_ref[...] = jnp.zeros_like(acc_ref)


def _attn_finish(o_ref, l_ref, acc_ref):
    out_t = jnp.concatenate(
        [acc_ref[hh * HEAD_DIM:(hh + 1) * HEAD_DIM, :] / l_ref[hh] for hh in range(HEADS_PER_STEP)],
        axis=0)
    o_ref[0] = out_t.T


def _block(a, g):
    return a[:, g * LANES:(g + 1) * LANES]


def _per_head(a, width):
    first = lax.broadcasted_iota(I32, (a.shape[0], LANES), 1) < width
    zero = jnp.zeros((a.shape[0], LANES), a.dtype)
    out = []
    for g in range(BLOCKS_PER_STEP):
        blk = _block(a, g)
        out += [jnp.where(first, blk, zero), jnp.where(first, zero, blk)]
    return out


def _key_le_query(tq):
    return lax.broadcasted_iota(I32, (tq, tq), 0) <= lax.broadcasted_iota(I32, (tq, tq), 1)


def _fox_kernel(q_ref, qa_ref, k_ref, ka_ref, vt_ref, o_ref, m_ref, l_ref, acc_ref):
    i = pl.program_id(2)
    tq = q_ref.shape[1]
    qa = qa_ref[0]
    lane = lax.broadcasted_iota(I32, qa.shape, 1) - pl.program_id(1) * (HEADS_PER_STEP * AUX_PER_HEAD)
    own_aux = lambda hh: jnp.logical_and(lane >= hh * AUX_PER_HEAD, lane < (hh + 1) * AUX_PER_HEAD)
    qq = [jnp.concatenate([qm, jnp.where(own_aux(hh), qa, jnp.zeros_like(qa))], axis=1)
          for hh, qm in enumerate(_per_head(q_ref[0], HEAD_DIM))]
    causal = _key_le_query(tq)
    _attn_init(m_ref, l_ref, acc_ref)

    def scores_of(kt, diag, heads):
        ks = pl.multiple_of(kt * tq, tq)
        ka = ka_ref[0, pl.ds(ks, tq), :]
        kk = {g: jnp.concatenate([k_ref[0, pl.ds(ks, tq), g * LANES:(g + 1) * LANES], ka], axis=1)
              for g in sorted({hh // HEADS_PER_BLOCK for hh in heads})}
        scores = [_dot_nt(kk[hh // HEADS_PER_BLOCK], qq[hh]) for hh in heads]
        if diag:
            scores = [jnp.where(causal, s, NEG_INF) for s in scores]
        return tuple(scores)

    _attend_tiles(i, scores_of, vt_ref, m_ref, l_ref, acc_ref)
    _attn_finish(o_ref, l_ref, acc_ref)


def _moba_kernel(q_ref, k_ref, hot_ref, vt_ref, kmean_ref, o_ref, m_ref, l_ref, acc_ref):
    i = pl.program_id(2)
    tq = q_ref.shape[1]
    nblk = kmean_ref.shape[1]
    qs = _per_head(q_ref[0], HEAD_DIM)
    causal = _key_le_query(tq)
    _attn_init(m_ref, l_ref, acc_ref)

    km_parts = _split3(kmean_ref[0])
    blk = lax.broadcasted_iota(I32, (nblk, tq), 0).astype(F32)
    past = blk < i.astype(F32)
    masks = []
    for hh in range(HEADS_PER_STEP):
        gate = sum(_dot_nt(_block(part, hh // HEADS_PER_BLOCK), qs[hh]) for part in km_parts)
        sel = jnp.zeros((nblk, tq), jnp.bool_)
        for _ in range(MOBA_TOPK):
            remaining = jnp.logical_and(past, jnp.logical_not(sel))
            g = jnp.where(remaining, gate, NEG_INF)
            first = jnp.min(jnp.where(g == jnp.max(g, axis=0, keepdims=True), blk, float(nblk)),
                            axis=0, keepdims=True)
            sel = jnp.logical_or(sel, jnp.logical_and(blk == first, remaining))
        masks.append(jnp.where(sel, 0.0, MASKED))
    unused = LANES - HEADS_PER_STEP * nblk
    masks += [jnp.zeros((unused, tq), F32)] if unused else []
    qa = jnp.concatenate(masks, axis=0).T.astype(BF16)
    lane = lax.broadcasted_iota(I32, qa.shape, 1)
    own = lambda hh: jnp.logical_and(lane >= hh * nblk, lane < (hh + 1) * nblk)
    qq = [jnp.concatenate([qs[hh], jnp.where(own(hh), qa, jnp.zeros_like(qa))], axis=1)
          for hh in range(HEADS_PER_STEP)]

    def scores_of(kt, diag, heads):
        ks = pl.multiple_of(kt * tq, tq)
        k = lambda hh: k_ref[0, pl.ds(ks, tq), (hh // HEADS_PER_BLOCK) * LANES:(hh // HEADS_PER_BLOCK + 1) * LANES]
        if diag:
            return tuple(jnp.where(causal, _dot_nt(k(hh), qs[hh]), NEG_INF) for hh in heads)
        hot = hot_ref[pl.ds(ks, tq), :]
        return tuple(_dot_nt(jnp.concatenate([k(hh), hot], axis=1), qq[hh]) for hh in heads)

    _attend_tiles(i, scores_of, vt_ref, m_ref, l_ref, acc_ref)
    _attn_finish(o_ref, l_ref, acc_ref)


def _attn_scratch(tq):
    return [pltpu.VMEM((HEADS_PER_STEP, 1, tq), F32), pltpu.VMEM((HEADS_PER_STEP, 1, tq), F32),
            pltpu.VMEM((STEP_LANES, tq), F32)]


def _attn_specs(s, tq):
    q_spec = pl.BlockSpec((1, tq, STEP_LANES), lambda bi, hb, i: (bi, i, hb))
    k_spec = pl.BlockSpec((1, s, STEP_LANES), lambda bi, hb, i: (bi, 0, hb))
    vt_spec = pl.BlockSpec((1, s // tq, STEP_LANES, tq), lambda bi, hb, i: (bi, 0, hb, 0))
    return q_spec, k_spec, vt_spec


def _fox(q, qa, k, ka, vt):
    b, s, width = q.shape
    tq = TQ
    q_spec, k_spec, vt_spec = _attn_specs(s, tq)
    return pl.pallas_call(
        _fox_kernel,
        grid=(b, width // STEP_LANES, s // tq),
        in_specs=[q_spec, pl.BlockSpec((1, tq, LANES), lambda bi, hb, i: (bi, i, 0)),
                  k_spec, pl.BlockSpec((1, s, LANES), lambda bi, hb, i: (bi, 0, 0)), vt_spec],
        out_specs=q_spec,
        out_shape=jax.ShapeDtypeStruct((b, s, width), F32),
        scratch_shapes=_attn_scratch(tq),
        compiler_params=_params(("arbitrary", "arbitrary", "arbitrary")),
        name="fox",
    )(q, qa, k, ka, vt)


def _moba(q, k, vt, kmean):
    b, s, width = q.shape
    tq = TQ
    nblk = kmean.shape[1]
    assert HEADS_PER_STEP * nblk <= LANES, "one aux lane per (head, key block)"
    lane = jnp.arange(LANES)
    hot = jnp.logical_and(lane[None, :] < HEADS_PER_STEP * nblk,
                          lane[None, :] % nblk == jnp.arange(s)[:, None] // MOBA_BLOCK).astype(BF16)
    q_spec, k_spec, vt_spec = _attn_specs(s, tq)
    return pl.pallas_call(
        _moba_kernel,
        grid=(b, width // STEP_LANES, s // tq),
        in_specs=[q_spec, k_spec, pl.BlockSpec((s, LANES), lambda bi, hb, i: (0, 0)), vt_spec,
                  pl.BlockSpec((1, nblk, STEP_LANES), lambda bi, hb, i: (bi, 0, hb))],
        out_specs=q_spec,
        out_shape=jax.ShapeDtypeStruct((b, s, width), F32),
        scratch_shapes=_attn_scratch(tq),
        compiler_params=_params(("arbitrary", "arbitrary", "arbitrary")),
        name="moba",
    )(q, k, hot, vt, kmean)


def _postattn_kernel(x_ref, fox_ref, moba_ref, gf_ref, gm_ref, wo_ref, gn_ref, wr_ref, br_ref,
                     x2_ref, h2_ref, wts_ref, route_ref, cnt_ref, carry_ref):
    t = pl.program_id(0)
    tm = x_ref.shape[0]
    fw = fox_ref.shape[1]
    mixed_f = _rms(fox_ref[...], gf_ref[...]).astype(BF16)
    mixed_m = _rms(moba_ref[...], gm_ref[...]).astype(BF16)
    x2 = x_ref[...] + _dot(mixed_f, wo_ref[:fw, :]) + _dot(mixed_m, wo_ref[fw:, :])
    x2_ref[...] = x2
    h2 = _rms(x2, gn_ref[...])
    _store_rows(h2_ref, h2)

    h_hi = h2.astype(BF16)
    h_lo = (h2 - h_hi.astype(F32)).astype(BF16)
    by_hi = _dot(h_hi, wr_ref[...])
    logits = by_hi[:, :LANES] + by_hi[:, LANES:] + _dot(h_lo, wr_ref[:, :LANES]) + br_ref[...]
    lane = lax.broadcasted_iota(I32, (tm, LANES), 1).astype(F32)

    def first_max(vals):
        mx = jnp.max(vals, axis=1, keepdims=True)
        return mx, jnp.min(jnp.where(vals == mx, lane, float(LANES)), axis=1, keepdims=True)

    gl = jnp.where(lane < N_GROUPS, logits, NEG_INF)
    gmax, g_idx = first_max(gl)
    g_top = 1.0 / jnp.sum(jnp.exp(gl - gmax), axis=1, keepdims=True)
    e_lo = N_GROUPS + EXPERTS_PER_GROUP * g_idx
    el = jnp.where(jnp.logical_and(lane >= e_lo, lane < e_lo + EXPERTS_PER_GROUP), logits, NEG_INF)
    emax, i1 = first_max(el)
    esum = jnp.sum(jnp.exp(el - emax), axis=1, keepdims=True)
    e2max, i2 = first_max(jnp.where(lane == i1, NEG_INF, el))
    p1 = 1.0 / esum
    p2 = jnp.exp(e2max - emax) / esum
    w1 = p1 / (p1 + p2) * g_top
    w2 = p2 / (p1 + p2) * g_top
    e1 = i1 - N_GROUPS
    e2 = i2 - N_GROUPS
    wts_ref[...] = jnp.where(lane == 0.0, w1, w2)[:, :2]

    @pl.when(t == 0)
    def _():
        carry_ref[...] = jnp.zeros_like(carry_ref)

    row = lax.broadcasted_iota(I32, (tm, tm), 0)
    col = lax.broadcasted_iota(I32, (tm, tm), 1)
    strict = jnp.where(row > col, 1.0, 0.0).astype(BF16)
    hit1 = lane == e1
    hit2 = lane == e2
    oh1 = jnp.where(hit1, 1.0, 0.0)
    oh2 = jnp.where(hit2, 1.0, 0.0)
    tot1 = jnp.sum(oh1, axis=0, keepdims=True)
    tot2 = jnp.sum(oh2, axis=0, keepdims=True)
    base = carry_ref[...]
    before = _dot(strict, jnp.concatenate([oh1, oh2], axis=1).astype(BF16))
    before1 = before[:, :LANES] + base
    before2 = before[:, LANES:] + (base + tot1)
    r1 = jnp.sum(jnp.where(hit1, before1, 0.0), axis=1, keepdims=True)
    r2 = jnp.sum(jnp.where(hit2, before2, 0.0), axis=1, keepdims=True)
    record = jnp.where(lane == 0.0, e1, jnp.where(lane == 1.0, e2, jnp.where(lane == 2.0, r1, r2)))
    route_ref[...] = record.T[:SUBLANES, :].astype(I32)
    total = base + tot1 + tot2
    carry_ref[...] = total
    cnt_ref[...] = total.astype(I32)


def _postattn(x, fox, moba, gf, gm, wo, gn, wr, br):
    t, d = x.shape
    tm = TM_PROJ
    fw = fox.shape[1]
    const = lambda i: (0, 0)
    rows = lambda i: (i, 0)
    return pl.pallas_call(
        _postattn_kernel,
        grid=(t // tm,),
        in_specs=[pl.BlockSpec((tm, d), rows), pl.BlockSpec((tm, fw), rows),
                  pl.BlockSpec((tm, moba.shape[1]), rows),
                  pl.BlockSpec((1, fw), const), pl.BlockSpec((1, moba.shape[1]), const),
                  pl.BlockSpec(wo.shape, const), pl.BlockSpec((1, d), const),
                  pl.BlockSpec(wr.shape, const), pl.BlockSpec((1, LANES), const)],
        out_specs=[pl.BlockSpec((tm, d), rows), pl.BlockSpec((tm * SUBLANES, LANES), rows),
                   pl.BlockSpec((tm, 2), rows), pl.BlockSpec((SUBLANES, tm), lambda i: (0, i)),
                   pl.BlockSpec((1, LANES), const)],
        out_shape=[jax.ShapeDtypeStruct((t, d), F32), jax.ShapeDtypeStruct((t * SUBLANES, LANES), F32),
                   jax.ShapeDtypeStruct((t, 2), F32), jax.ShapeDtypeStruct((SUBLANES, t), I32),
                   jax.ShapeDtypeStruct((1, LANES), I32)],
        scratch_shapes=[pltpu.VMEM((1, LANES), F32)],
        compiler_params=_params(("arbitrary",)),
        name="postattn",
    )(x, fox, moba, gf, gm, wo, gn, wr, br)


def _store_rows(ref, val):
    for g in range(SUBLANES):
        ref[pl.ds(g, val.shape[0], stride=SUBLANES), :] = val[:, g * LANES:(g + 1) * LANES]


def _load_rows(ref):
    tokens = ref.shape[0] // SUBLANES
    return jnp.concatenate([ref[pl.ds(g, tokens, stride=SUBLANES), :] for g in range(SUBLANES)], axis=1)


def _row_copy(src, src_row, dst, dst_row, sem, tokens=1):
    window = lambda r: pl.ds(pl.multiple_of(r * SUBLANES, SUBLANES), tokens * SUBLANES)
    return pltpu.make_async_copy(src.at[window(src_row)], dst.at[window(dst_row)], sem)


RING = 3
ZERO_TOKENS = 256


def _dispatch_kernel(pad_start_ref, pad_len_ref, dest_ref, h_ref, xs_ref,
                     ring_ref, zero_ref, fetch_sems, scatter_sems, pad_sem):
    i = pl.program_id(0)
    last = pl.num_programs(0) - 1
    tm = dest_ref.shape[2] // 2
    tile_rows = tm * SUBLANES
    row_copy = _row_copy
    zero_tokens = zero_ref.shape[0] // SUBLANES

    def fetch(tile):
        start = pl.multiple_of(tile * tile_rows, tile_rows)
        slot = lax.rem(tile, RING)
        return pltpu.make_async_copy(h_ref.at[pl.ds(start, tile_rows)], ring_ref.at[slot],
                                     fetch_sems.at[slot])

    def zero_fill(e, act):
        start, n = pad_start_ref[e], pad_len_ref[e]
        whole = lax.shift_right_logical(n, zero_tokens.bit_length() - 1)
        rest = jnp.bitwise_and(n, zero_tokens - 1)

        def chunk(c, carry):
            act(_row_copy(zero_ref, 0, xs_ref, start + c * zero_tokens, pad_sem, tokens=zero_tokens))
            return carry

        lax.fori_loop(0, whole, chunk, 0)
        bit = zero_tokens // 2
        while bit:
            @pl.when(jnp.bitwise_and(rest, bit) != 0)
            def _(bit=bit):
                above = jnp.bitwise_and(rest, -2 * bit)
                act(_row_copy(zero_ref, 0, xs_ref, start + whole * zero_tokens + above, pad_sem, tokens=bit))
            bit //= 2

    @pl.when(i == 0)
    def _():
        fetch(0).start()
        zero_ref[...] = jnp.zeros_like(zero_ref)
        n_regions = pad_start_ref.shape[0]
        lax.fori_loop(0, n_regions, lambda e, c: (zero_fill(e, lambda cp: cp.start()), c)[1], 0)
        lax.fori_loop(0, n_regions, lambda e, c: (zero_fill(e, lambda cp: cp.wait()), c)[1], 0)

    @pl.when(i < last)
    def _():
        fetch(i + 1).start()

    fetch(i).wait()
    src = ring_ref.at[lax.rem(i, RING)]

    def issue(r, c):
        for k in range(2):
            row_copy(src, r, xs_ref, dest_ref[0, 0, k * tm + r], scatter_sems.at[i % 2]).start(priority=k)
        return c

    lax.fori_loop(0, tm, issue, 0, unroll=ISSUE_UNROLL)

    def drain(parity):
        def one(r, c):
            row_copy(src, 0, xs_ref, 0, scatter_sems.at[parity]).wait()
            return c

        lax.fori_loop(0, 2 * tm, one, 0, unroll=ISSUE_UNROLL)

    @pl.when(i > 0)
    def _():
        drain((i - 1) % 2)

    @pl.when(i == last)
    def _():
        drain(i % 2)


def _dispatch(pad_start, pad_len, dest3, h2, n_rows):
    tm = dest3.shape[2] // 2
    grid_spec = pltpu.PrefetchScalarGridSpec(
        num_scalar_prefetch=2,
        grid=(dest3.shape[0],),
        in_specs=[pl.BlockSpec((1, 1, 2 * tm), lambda i, ps, pn: (i, 0, 0), memory_space=pltpu.SMEM),
                  pl.BlockSpec(memory_space=pl.ANY)],
        out_specs=pl.BlockSpec(memory_space=pl.ANY),
        scratch_shapes=[pltpu.VMEM((RING, tm * SUBLANES, LANES), F32),
                        pltpu.VMEM((ZERO_TOKENS * SUBLANES, LANES), F32),
                        pltpu.SemaphoreType.DMA((RING,)), pltpu.SemaphoreType.DMA((2,)),
                        pltpu.SemaphoreType.DMA(())],
    )
    return pl.pallas_call(
        _dispatch_kernel,
        grid_spec=grid_spec,
        out_shape=jax.ShapeDtypeStruct((n_rows * SUBLANES, LANES), F32),
        compiler_params=_params(("arbitrary",)),
        name="dispatch",
    )(pad_start, pad_len, dest3, h2)


def _experts_kernel(te_ref, ts_ref, nv_ref, xs_ref, wg_ref, wu_ref, wd_ref, ys_ref,
                    wgb_ref, wub_ref, wdb_ref):
    del ts_ref
    t = pl.program_id(0)

    @pl.when(jnp.logical_or(t == 0, te_ref[t] != te_ref[jnp.maximum(t - 1, 0)]))
    def _():
        wgb_ref[...] = wg_ref[0].astype(BF16)
        wub_ref[...] = wu_ref[0].astype(BF16)
        wdb_ref[...] = wd_ref[0].astype(BF16)

    @pl.when(t < nv_ref[0])
    def _():
        xb = _load_rows(xs_ref).astype(BF16)
        a = _dot(xb, wgb_ref[...])
        u = _dot(xb, wub_ref[...])
        act = (a * jax.nn.sigmoid(a) * u).astype(BF16)
        _store_rows(ys_ref, _dot(act, wdb_ref[...]))

    @pl.when(t >= nv_ref[0])
    def _():
        ys_ref[...] = jnp.zeros_like(ys_ref)


def _experts(tile_expert, tile_src, n_valid, xs, wg, wu, wd):
    tm = TM_EXPERT
    n_tiles = xs.shape[0] // (tm * SUBLANES)
    _, d, f = wg.shape
    row_block = (tm * SUBLANES, LANES)
    grid_spec = pltpu.PrefetchScalarGridSpec(
        num_scalar_prefetch=3,
        grid=(n_tiles,),
        in_specs=[pl.BlockSpec(row_block, lambda t, te, ts, nv: (ts[t], 0)),
                  pl.BlockSpec((1, d, f), lambda t, te, ts, nv: (te[t], 0, 0)),
                  pl.BlockSpec((1, d, f), lambda t, te, ts, nv: (te[t], 0, 0)),
                  pl.BlockSpec((1, f, d), lambda t, te, ts, nv: (te[t], 0, 0))],
        out_specs=pl.BlockSpec(row_block, lambda t, te, ts, nv: (t, 0)),
        scratch_shapes=[pltpu.VMEM((d, f), BF16), pltpu.VMEM((d, f), BF16), pltpu.VMEM((f, d), BF16)],
    )
    return pl.pallas_call(
        _experts_kernel,
        grid_spec=grid_spec,
        out_shape=jax.ShapeDtypeStruct((n_tiles * tm * SUBLANES, LANES), F32),
        compiler_params=_params(("arbitrary",)),
        name="experts",
    )(tile_expert, tile_src, n_valid, xs, wg, wu, wd)


def _combine_kernel(dest_ref, next_ref, x2_ref, wts_ref, g_ref, ys_ref, o_ref, buf_ref, sems):
    i = pl.program_id(0)
    tm = dest_ref.shape[2] // 2
    slot = i % 2

    def gather(d_ref, to):
        def issue(r, c):
            for k in range(2):
                _row_copy(ys_ref, d_ref[0, 0, k * tm + r], buf_ref.at[to, k], r,
                          sems.at[to]).start(priority=k)
            return c

        lax.fori_loop(0, tm, issue, 0, unroll=ISSUE_UNROLL)

    @pl.when(i == 0)
    def _():
        gather(dest_ref, 0)

    @pl.when(i + 1 < pl.num_programs(0))
    def _():
        gather(next_ref, 1 - slot)

    def drain(r, c):
        _row_copy(ys_ref, 0, buf_ref.at[slot, 0], 0, sems.at[slot]).wait()
        return c

    lax.fori_loop(0, 2 * tm, drain, 0, unroll=ISSUE_UNROLL)
    w = wts_ref[...]
    y = (x2_ref[...] + w[:, 0:1] * _load_rows(buf_ref.at[slot, 0])
         + w[:, 1:2] * _load_rows(buf_ref.at[slot, 1]))
    o_ref[...] = _rms(y, g_ref[...])


def _combine(dest3, x2, wts, g, ys):
    t, d = x2.shape
    tm = dest3.shape[2] // 2
    rows = lambda i: (i, 0)
    n = t // tm
    return pl.pallas_call(
        _combine_kernel,
        grid=(n,),
        in_specs=[pl.BlockSpec((1, 1, 2 * tm), lambda i: (i, 0, 0), memory_space=pltpu.SMEM),
                  pl.BlockSpec((1, 1, 2 * tm), lambda i: (jnp.minimum(i + 1, n - 1), 0, 0),
                               memory_space=pltpu.SMEM),
                  pl.BlockSpec((tm, d), rows), pl.BlockSpec((tm, 2), rows),
                  pl.BlockSpec((1, d), lambda i: (0, 0)),
                  pl.BlockSpec(memory_space=pl.ANY)],
        out_specs=pl.BlockSpec((tm, d), rows),
        out_shape=jax.ShapeDtypeStruct((t, d), F32),
        scratch_shapes=[pltpu.VMEM((2, 2, tm * SUBLANES, LANES), F32), pltpu.SemaphoreType.DMA((2,))],
        compiler_params=_params(("arbitrary",)),
        name="combine",
    )(dest3, dest3, x2, wts, g, ys)


def _rotary_tables(seq):
    half = ROPE_DIM // 2
    inv_freq = ROPE_THETA ** (-jnp.arange(half, dtype=F32) / half)
    ang = jnp.arange(seq, dtype=F32)[:, None] * inv_freq[None, :]
    cos, sin = jnp.cos(ang), jnp.sin(ang)
    ones = jnp.ones((seq, HEAD_DIM - ROPE_DIM), F32)
    zeros = jnp.zeros((seq, HEAD_DIM - ROPE_DIM), F32)
    zh = jnp.zeros((seq, half), F32)
    cosm = jnp.concatenate([cos, cos, ones], axis=1)
    sina = jnp.concatenate([-sin, zh, zeros], axis=1)
    sinb = jnp.concatenate([zh, sin, zeros], axis=1)
    tile = lambda a: jnp.tile(a, (1, HEADS_PER_BLOCK))
    return tile(cosm), tile(sina), tile(sinb)


def _pad_lanes(a):
    return jnp.pad(a, ((0, 0), (0, LANES - a.shape[1])))


def kernel(x, norm_mix_g, w_in, b_forget, fox_out_g, moba_out_g, w_out, norm_ffn_g, w_router_group,
           b_router_group, w_router_expert, b_router_expert, w_gate, w_up, w_down, norm_final_g):
    b, s, d = x.shape
    t = b * s
    assert w_in.shape[0] == 1, "the closing RMSNorm is fused into the only layer's combine step"
    cosm, sina, sinb = _rotary_tables(s)
    n_tiles = (2 * t) // TM_EXPERT + N_EXPERTS
    fw3 = 3 * FOX_WIDTH
    m0 = fw3 + N_FOX_HEADS
    wl = w_in[0]
    w_main = jnp.concatenate([wl[:, :2 * FOX_WIDTH], wl[:, m0:m0 + 2 * MOBA_WIDTH]],
                             axis=1).astype(BF16)
    w_vt = jnp.stack([wl[:, 2 * FOX_WIDTH:fw3].T, wl[:, m0 + 2 * MOBA_WIDTH:].T]).astype(BF16)
    aux_head = jnp.arange(AUX_PER_HEAD * N_FOX_HEADS) // AUX_PER_HEAD
    w_logit = _pad_lanes(wl[:, fw3:m0][:, aux_head]).astype(BF16)
    b_logit = _pad_lanes(b_forget[0][None, aux_head])
    fq, fqa, fk, fka, fvt, mq, mk, mvt, kmean = _inproj(
        x, norm_mix_g[0][None, :], w_main, w_vt, w_logit, b_logit, cosm, sina, sinb)
    fox = _fox(fq, fqa, fk, fka, fvt)
    moba = _moba(mq, mk, mvt, kmean)

    w_router = _pad_lanes(jnp.concatenate(
        [w_router_group[0], w_router_expert[0].reshape(d, N_EXPERTS)], axis=1))
    w_router_hi = w_router.astype(BF16)
    w_router = jnp.concatenate([w_router_hi, (w_router - w_router_hi.astype(F32)).astype(BF16)], axis=1)
    b_router =_pad_lanes(jnp.concatenate(
        [b_router_group[0], b_router_expert[0].reshape(N_EXPERTS)])[None, :])
    x2, h2, wts, route, counts = _postattn(
        x.reshape(t, d), fox.reshape(t, FOX_WIDTH), moba.reshape(t, MOBA_WIDTH),
        fox_out_g[0][None, :], moba_out_g[0][None, :], w_out[0].astype(BF16),
        norm_ffn_g[0][None, :], w_router, b_router)

    counts = counts[0, :N_EXPERTS]
    padded = (counts + TM_EXPERT - 1) // TM_EXPERT * TM_EXPERT
    ends = jnp.cumsum(padded)
    starts = ends - padded
    expert_ids = jnp.arange(N_EXPERTS, dtype=I32)[:, None, None]
    dest = route[2:4] + jnp.sum(jnp.where(route[None, 0:2] == expert_ids, starts[:, None, None], 0),
                                axis=0)
    dest3 = dest.reshape(2, t // TM_ROWS, TM_ROWS).transpose(1, 0, 2).reshape(t // TM_ROWS, 1, 2 * TM_ROWS)
    n_rows = n_tiles * TM_EXPERT
    pad_start = jnp.concatenate([starts + counts, ends[-1:]]).astype(I32)
    pad_len = jnp.concatenate([padded - counts, n_rows - ends[-1:]]).astype(I32)
    n_valid = ends[-1] // TM_EXPERT
    tile_src = jnp.minimum(jnp.arange(n_tiles, dtype=I32), n_valid - 1)
    tile_expert = jnp.sum(ends[None, :] <= (tile_src * TM_EXPERT)[:, None], axis=1).astype(I32)
    xs = _dispatch(pad_start, pad_len, dest3, h2, n_rows)
    ys = _experts(tile_expert, tile_src, n_valid.reshape(1).astype(I32), xs,
                  w_gate[0], w_up[0], w_down[0])
    return _combine(dest3, x2, wts, norm_final_g[None, :], ys).reshape(b, s, d)
```

```python
import math

import jax
import jax.numpy as jnp
from jax import lax
from jax.experimental import pallas as pl
from jax.experimental.pallas import tpu as pltpu

F32 = jnp.float32
BF16 = jnp.bfloat16
I32 = jnp.int32

HEAD_DIM = 64
N_FOX_HEADS = 8
N_MOBA_HEADS = 8
FOX_WIDTH = N_FOX_HEADS * HEAD_DIM
MOBA_WIDTH = N_MOBA_HEADS * HEAD_DIM
MOBA_BLOCK = 256
MOBA_TOPK = 3
ROPE_THETA = 500000.0
ROPE_DIM = HEAD_DIM // 4
N_GROUPS = 4
EXPERTS_PER_GROUP = 8
N_EXPERTS = N_GROUPS * EXPERTS_PER_GROUP
EPS = 1e-6

LANES = 128
SUBLANES = 8
BF16_SUBLANES = 16
LOG2_E = math.log2(math.e)
HEADS_PER_BLOCK = LANES // HEAD_DIM
BLOCKS_PER_STEP = 4
HEADS_PER_STEP = HEADS_PER_BLOCK * BLOCKS_PER_STEP
STEP_LANES = LANES * BLOCKS_PER_STEP
VMEM_LIMIT = 56 * 1024 * 1024
AUX_PER_HEAD = 6

TM_PROJ = 512
TQ = 256
TM_EXPERT = 512
TM_ROWS = 256
ISSUE_UNROLL = 8

NEG_INF = float("-inf")
MASKED = -1e30


def _params(sem):
    return pltpu.CompilerParams(dimension_semantics=sem, vmem_limit_bytes=VMEM_LIMIT)


def _rms(x, g):
    return x * lax.rsqrt(jnp.mean(x * x, axis=-1, keepdims=True) + EPS) * g


def _split3(x):
    hi = x.astype(BF16)
    r = x - hi.astype(F32)
    mid = r.astype(BF16)
    lo = (r - mid.astype(F32)).astype(BF16)
    return hi, mid, lo


def _dot(a, b):
    return jnp.dot(a, b, preferred_element_type=F32)


def _dot_nt(a, b):
    return lax.dot_general(a, b, (((1,), (1,)), ((), ())), preferred_element_type=F32)


def _inproj_kernel(x_ref, g_ref, w_ref, wvt_ref, wl_ref, bf_ref, kind_ref, cosm_ref, sina_ref, sinb_ref,
                   fq_ref, fqa_ref, fk_ref, fka_ref, fvt_ref, mq_ref, mk_ref, mvt_ref, kmean_ref,
                   carry_ref):
    j = pl.program_id(1)
    tm = x_ref.shape[1]
    tk = fvt_ref.shape[3]
    h = _rms(x_ref[0], g_ref[...]).astype(BF16)
    scale = HEAD_DIM ** -0.5 * LOG2_E

    def proj(seg):
        return _dot(h, w_ref[:, seg * FOX_WIDTH:(seg + 1) * FOX_WIDTH])

    fq_ref[0] = (proj(0) * scale).astype(BF16)
    fk_ref[0] = proj(1).astype(BF16)

    for vt_ref, seg in ((fvt_ref, 0), (mvt_ref, 1)):
        vt = _dot_nt(wvt_ref[seg], h).astype(BF16)
        for r in range(tm // tk):
            vt_ref[0, r] = vt[:, r * tk:(r + 1) * tk]

    cosm, sina, sinb = cosm_ref[...], sina_ref[...], sinb_ref[...]

    def rotary(t):
        outs = []
        for g in range(MOBA_WIDTH // LANES):
            tg = t[:, g * LANES:(g + 1) * LANES]
            outs.append(tg * cosm + pltpu.roll(tg, LANES - ROPE_DIM // 2, 1) * sina
                        + pltpu.roll(tg, ROPE_DIM // 2, 1) * sinb)
        return jnp.concatenate(outs, axis=1)

    mq_ref[0] = (rotary(proj(2)) * scale).astype(BF16)
    mk = rotary(proj(3))
    mk_ref[0] = mk.astype(BF16)
    nblk_tile = tm // MOBA_BLOCK
    means = [jnp.mean(mk[r * MOBA_BLOCK:(r + 1) * MOBA_BLOCK], axis=0, keepdims=True)
             for r in range(nblk_tile)]
    means += [jnp.zeros_like(means[0])] * (kmean_ref.shape[2] - nblk_tile)
    kmean_ref[0, 0] = jnp.concatenate(means, axis=0)

    z = _dot(h, wl_ref[...]) + bf_ref[...]
    log_f = jnp.minimum(z, 0.0) - jnp.log1p(jnp.exp(-jnp.abs(z)))

    @pl.when(j == 0)
    def _():
        carry_ref[...] = jnp.zeros_like(carry_ref)

    half = tm // 2
    row = lax.broadcasted_iota(I32, (half, half), 0)
    col = lax.broadcasted_iota(I32, (half, half), 1)
    tri = jnp.where(row >= col, 1.0, 0.0).astype(BF16)
    pieces = jnp.concatenate(_split3(log_f), axis=1)
    carry = carry_ref[...]
    cs = []
    for r in range(2):
        local = _dot(tri, pieces[r * half:(r + 1) * half])
        cs.append(local[:, :LANES] + local[:, LANES:2 * LANES] + local[:, 2 * LANES:] + carry)
        carry = cs[-1][half - 1:half, :]
    carry_ref[...] = carry
    c = jnp.concatenate(cs, axis=0) * LOG2_E

    hi = c.astype(BF16).astype(F32)
    mid = (c - hi).astype(BF16).astype(F32)
    lo = c - hi - mid
    kind = kind_ref[...]
    one = jnp.where(kind < AUX_PER_HEAD, 1.0, 0.0)
    pick = lambda base: jnp.where(kind == base, hi, jnp.where(kind == base + 1, mid,
                                  jnp.where(kind == base + 2, lo, 0.0)))
    fqa_ref[0] = (pick(3) + jnp.where(kind < 3, one, 0.0)).astype(BF16)
    fka_ref[0] = (jnp.where(kind >= 3, one, 0.0) - pick(0)).astype(BF16)


def _inproj(x, g, w_main, w_vt, w_logit, b_logit, cosm, sina, sinb):
    b, s, d = x.shape
    tm, tk = TM_PROJ, TQ
    lane = jnp.arange(LANES, dtype=I32)
    kind = jnp.where(lane < AUX_PER_HEAD * N_FOX_HEADS, lane % AUX_PER_HEAD, AUX_PER_HEAD)[None, :]
    act = jax.ShapeDtypeStruct((b, s, FOX_WIDTH), BF16)
    aux = jax.ShapeDtypeStruct((b, s, LANES), BF16)
    vt = jax.ShapeDtypeStruct((b, s // tk, FOX_WIDTH, tk), BF16)
    out_shape = [act, aux, act, aux, vt, act, act, vt,
                 jax.ShapeDtypeStruct((b, s // tm, SUBLANES, MOBA_WIDTH), F32)]
    act_spec = pl.BlockSpec((1, tm, FOX_WIDTH), lambda bi, j: (bi, j, 0))
    aux_spec = pl.BlockSpec((1, tm, LANES), lambda bi, j: (bi, j, 0))
    vt_spec = pl.BlockSpec((1, tm // tk, FOX_WIDTH, tk), lambda bi, j: (bi, j, 0, 0))
    tab_spec = pl.BlockSpec((tm, LANES), lambda bi, j: (j, 0))
    const2 = lambda bi, j: (0, 0)
    const3 = lambda bi, j: (0, 0, 0)
    *acts, kmean = pl.pallas_call(
        _inproj_kernel,
        grid=(b, s // tm),
        in_specs=[pl.BlockSpec((1, tm, d), lambda bi, j: (bi, j, 0)),
                  pl.BlockSpec((1, d), const2),
                  pl.BlockSpec(w_main.shape, const2),
                  pl.BlockSpec(w_vt.shape, const3),
                  pl.BlockSpec(w_logit.shape, const2),
                  pl.BlockSpec((1, LANES), const2), pl.BlockSpec((1, LANES), const2),
                  tab_spec, tab_spec, tab_spec],
        out_specs=[act_spec, aux_spec, act_spec, aux_spec, vt_spec, act_spec, act_spec, vt_spec,
                   pl.BlockSpec((1, 1, SUBLANES, MOBA_WIDTH), lambda bi, j: (bi, j, 0, 0))],
        out_shape=out_shape,
        scratch_shapes=[pltpu.VMEM((1, LANES), F32)],
        compiler_params=_params(("arbitrary", "arbitrary")),
        name="inproj",
    )(x, g, w_main, w_vt, w_logit, b_logit, kind, cosm, sina, sinb)
    kmean = kmean[:, :, :tm // MOBA_BLOCK].reshape(b, s // MOBA_BLOCK, MOBA_WIDTH)
    return (*acts, kmean)


def _softmax(heads, scores, m_ref):
    stats = []
    for hh, s in zip(heads, scores):
        m_prev = m_ref[hh]
        m_new = jnp.maximum(m_prev, jnp.max(s, axis=0, keepdims=True))
        m_ref[hh] = m_new
        stats.append((jnp.exp2(m_prev - m_new), m_new))
    return [(alpha, jnp.exp2(s - m_new).astype(BF16)) for (alpha, m_new), s in zip(stats, scores)]


def _values(weighted, vt, l_ref, acc_ref):
    ones = jnp.ones((BF16_SUBLANES, vt.shape[1]), BF16)
    for hh, (alpha, p) in enumerate(weighted):
        rows = slice(hh * HEAD_DIM, (hh + 1) * HEAD_DIM)
        pv = _dot(jnp.concatenate([vt[rows, :], ones], axis=0), p)
        acc_ref[rows, :] = alpha * acc_ref[rows, :] + pv[:HEAD_DIM]
        l_ref[hh] = alpha * l_ref[hh] + pv[HEAD_DIM:HEAD_DIM + 1]


def _attend_tiles(i, scores_of, vt_ref, m_ref, l_ref, acc_ref):
    heads = tuple(range(HEADS_PER_STEP))

    def block(tiles):
        scores = [scores_of(kt, diag, heads) for kt, diag in tiles]
        for (kt, _), s in zip(tiles, scores):
            _values(_softmax(heads, s, m_ref), vt_ref[0, kt], l_ref, acc_ref)

    @pl.when(i % 2 == 1)
    def _():
        block([(i, True), (i - 1, False)])

    @pl.when(i % 2 == 0)
    def _():
        block([(i, True)])

    def trip(j, c):
        block([(2 * j, False), (2 * j + 1, False)])
        return c

    lax.fori_loop(0, i // 2, trip, 0)


def _attn_init(m_ref, l_ref, acc_ref):
    m_ref[...] = jnp.full(m_ref.shape, NEG_INF, F32)
    l_ref[...] = jnp.zeros_like(l_ref)
    acc_ref[...] = jnp.zeros_like(acc_ref)


def _attn_finish(o_ref, l_ref, acc_ref):
    out_t = jnp.concatenate(
        [acc_ref[hh * HEAD_DIM:(hh + 1) * HEAD_DIM, :] / l_ref[hh] for hh in range(HEADS_PER_STEP)],
        axis=0)
    o_ref[0] = out_t.T


def _block(a, g):
    return a[:, g * LANES:(g + 1) * LANES]


def _per_head(a, width):
    first = lax.broadcasted_iota(I32, (a.shape[0], LANES), 1) < width
    zero = jnp.zeros((a.shape[0], LANES), a.dtype)
    out = []
    for g in range(BLOCKS_PER_STEP):
        blk = _block(a, g)
        out += [jnp.where(first, blk, zero), jnp.where(first, zero, blk)]
    return out


def _key_le_query(tq):
    return lax.broadcasted_iota(I32, (tq, tq), 0) <= lax.broadcasted_iota(I32, (tq, tq), 1)


def _fox_kernel(q_ref, qa_ref, k_ref, ka_ref, vt_ref, o_ref, m_ref, l_ref, acc_ref):
    i = pl.program_id(2)
    tq = q_ref.shape[1]
    qa = qa_ref[0]
    lane = lax.broadcasted_iota(I32, qa.shape, 1) - pl.program_id(1) * (HEADS_PER_STEP * AUX_PER_HEAD)
    own_aux = lambda hh: jnp.logical_and(lane >= hh * AUX_PER_HEAD, lane < (hh + 1) * AUX_PER_HEAD)
    qq = [jnp.concatenate([qm, jnp.where(own_aux(hh), qa, jnp.zeros_like(qa))], axis=1)
          for hh, qm in enumerate(_per_head(q_ref[0], HEAD_DIM))]
    causal = _key_le_query(tq)
    _attn_init(m_ref, l_ref, acc_ref)

    def scores_of(kt, diag, heads):
        ks = pl.multiple_of(kt * tq, tq)
        ka = ka_ref[0, pl.ds(ks, tq), :]
        kk = {g: jnp.concatenate([k_ref[0, pl.ds(ks, tq), g * LANES:(g + 1) * LANES], ka], axis=1)
              for g in sorted({hh // HEADS_PER_BLOCK for hh in heads})}
        scores = [_dot_nt(kk[hh // HEADS_PER_BLOCK], qq[hh]) for hh in heads]
        if diag:
            scores = [jnp.where(causal, s, NEG_INF) for s in scores]
        return tuple(scores)

    _attend_tiles(i, scores_of, vt_ref, m_ref, l_ref, acc_ref)
    _attn_finish(o_ref, l_ref, acc_ref)


def _moba_kernel(q_ref, k_ref, hot_ref, vt_ref, kmean_ref, o_ref, m_ref, l_ref, acc_ref):
    i = pl.program_id(2)
    tq = q_ref.shape[1]
    nblk = kmean_ref.shape[1]
    qs = _per_head(q_ref[0], HEAD_DIM)
    causal = _key_le_query(tq)
    _attn_init(m_ref, l_ref, acc_ref)

    km_parts = _split3(kmean_ref[0])
    blk = lax.broadcasted_iota(I32, (nblk, tq), 0).astype(F32)
    past = blk < i.astype(F32)
    masks = []
    for hh in range(HEADS_PER_STEP):
        gate = sum(_dot_nt(_block(part, hh // HEADS_PER_BLOCK), qs[hh]) for part in km_parts)
        sel = jnp.zeros((nblk, tq), jnp.bool_)
        for _ in range(MOBA_TOPK):
            remaining = jnp.logical_and(past, jnp.logical_not(sel))
            g = jnp.where(remaining, gate, NEG_INF)
            first = jnp.min(jnp.where(g == jnp.max(g, axis=0, keepdims=True), blk, float(nblk)),
                            axis=0, keepdims=True)
            sel = jnp.logical_or(sel, jnp.logical_and(blk == first, remaining))
        masks.append(jnp.where(sel, 0.0, MASKED))
    unused = LANES - HEADS_PER_STEP * nblk
    masks += [jnp.zeros((unused, tq), F32)] if unused else []
    qa = jnp.concatenate(masks, axis=0).T.astype(BF16)
    lane = lax.broadcasted_iota(I32, qa.shape, 1)
    own = lambda hh: jnp.logical_and(lane >= hh * nblk, lane < (hh + 1) * nblk)
    qq = [jnp.concatenate([qs[hh], jnp.where(own(hh), qa, jnp.zeros_like(qa))], axis=1)
          for hh in range(HEADS_PER_STEP)]

    def scores_of(kt, diag, heads):
        ks = pl.multiple_of(kt * tq, tq)
        k = lambda hh: k_ref[0, pl.ds(ks, tq), (hh // HEADS_PER_BLOCK) * LANES:(hh // HEADS_PER_BLOCK + 1) * LANES]
        if diag:
            return tuple(jnp.where(causal, _dot_nt(k(hh), qs[hh]), NEG_INF) for hh in heads)
        hot = hot_ref[pl.ds(ks, tq), :]
        return tuple(_dot_nt(jnp.concatenate([k(hh), hot], axis=1), qq[hh]) for hh in heads)

    _attend_tiles(i, scores_of, vt_ref, m_ref, l_ref, acc_ref)
    _attn_finish(o_ref, l_ref, acc_ref)


def _attn_scratch(tq):
    return [pltpu.VMEM((HEADS_PER_STEP, 1, tq), F32), pltpu.VMEM((HEADS_PER_STEP, 1, tq), F32),
            pltpu.VMEM((STEP_LANES, tq), F32)]


def _attn_specs(s, tq):
    q_spec = pl.BlockSpec((1, tq, STEP_LANES), lambda bi, hb, i: (bi, i, hb))
    k_spec = pl.BlockSpec((1, s, STEP_LANES), lambda bi, hb, i: (bi, 0, hb))
    vt_spec = pl.BlockSpec((1, s // tq, STEP_LANES, tq), lambda bi, hb, i: (bi, 0, hb, 0))
    return q_spec, k_spec, vt_spec


def _fox(q, qa, k, ka, vt):
    b, s, width = q.shape
    tq = TQ
    q_spec, k_spec, vt_spec = _attn_specs(s, tq)
    return pl.pallas_call(
        _fox_kernel,
        grid=(b, width // STEP_LANES, s // tq),
        in_specs=[q_spec, pl.BlockSpec((1, tq, LANES), lambda bi, hb, i: (bi, i, 0)),
                  k_spec, pl.BlockSpec((1, s, LANES), lambda bi, hb, i: (bi, 0, 0)), vt_spec],
        out_specs=q_spec,
        out_shape=jax.ShapeDtypeStruct((b, s, width), F32),
        scratch_shapes=_attn_scratch(tq),
        compiler_params=_params(("arbitrary", "arbitrary", "arbitrary")),
        name="fox",
    )(q, qa, k, ka, vt)


def _moba(q, k, vt, kmean):
    b, s, width = q.shape
    tq = TQ
    nblk = kmean.shape[1]
    assert HEADS_PER_STEP * nblk <= LANES, "one aux lane per (head, key block)"
    lane = jnp.arange(LANES)
    hot = jnp.logical_and(lane[None, :] < HEADS_PER_STEP * nblk,
                          lane[None, :] % nblk == jnp.arange(s)[:, None] // MOBA_BLOCK).astype(BF16)
    q_spec, k_spec, vt_spec = _attn_specs(s, tq)
    return pl.pallas_call(
        _moba_kernel,
        grid=(b, width // STEP_LANES, s // tq),
        in_specs=[q_spec, k_spec, pl.BlockSpec((s, LANES), lambda bi, hb, i: (0, 0)), vt_spec,
                  pl.BlockSpec((1, nblk, STEP_LANES), lambda bi, hb, i: (bi, 0, hb))],
        out_specs=q_spec,
        out_shape=jax.ShapeDtypeStruct((b, s, width), F32),
        scratch_shapes=_attn_scratch(tq),
        compiler_params=_params(("arbitrary", "arbitrary", "arbitrary")),
        name="moba",
    )(q, k, hot, vt, kmean)


def _postattn_kernel(x_ref, fox_ref, moba_ref, gf_ref, gm_ref, wo_ref, gn_ref, wr_ref, br_ref,
                     x2_ref, h2_ref, wts_ref, route_ref, cnt_ref, carry_ref):
    t = pl.program_id(0)
    tm = x_ref.shape[0]
    fw = fox_ref.shape[1]
    mixed_f = _rms(fox_ref[...], gf_ref[...]).astype(BF16)
    mixed_m = _rms(moba_ref[...], gm_ref[...]).astype(BF16)
    x2 = x_ref[...] + _dot(mixed_f, wo_ref[:fw, :]) + _dot(mixed_m, wo_ref[fw:, :])
    x2_ref[...] = x2
    h2 = _rms(x2, gn_ref[...])
    _store_rows(h2_ref, h2)

    h_hi = h2.astype(BF16)
    h_lo = (h2 - h_hi.astype(F32)).astype(BF16)
    by_hi = _dot(h_hi, wr_ref[...])
    logits = by_hi[:, :LANES] + by_hi[:, LANES:] + _dot(h_lo, wr_ref[:, :LANES]) + br_ref[...]
    lane = lax.broadcasted_iota(I32, (tm, LANES), 1).astype(F32)

    def first_max(vals):
        mx = jnp.max(vals, axis=1, keepdims=True)
        return mx, jnp.min(jnp.where(vals == mx, lane, float(LANES)), axis=1, keepdims=True)

    gl = jnp.where(lane < N_GROUPS, logits, NEG_INF)
    gmax, g_idx = first_max(gl)
    g_top = 1.0 / jnp.sum(jnp.exp(gl - gmax), axis=1, keepdims=True)
    e_lo = N_GROUPS + EXPERTS_PER_GROUP * g_idx
    el = jnp.where(jnp.logical_and(lane >= e_lo, lane < e_lo + EXPERTS_PER_GROUP), logits, NEG_INF)
    emax, i1 = first_max(el)
    esum = jnp.sum(jnp.exp(el - emax), axis=1, keepdims=True)
    e2max, i2 = first_max(jnp.where(lane == i1, NEG_INF, el))
    p1 = 1.0 / esum
    p2 = jnp.exp(e2max - emax) / esum
    w1 = p1 / (p1 + p2) * g_top
    w2 = p2 / (p1 + p2) * g_top
    e1 = i1 - N_GROUPS
    e2 = i2 - N_GROUPS
    wts_ref[...] = jnp.where(lane == 0.0, w1, w2)[:, :2]

    @pl.when(t == 0)
    def _():
        carry_ref[...] = jnp.zeros_like(carry_ref)

    row = lax.broadcasted_iota(I32, (tm, tm), 0)
    col = lax.broadcasted_iota(I32, (tm, tm), 1)
    strict = jnp.where(row > col, 1.0, 0.0).astype(BF16)
    hit1 = lane == e1
    hit2 = lane == e2
    oh1 = jnp.where(hit1, 1.0, 0.0)
    oh2 = jnp.where(hit2, 1.0, 0.0)
    tot1 = jnp.sum(oh1, axis=0, keepdims=True)
    tot2 = jnp.sum(oh2, axis=0, keepdims=True)
    base = carry_ref[...]
    before = _dot(strict, jnp.concatenate([oh1, oh2], axis=1).astype(BF16))
    before1 = before[:, :LANES] + base
    before2 = before[:, LANES:] + (base + tot1)
    r1 = jnp.sum(jnp.where(hit1, before1, 0.0), axis=1, keepdims=True)
    r2 = jnp.sum(jnp.where(hit2, before2, 0.0), axis=1, keepdims=True)
    record = jnp.where(lane == 0.0, e1, jnp.where(lane == 1.0, e2, jnp.where(lane == 2.0, r1, r2)))
    route_ref[...] = record.T[:SUBLANES, :].astype(I32)
    total = base + tot1 + tot2
    carry_ref[...] = total
    cnt_ref[...] = total.astype(I32)


def _postattn(x, fox, moba, gf, gm, wo, gn, wr, br):
    t, d = x.shape
    tm = TM_PROJ
    fw = fox.shape[1]
    const = lambda i: (0, 0)
    rows = lambda i: (i, 0)
    return pl.pallas_call(
        _postattn_kernel,
        grid=(t // tm,),
        in_specs=[pl.BlockSpec((tm, d), rows), pl.BlockSpec((tm, fw), rows),
                  pl.BlockSpec((tm, moba.shape[1]), rows),
                  pl.BlockSpec((1, fw), const), pl.BlockSpec((1, moba.shape[1]), const),
                  pl.BlockSpec(wo.shape, const), pl.BlockSpec((1, d), const),
                  pl.BlockSpec(wr.shape, const), pl.BlockSpec((1, LANES), const)],
        out_specs=[pl.BlockSpec((tm, d), rows), pl.BlockSpec((tm * SUBLANES, LANES), rows),
                   pl.BlockSpec((tm, 2), rows), pl.BlockSpec((SUBLANES, tm), lambda i: (0, i)),
                   pl.BlockSpec((1, LANES), const)],
        out_shape=[jax.ShapeDtypeStruct((t, d), F32), jax.ShapeDtypeStruct((t * SUBLANES, LANES), F32),
                   jax.ShapeDtypeStruct((t, 2), F32), jax.ShapeDtypeStruct((SUBLANES, t), I32),
                   jax.ShapeDtypeStruct((1, LANES), I32)],
        scratch_shapes=[pltpu.VMEM((1, LANES), F32)],
        compiler_params=_params(("arbitrary",)),
        name="postattn",
    )(x, fox, moba, gf, gm, wo, gn, wr, br)


def _store_rows(ref, val):
    for g in range(SUBLANES):
        ref[pl.ds(g, val.shape[0], stride=SUBLANES), :] = val[:, g * LANES:(g + 1) * LANES]


def _load_rows(ref):
    tokens = ref.shape[0] // SUBLANES
    return jnp.concatenate([ref[pl.ds(g, tokens, stride=SUBLANES), :] for g in range(SUBLANES)], axis=1)


def _row_copy(src, src_row, dst, dst_row, sem, tokens=1):
    window = lambda r: pl.ds(pl.multiple_of(r * SUBLANES, SUBLANES), tokens * SUBLANES)
    return pltpu.make_async_copy(src.at[window(src_row)], dst.at[window(dst_row)], sem)


RING = 3
ZERO_TOKENS = 256


def _dispatch_kernel(pad_start_ref, pad_len_ref, dest_ref, h_ref, xs_ref,
                     ring_ref, zero_ref, fetch_sems, scatter_sems, pad_sem):
    i = pl.program_id(0)
    last = pl.num_programs(0) - 1
    tm = dest_ref.shape[2] // 2
    tile_rows = tm * SUBLANES
    row_copy = _row_copy
    zero_tokens = zero_ref.shape[0] // SUBLANES

    def fetch(tile):
        start = pl.multiple_of(tile * tile_rows, tile_rows)
        slot = lax.rem(tile, RING)
        return pltpu.make_async_copy(h_ref.at[pl.ds(start, tile_rows)], ring_ref.at[slot],
                                     fetch_sems.at[slot])

    def zero_fill(e, act):
        start, n = pad_start_ref[e], pad_len_ref[e]
        whole = lax.shift_right_logical(n, zero_tokens.bit_length() - 1)
        rest = jnp.bitwise_and(n, zero_tokens - 1)

        def chunk(c, carry):
            act(_row_copy(zero_ref, 0, xs_ref, start + c * zero_tokens, pad_sem, tokens=zero_tokens))
            return carry

        lax.fori_loop(0, whole, chunk, 0)
        bit = zero_tokens // 2
        while bit:
            @pl.when(jnp.bitwise_and(rest, bit) != 0)
            def _(bit=bit):
                above = jnp.bitwise_and(rest, -2 * bit)
                act(_row_copy(zero_ref, 0, xs_ref, start + whole * zero_tokens + above, pad_sem, tokens=bit))
            bit //= 2

    @pl.when(i == 0)
    def _():
        fetch(0).start()
        zero_ref[...] = jnp.zeros_like(zero_ref)
        n_regions = pad_start_ref.shape[0]
        lax.fori_loop(0, n_regions, lambda e, c: (zero_fill(e, lambda cp: cp.start()), c)[1], 0)
        lax.fori_loop(0, n_regions, lambda e, c: (zero_fill(e, lambda cp: cp.wait()), c)[1], 0)

    @pl.when(i < last)
    def _():
        fetch(i + 1).start()

    fetch(i).wait()
    src = ring_ref.at[lax.rem(i, RING)]

    def issue(r, c):
        for k in range(2):
            row_copy(src, r, xs_ref, dest_ref[0, 0, k * tm + r], scatter_sems.at[i % 2]).start(priority=k)
        return c

    lax.fori_loop(0, tm, issue, 0, unroll=ISSUE_UNROLL)

    def drain(parity):
        def one(r, c):
            row_copy(src, 0, xs_ref, 0, scatter_sems.at[parity]).wait()
            return c

        lax.fori_loop(0, 2 * tm, one, 0, unroll=ISSUE_UNROLL)

    @pl.when(i > 0)
    def _():
        drain((i - 1) % 2)

    @pl.when(i == last)
    def _():
        drain(i % 2)


def _dispatch(pad_start, pad_len, dest3, h2, n_rows):
    tm = dest3.shape[2] // 2
    grid_spec = pltpu.PrefetchScalarGridSpec(
        num_scalar_prefetch=2,
        grid=(dest3.shape[0],),
        in_specs=[pl.BlockSpec((1, 1, 2 * tm), lambda i, ps, pn: (i, 0, 0), memory_space=pltpu.SMEM),
                  pl.BlockSpec(memory_space=pl.ANY)],
        out_specs=pl.BlockSpec(memory_space=pl.ANY),
        scratch_shapes=[pltpu.VMEM((RING, tm * SUBLANES, LANES), F32),
                        pltpu.VMEM((ZERO_TOKENS * SUBLANES, LANES), F32),
                        pltpu.SemaphoreType.DMA((RING,)), pltpu.SemaphoreType.DMA((2,)),
                        pltpu.SemaphoreType.DMA(())],
    )
    return pl.pallas_call(
        _dispatch_kernel,
        grid_spec=grid_spec,
        out_shape=jax.ShapeDtypeStruct((n_rows * SUBLANES, LANES), F32),
        compiler_params=_params(("arbitrary",)),
        name="dispatch",
    )(pad_start, pad_len, dest3, h2)


def _experts_kernel(te_ref, ts_ref, nv_ref, xs_ref, wg_ref, wu_ref, wd_ref, ys_ref,
                    wgb_ref, wub_ref, wdb_ref):
    del ts_ref
    t = pl.program_id(0)

    @pl.when(jnp.logical_or(t == 0, te_ref[t] != te_ref[jnp.maximum(t - 1, 0)]))
    def _():
        wgb_ref[...] = wg_ref[0].astype(BF16)
        wub_ref[...] = wu_ref[0].astype(BF16)
        wdb_ref[...] = wd_ref[0].astype(BF16)

    @pl.when(t < nv_ref[0])
    def _():
        xb = _load_rows(xs_ref).astype(BF16)
        a = _dot(xb, wgb_ref[...])
        u = _dot(xb, wub_ref[...])
        act = (a * jax.nn.sigmoid(a) * u).astype(BF16)
        _store_rows(ys_ref, _dot(act, wdb_ref[...]))

    @pl.when(t >= nv_ref[0])
    def _():
        ys_ref[...] = jnp.zeros_like(ys_ref)


def _experts(tile_expert, tile_src, n_valid, xs, wg, wu, wd):
    tm = TM_EXPERT
    n_tiles = xs.shape[0] // (tm * SUBLANES)
    _, d, f = wg.shape
    row_block = (tm * SUBLANES, LANES)
    grid_spec = pltpu.PrefetchScalarGridSpec(
        num_scalar_prefetch=3,
        grid=(n_tiles,),
        in_specs=[pl.BlockSpec(row_block, lambda t, te, ts, nv: (ts[t], 0)),
                  pl.BlockSpec((1, d, f), lambda t, te, ts, nv: (te[t], 0, 0)),
                  pl.BlockSpec((1, d, f), lambda t, te, ts, nv: (te[t], 0, 0)),
                  pl.BlockSpec((1, f, d), lambda t, te, ts, nv: (te[t], 0, 0))],
        out_specs=pl.BlockSpec(row_block, lambda t, te, ts, nv: (t, 0)),
        scratch_shapes=[pltpu.VMEM((d, f), BF16), pltpu.VMEM((d, f), BF16), pltpu.VMEM((f, d), BF16)],
    )
    return pl.pallas_call(
        _experts_kernel,
        grid_spec=grid_spec,
        out_shape=jax.ShapeDtypeStruct((n_tiles * tm * SUBLANES, LANES), F32),
        compiler_params=_params(("arbitrary",)),
        name="experts",
    )(tile_expert, tile_src, n_valid, xs, wg, wu, wd)


def _combine_kernel(dest_ref, next_ref, x2_ref, wts_ref, g_ref, ys_ref, o_ref, buf_ref, sems):
    i = pl.program_id(0)
    tm = dest_ref.shape[2] // 2
    slot = i % 2

    def gather(d_ref, to):
        def issue(r, c):
            for k in range(2):
                _row_copy(ys_ref, d_ref[0, 0, k * tm + r], buf_ref.at[to, k], r,
                          sems.at[to]).start(priority=k)
            return c

        lax.fori_loop(0, tm, issue, 0, unroll=ISSUE_UNROLL)

    @pl.when(i == 0)
    def _():
        gather(dest_ref, 0)

    @pl.when(i + 1 < pl.num_programs(0))
    def _():
        gather(next_ref, 1 - slot)

    def drain(r, c):
        _row_copy(ys_ref, 0, buf_ref.at[slot, 0], 0, sems.at[slot]).wait()
        return c

    lax.fori_loop(0, 2 * tm, drain, 0, unroll=ISSUE_UNROLL)
    w = wts_ref[...]
    y = (x2_ref[...] + w[:, 0:1] * _load_rows(buf_ref.at[slot, 0])
         + w[:, 1:2] * _load_rows(buf_ref.at[slot, 1]))
    o_ref[...] = _rms(y, g_ref[...])


def _combine(dest3, x2, wts, g, ys):
    t, d = x2.shape
    tm = dest3.shape[2] // 2
    rows = lambda i: (i, 0)
    n = t // tm
    return pl.pallas_call(
        _combine_kernel,
        grid=(n,),
        in_specs=[pl.BlockSpec((1, 1, 2 * tm), lambda i: (i, 0, 0), memory_space=pltpu.SMEM),
                  pl.BlockSpec((1, 1, 2 * tm), lambda i: (jnp.minimum(i + 1, n - 1), 0, 0),
                               memory_space=pltpu.SMEM),
                  pl.BlockSpec((tm, d), rows), pl.BlockSpec((tm, 2), rows),
                  pl.BlockSpec((1, d), lambda i: (0, 0)),
                  pl.BlockSpec(memory_space=pl.ANY)],
        out_specs=pl.BlockSpec((tm, d), rows),
        out_shape=jax.ShapeDtypeStruct((t, d), F32),
        scratch_shapes=[pltpu.VMEM((2, 2, tm * SUBLANES, LANES), F32), pltpu.SemaphoreType.DMA((2,))],
        compiler_params=_params(("arbitrary",)),
        name="combine",
    )(dest3, dest3, x2, wts, g, ys)


def _rotary_tables(seq):
    half = ROPE_DIM // 2
    inv_freq = ROPE_THETA ** (-jnp.arange(half, dtype=F32) / half)
    ang = jnp.arange(seq, dtype=F32)[:, None] * inv_freq[None, :]
    cos, sin = jnp.cos(ang), jnp.sin(ang)
    ones = jnp.ones((seq, HEAD_DIM - ROPE_DIM), F32)
    zeros = jnp.zeros((seq, HEAD_DIM - ROPE_DIM), F32)
    zh = jnp.zeros((seq, half), F32)
    cosm = jnp.concatenate([cos, cos, ones], axis=1)
    sina = jnp.concatenate([-sin, zh, zeros], axis=1)
    sinb = jnp.concatenate([zh, sin, zeros], axis=1)
    tile = lambda a: jnp.tile(a, (1, HEADS_PER_BLOCK))
    return tile(cosm), tile(sina), tile(sinb)


def _pad_lanes(a):
    return jnp.pad(a, ((0, 0), (0, LANES - a.shape[1])))


def kernel(x, norm_mix_g, w_in, b_forget, fox_out_g, moba_out_g, w_out, norm_ffn_g, w_router_group,
           b_router_group, w_router_expert, b_router_expert, w_gate, w_up, w_down, norm_final_g):
    b, s, d = x.shape
    t = b * s
    assert w_in.shape[0] == 1, "the closing RMSNorm is fused into the only layer's combine step"
    cosm, sina, sinb = _rotary_tables(s)
    n_tiles = (2 * t) // TM_EXPERT + N_EXPERTS
    fw3 = 3 * FOX_WIDTH
    m0 = fw3 + N_FOX_HEADS
    wl = w_in[0]
    w_main = jnp.concatenate([wl[:, :2 * FOX_WIDTH], wl[:, m0:m0 + 2 * MOBA_WIDTH]],
                             axis=1).astype(BF16)
    w_vt = jnp.stack([wl[:, 2 * FOX_WIDTH:fw3].T, wl[:, m0 + 2 * MOBA_WIDTH:].T]).astype(BF16)
    aux_head = jnp.arange(AUX_PER_HEAD * N_FOX_HEADS) // AUX_PER_HEAD
    w_logit = _pad_lanes(wl[:, fw3:m0][:, aux_head]).astype(BF16)
    b_logit = _pad_lanes(b_forget[0][None, aux_head])
    fq, fqa, fk, fka, fvt, mq, mk, mvt, kmean = _inproj(
        x, norm_mix_g[0][None, :], w_main, w_vt, w_logit, b_logit, cosm, sina, sinb)
    fox = _fox(fq, fqa, fk, fka, fvt)
    moba = _moba(mq, mk, mvt, kmean)

    w_router = _pad_lanes(jnp.concatenate(
        [w_router_group[0], w_router_expert[0].reshape(d, N_EXPERTS)], axis=1))
    w_router_hi = w_router.astype(BF16)
    w_router = jnp.concatenate([w_router_hi, (w_router - w_router_hi.astype(F32)).astype(BF16)], axis=1)
    b_router =_pad_lanes(jnp.concatenate(
        [b_router_group[0], b_router_expert[0].reshape(N_EXPERTS)])[None, :])
    x2, h2, wts, route, counts = _postattn(
        x.reshape(t, d), fox.reshape(t, FOX_WIDTH), moba.reshape(t, MOBA_WIDTH),
        fox_out_g[0][None, :], moba_out_g[0][None, :], w_out[0].astype(BF16),
        norm_ffn_g[0][None, :], w_router, b_router)

    counts = counts[0, :N_EXPERTS]
    padded = (counts + TM_EXPERT - 1) // TM_EXPERT * TM_EXPERT
    ends = jnp.cumsum(padded)
    starts = ends - padded
    expert_ids = jnp.arange(N_EXPERTS, dtype=I32)[:, None, None]
    dest = route[2:4] + jnp.sum(jnp.where(route[None, 0:2] == expert_ids, starts[:, None, None], 0),
                                axis=0)
    dest3 = dest.reshape(2, t // TM_ROWS, TM_ROWS).transpose(1, 0, 2).reshape(t // TM_ROWS, 1, 2 * TM_ROWS)
    n_rows = n_tiles * TM_EXPERT
    pad_start = jnp.concatenate([starts + counts, ends[-1:]]).astype(I32)
    pad_len = jnp.concatenate([padded - counts, n_rows - ends[-1:]]).astype(I32)
    n_valid = ends[-1] // TM_EXPERT
    tile_src = jnp.minimum(jnp.arange(n_tiles, dtype=I32), n_valid - 1)
    tile_expert = jnp.sum(ends[None, :] <= (tile_src * TM_EXPERT)[:, None], axis=1).astype(I32)
    xs = _dispatch(pad_start, pad_len, dest3, h2, n_rows)
    ys = _experts(tile_expert, tile_src, n_valid.reshape(1).astype(I32), xs,
                  w_gate[0], w_up[0], w_down[0])
    return _combine(dest3, x2, wts, norm_final_g[None, :], ys).reshape(b, s, d)
```

```python
import math

import jax
import jax.numpy as jnp
from jax import lax
from jax.experimental import pallas as pl
from jax.experimental.pallas import tpu as pltpu

F32 = jnp.float32
BF16 = jnp.bfloat16
I32 = jnp.int32

HEAD_DIM = 64
N_FOX_HEADS = 8
N_MOBA_HEADS = 8
FOX_WIDTH = N_FOX_HEADS * HEAD_DIM
MOBA_WIDTH = N_MOBA_HEADS * HEAD_DIM
MOBA_BLOCK = 256
MOBA_TOPK = 3
ROPE_THETA = 500000.0
ROPE_DIM = HEAD_DIM // 4
N_GROUPS = 4
EXPERTS_PER_GROUP = 8
N_EXPERTS = N_GROUPS * EXPERTS_PER_GROUP
EPS = 1e-6

LANES = 128
SUBLANES = 8
BF16_SUBLANES = 16
LOG2_E = math.log2(math.e)
HEADS_PER_BLOCK = LANES // HEAD_DIM
BLOCKS_PER_STEP = 4
HEADS_PER_STEP = HEADS_PER_BLOCK * BLOCKS_PER_STEP
STEP_LANES = LANES * BLOCKS_PER_STEP
VMEM_LIMIT = 56 * 1024 * 1024
AUX_PER_HEAD = 6

TM_PROJ = 512
TQ = 256
TM_EXPERT = 512
TM_ROWS = 256
ISSUE_UNROLL = 8

NEG_INF = float("-inf")
MASKED = -1e30


def _params(sem):
    return pltpu.CompilerParams(dimension_semantics=sem, vmem_limit_bytes=VMEM_LIMIT)


def _rms(x, g):
    return x * lax.rsqrt(jnp.mean(x * x, axis=-1, keepdims=True) + EPS) * g


def _split3(x):
    hi = x.astype(BF16)
    r = x - hi.astype(F32)
    mid = r.astype(BF16)
    lo = (r - mid.astype(F32)).astype(BF16)
    return hi, mid, lo


def _dot(a, b):
    return jnp.dot(a, b, preferred_element_type=F32)


def _dot_nt(a, b):
    return lax.dot_general(a, b, (((1,), (1,)), ((), ())), preferred_element_type=F32)


def _inproj_kernel(x_ref, g_ref, w_ref, wvt_ref, wl_ref, bf_ref, kind_ref, cosm_ref, sina_ref, sinb_ref,
                   fq_ref, fqa_ref, fk_ref, fka_ref, fvt_ref, mq_ref, mk_ref, mvt_ref, kmean_ref,
                   carry_ref):
    j = pl.program_id(1)
    tm = x_ref.shape[1]
    tk = fvt_ref.shape[3]
    h = _rms(x_ref[0], g_ref[...]).astype(BF16)
    scale = HEAD_DIM ** -0.5 * LOG2_E

    def proj(seg):
        return _dot(h, w_ref[:, seg * FOX_WIDTH:(seg + 1) * FOX_WIDTH])

    fq_ref[0] = (proj(0) * scale).astype(BF16)
    fk_ref[0] = proj(1).astype(BF16)

    for vt_ref, seg in ((fvt_ref, 0), (mvt_ref, 1)):
        vt = _dot_nt(wvt_ref[seg], h).astype(BF16)
        for r in range(tm // tk):
            vt_ref[0, r] = vt[:, r * tk:(r + 1) * tk]

    cosm, sina, sinb = cosm_ref[...], sina_ref[...], sinb_ref[...]

    def rotary(t):
        outs = []
        for g in range(MOBA_WIDTH // LANES):
            tg = t[:, g * LANES:(g + 1) * LANES]
            outs.append(tg * cosm + pltpu.roll(tg, LANES - ROPE_DIM // 2, 1) * sina
                        + pltpu.roll(tg, ROPE_DIM // 2, 1) * sinb)
        return jnp.concatenate(outs, axis=1)

    mq_ref[0] = (rotary(proj(2)) * scale).astype(BF16)
    mk = rotary(proj(3))
    mk_ref[0] = mk.astype(BF16)
    nblk_tile = tm // MOBA_BLOCK
    means = [jnp.mean(mk[r * MOBA_BLOCK:(r + 1) * MOBA_BLOCK], axis=0, keepdims=True)
             for r in range(nblk_tile)]
    means += [jnp.zeros_like(means[0])] * (kmean_ref.shape[2] - nblk_tile)
    kmean_ref[0, 0] = jnp.concatenate(means, axis=0)

    z = _dot(h, wl_ref[...]) + bf_ref[...]
    log_f = jnp.minimum(z, 0.0) - jnp.log1p(jnp.exp(-jnp.abs(z)))

    @pl.when(j == 0)
    def _():
        carry_ref[...] = jnp.zeros_like(carry_ref)

    half = tm // 2
    row = lax.broadcasted_iota(I32, (half, half), 0)
    col = lax.broadcasted_iota(I32, (half, half), 1)
    tri = jnp.where(row >= col, 1.0, 0.0).astype(BF16)
    pieces = jnp.concatenate(_split3(log_f), axis=1)
    carry = carry_ref[...]
    cs = []
    for r in range(2):
        local = _dot(tri, pieces[r * half:(r + 1) * half])
        cs.append(local[:, :LANES] + local[:, LANES:2 * LANES] + local[:, 2 * LANES:] + carry)
        carry = cs[-1][half - 1:half, :]
    carry_ref[...] = carry
    c = jnp.concatenate(cs, axis=0) * LOG2_E

    hi = c.astype(BF16).astype(F32)
    mid = (c - hi).astype(BF16).astype(F32)
    lo = c - hi - mid
    kind = kind_ref[...]
    one = jnp.where(kind < AUX_PER_HEAD, 1.0, 0.0)
    pick = lambda base: jnp.where(kind == base, hi, jnp.where(kind == base + 1, mid,
                                  jnp.where(kind == base + 2, lo, 0.0)))
    fqa_ref[0] = (pick(3) + jnp.where(kind < 3, one, 0.0)).astype(BF16)
    fka_ref[0] = (jnp.where(kind >= 3, one, 0.0) - pick(0)).astype(BF16)


def _inproj(x, g, w_main, w_vt, w_logit, b_logit, cosm, sina, sinb):
    b, s, d = x.shape
    tm, tk = TM_PROJ, TQ
    lane = jnp.arange(LANES, dtype=I32)
    kind = jnp.where(lane < AUX_PER_HEAD * N_FOX_HEADS, lane % AUX_PER_HEAD, AUX_PER_HEAD)[None, :]
    act = jax.ShapeDtypeStruct((b, s, FOX_WIDTH), BF16)
    aux = jax.ShapeDtypeStruct((b, s, LANES), BF16)
    vt = jax.ShapeDtypeStruct((b, s // tk, FOX_WIDTH, tk), BF16)
    out_shape = [act, aux, act, aux, vt, act, act, vt,
                 jax.ShapeDtypeStruct((b, s // tm, SUBLANES, MOBA_WIDTH), F32)]
    act_spec = pl.BlockSpec((1, tm, FOX_WIDTH), lambda bi, j: (bi, j, 0))
    aux_spec = pl.BlockSpec((1, tm, LANES), lambda bi, j: (bi, j, 0))
    vt_spec = pl.BlockSpec((1, tm // tk, FOX_WIDTH, tk), lambda bi, j: (bi, j, 0, 0))
    tab_spec = pl.BlockSpec((tm, LANES), lambda bi, j: (j, 0))
    const2 = lambda bi, j: (0, 0)
    const3 = lambda bi, j: (0, 0, 0)
    *acts, kmean = pl.pallas_call(
        _inproj_kernel,
        grid=(b, s // tm),
        in_specs=[pl.BlockSpec((1, tm, d), lambda bi, j: (bi, j, 0)),
                  pl.BlockSpec((1, d), const2),
                  pl.BlockSpec(w_main.shape, const2),
                  pl.BlockSpec(w_vt.shape, const3),
                  pl.BlockSpec(w_logit.shape, const2),
                  pl.BlockSpec((1, LANES), const2), pl.BlockSpec((1, LANES), const2),
                  tab_spec, tab_spec, tab_spec],
        out_specs=[act_spec, aux_spec, act_spec, aux_spec, vt_spec, act_spec, act_spec, vt_spec,
                   pl.BlockSpec((1, 1, SUBLANES, MOBA_WIDTH), lambda bi, j: (bi, j, 0, 0))],
        out_shape=out_shape,
        scratch_shapes=[pltpu.VMEM((1, LANES), F32)],
        compiler_params=_params(("arbitrary", "arbitrary")),
        name="inproj",
    )(x, g, w_main, w_vt, w_logit, b_logit, kind, cosm, sina, sinb)
    kmean = kmean[:, :, :tm // MOBA_BLOCK].reshape(b, s // MOBA_BLOCK, MOBA_WIDTH)
    return (*acts, kmean)


def _softmax(heads, scores, m_ref):
    stats = []
    for hh, s in zip(heads, scores):
        m_prev = m_ref[hh]
        m_new = jnp.maximum(m_prev, jnp.max(s, axis=0, keepdims=True))
        m_ref[hh] = m_new
        stats.append((jnp.exp2(m_prev - m_new), m_new))
    return [(alpha, jnp.exp2(s - m_new).astype(BF16)) for (alpha, m_new), s in zip(stats, scores)]


def _values(weighted, vt, l_ref, acc_ref):
    ones = jnp.ones((BF16_SUBLANES, vt.shape[1]), BF16)
    for hh, (alpha, p) in enumerate(weighted):
        rows = slice(hh * HEAD_DIM, (hh + 1) * HEAD_DIM)
        pv = _dot(jnp.concatenate([vt[rows, :], ones], axis=0), p)
        acc_ref[rows, :] = alpha * acc_ref[rows, :] + pv[:HEAD_DIM]
        l_ref[hh] = alpha * l_ref[hh] + pv[HEAD_DIM:HEAD_DIM + 1]


def _attend_tiles(i, scores_of, vt_ref, m_ref, l_ref, acc_ref):
    heads = tuple(range(HEADS_PER_STEP))

    def block(tiles):
        scores = [scores_of(kt, diag, heads) for kt, diag in tiles]
        for (kt, _), s in zip(tiles, scores):
            _values(_softmax(heads, s, m_ref), vt_ref[0, kt], l_ref, acc_ref)

    @pl.when(i % 2 == 1)
    def _():
        block([(i, True), (i - 1, False)])

    @pl.when(i % 2 == 0)
    def _():
        block([(i, True)])

    paired = i - i % 2

    def trip(j, c):
        block([(4 * j + r, False) for r in range(4)])
        return c

    lax.fori_loop(0, paired // 4, trip, 0)

    @pl.when(paired % 4 == 2)
    def _():
        block([(paired - 2, False), (paired - 1, False)])


def _attn_init(m_ref, l_ref, acc_ref):
    m_ref[...] = jnp.full(m_ref.shape, NEG_INF, F32)
    l_ref[...] = jnp.zeros_like(l_ref)
    acc_ref[...] = jnp.zeros_like(acc_ref)


def _attn_finish(o_ref, l_ref, acc_ref):
    out_t = jnp.concatenate(
        [acc_ref[hh * HEAD_DIM:(hh + 1) * HEAD_DIM, :] / l_ref[hh] for hh in range(HEADS_PER_STEP)],
        axis=0)
    o_ref[0] = out_t.T


def _block(a, g):
    return a[:, g * LANES:(g + 1) * LANES]


def _per_head(a, width):
    first = lax.broadcasted_iota(I32, (a.shape[0], LANES), 1) < width
    zero = jnp.zeros((a.shape[0], LANES), a.dtype)
    out = []
    for g in range(BLOCKS_PER_STEP):
        blk = _block(a, g)
        out += [jnp.where(first, blk, zero), jnp.where(first, zero, blk)]
    return out


def _key_le_query(tq):
    return lax.broadcasted_iota(I32, (tq, tq), 0) <= lax.broadcasted_iota(I32, (tq, tq), 1)


def _fox_kernel(q_ref, qa_ref, k_ref, ka_ref, vt_ref, o_ref, m_ref, l_ref, acc_ref):
    i = pl.program_id(2)
    tq = q_ref.shape[1]
    qa = qa_ref[0]
    lane = lax.broadcasted_iota(I32, qa.shape, 1) - pl.program_id(1) * (HEADS_PER_STEP * AUX_PER_HEAD)
    own_aux = lambda hh: jnp.logical_and(lane >= hh * AUX_PER_HEAD, lane < (hh + 1) * AUX_PER_HEAD)
    qq = [jnp.concatenate([qm, jnp.where(own_aux(hh), qa, jnp.zeros_like(qa))], axis=1)
          for hh, qm in enumerate(_per_head(q_ref[0], HEAD_DIM))]
    causal = _key_le_query(tq)
    _attn_init(m_ref, l_ref, acc_ref)

    def scores_of(kt, diag, heads):
        ks = pl.multiple_of(kt * tq, tq)
        ka = ka_ref[0, pl.ds(ks, tq), :]
        kk = {g: jnp.concatenate([k_ref[0, pl.ds(ks, tq), g * LANES:(g + 1) * LANES], ka], axis=1)
              for g in sorted({hh // HEADS_PER_BLOCK for hh in heads})}
        scores = [_dot_nt(kk[hh // HEADS_PER_BLOCK], qq[hh]) for hh in heads]
        if diag:
            scores = [jnp.where(causal, s, NEG_INF) for s in scores]
        return tuple(scores)

    _attend_tiles(i, scores_of, vt_ref, m_ref, l_ref, acc_ref)
    _attn_finish(o_ref, l_ref, acc_ref)


def _moba_kernel(q_ref, k_ref, hot_ref, vt_ref, kmean_ref, o_ref, m_ref, l_ref, acc_ref):
    i = pl.program_id(2)
    tq = q_ref.shape[1]
    nblk = kmean_ref.shape[1]
    qs = _per_head(q_ref[0], HEAD_DIM)
    causal = _key_le_query(tq)
    _attn_init(m_ref, l_ref, acc_ref)

    km_parts = jnp.concatenate(_split3(kmean_ref[0]), axis=0)
    blk = lax.broadcasted_iota(I32, (nblk, tq), 0).astype(F32)
    past = blk < i.astype(F32)
    masks = []
    for hh in range(HEADS_PER_STEP):
        pieces = _dot_nt(_block(km_parts, hh // HEADS_PER_BLOCK), qs[hh])
        gate = pieces[:nblk] + pieces[nblk:2 * nblk] + pieces[2 * nblk:]
        sel = jnp.zeros((nblk, tq), jnp.bool_)
        for _ in range(MOBA_TOPK):
            remaining = jnp.logical_and(past, jnp.logical_not(sel))
            g = jnp.where(remaining, gate, NEG_INF)
            first = jnp.min(jnp.where(g == jnp.max(g, axis=0, keepdims=True), blk, float(nblk)),
                            axis=0, keepdims=True)
            sel = jnp.logical_or(sel, jnp.logical_and(blk == first, remaining))
        masks.append(jnp.where(sel, 0.0, MASKED))
    unused = LANES - HEADS_PER_STEP * nblk
    masks += [jnp.zeros((unused, tq), F32)] if unused else []
    qa = jnp.concatenate(masks, axis=0).T.astype(BF16)
    lane = lax.broadcasted_iota(I32, qa.shape, 1)
    own = lambda hh: jnp.logical_and(lane >= hh * nblk, lane < (hh + 1) * nblk)
    qq = [jnp.concatenate([qs[hh], jnp.where(own(hh), qa, jnp.zeros_like(qa))], axis=1)
          for hh in range(HEADS_PER_STEP)]

    def scores_of(kt, diag, heads):
        ks = pl.multiple_of(kt * tq, tq)
        k = lambda hh: k_ref[0, pl.ds(ks, tq), (hh // HEADS_PER_BLOCK) * LANES:(hh // HEADS_PER_BLOCK + 1) * LANES]
        if diag:
            return tuple(jnp.where(causal, _dot_nt(k(hh), qs[hh]), NEG_INF) for hh in heads)
        hot = hot_ref[pl.ds(ks, tq), :]
        return tuple(_dot_nt(jnp.concatenate([k(hh), hot], axis=1), qq[hh]) for hh in heads)

    _attend_tiles(i, scores_of, vt_ref, m_ref, l_ref, acc_ref)
    _attn_finish(o_ref, l_ref, acc_ref)


def _attn_scratch(tq):
    return [pltpu.VMEM((HEADS_PER_STEP, 1, tq), F32), pltpu.VMEM((HEADS_PER_STEP, 1, tq), F32),
            pltpu.VMEM((STEP_LANES, tq), F32)]


def _attn_specs(s, tq):
    q_spec = pl.BlockSpec((1, tq, STEP_LANES), lambda bi, hb, i: (bi, i, hb))
    k_spec = pl.BlockSpec((1, s, STEP_LANES), lambda bi, hb, i: (bi, 0, hb))
    vt_spec = pl.BlockSpec((1, s // tq, STEP_LANES, tq), lambda bi, hb, i: (bi, 0, hb, 0))
    return q_spec, k_spec, vt_spec


def _fox(q, qa, k, ka, vt):
    b, s, width = q.shape
    tq = TQ
    q_spec, k_spec, vt_spec = _attn_specs(s, tq)
    return pl.pallas_call(
        _fox_kernel,
        grid=(b, width // STEP_LANES, s // tq),
        in_specs=[q_spec, pl.BlockSpec((1, tq, LANES), lambda bi, hb, i: (bi, i, 0)),
                  k_spec, pl.BlockSpec((1, s, LANES), lambda bi, hb, i: (bi, 0, 0)), vt_spec],
        out_specs=q_spec,
        out_shape=jax.ShapeDtypeStruct((b, s, width), F32),
        scratch_shapes=_attn_scratch(tq),
        compiler_params=_params(("arbitrary", "arbitrary", "arbitrary")),
        name="fox",
    )(q, qa, k, ka, vt)


def _moba(q, k, vt, kmean):
    b, s, width = q.shape
    tq = TQ
    nblk = kmean.shape[1]
    assert HEADS_PER_STEP * nblk <= LANES, "one aux lane per (head, key block)"
    lane = jnp.arange(LANES)
    hot = jnp.logical_and(lane[None, :] < HEADS_PER_STEP * nblk,
                          lane[None, :] % nblk == jnp.arange(s)[:, None] // MOBA_BLOCK).astype(BF16)
    q_spec, k_spec, vt_spec = _attn_specs(s, tq)
    return pl.pallas_call(
        _moba_kernel,
        grid=(b, width // STEP_LANES, s // tq),
        in_specs=[q_spec, k_spec, pl.BlockSpec((s, LANES), lambda bi, hb, i: (0, 0)), vt_spec,
                  pl.BlockSpec((1, nblk, STEP_LANES), lambda bi, hb, i: (bi, 0, hb))],
        out_specs=q_spec,
        out_shape=jax.ShapeDtypeStruct((b, s, width), F32),
        scratch_shapes=_attn_scratch(tq),
        compiler_params=_params(("arbitrary", "arbitrary", "arbitrary")),
        name="moba",
    )(q, k, hot, vt, kmean)


def _postattn_kernel(x_ref, fox_ref, moba_ref, gf_ref, gm_ref, wo_ref, gn_ref, wr_ref, br_ref,
                     x2_ref, h2_ref, wts_ref, route_ref, cnt_ref, carry_ref):
    t = pl.program_id(0)
    tm = x_ref.shape[0]
    fw = fox_ref.shape[1]
    mixed_f = _rms(fox_ref[...], gf_ref[...]).astype(BF16)
    mixed_m = _rms(moba_ref[...], gm_ref[...]).astype(BF16)
    x2 = x_ref[...] + _dot(mixed_f, wo_ref[:fw, :]) + _dot(mixed_m, wo_ref[fw:, :])
    x2_ref[...] = x2
    h2 = _rms(x2, gn_ref[...])
    _store_rows(h2_ref, h2)

    h_hi = h2.astype(BF16)
    h_lo = (h2 - h_hi.astype(F32)).astype(BF16)
    by_hi = _dot(h_hi, wr_ref[...])
    logits = by_hi[:, :LANES] + by_hi[:, LANES:] + _dot(h_lo, wr_ref[:, :LANES]) + br_ref[...]
    lane = lax.broadcasted_iota(I32, (tm, LANES), 1).astype(F32)

    def first_max(vals):
        mx = jnp.max(vals, axis=1, keepdims=True)
        return mx, jnp.min(jnp.where(vals == mx, lane, float(LANES)), axis=1, keepdims=True)

    gl = jnp.where(lane < N_GROUPS, logits, NEG_INF)
    gmax, g_idx = first_max(gl)
    g_top = 1.0 / jnp.sum(jnp.exp(gl - gmax), axis=1, keepdims=True)
    e_lo = N_GROUPS + EXPERTS_PER_GROUP * g_idx
    el = jnp.where(jnp.logical_and(lane >= e_lo, lane < e_lo + EXPERTS_PER_GROUP), logits, NEG_INF)
    emax, i1 = first_max(el)
    esum = jnp.sum(jnp.exp(el - emax), axis=1, keepdims=True)
    e2max, i2 = first_max(jnp.where(lane == i1, NEG_INF, el))
    p1 = 1.0 / esum
    p2 = jnp.exp(e2max - emax) / esum
    w1 = p1 / (p1 + p2) * g_top
    w2 = p2 / (p1 + p2) * g_top
    e1 = i1 - N_GROUPS
    e2 = i2 - N_GROUPS
    wts_ref[...] = jnp.where(lane == 0.0, w1, w2)[:, :2]

    @pl.when(t == 0)
    def _():
        carry_ref[...] = jnp.zeros_like(carry_ref)

    row = lax.broadcasted_iota(I32, (tm, tm), 0)
    col = lax.broadcasted_iota(I32, (tm, tm), 1)
    strict = jnp.where(row > col, 1.0, 0.0).astype(BF16)
    hit1 = lane == e1
    hit2 = lane == e2
    oh1 = jnp.where(hit1, 1.0, 0.0)
    oh2 = jnp.where(hit2, 1.0, 0.0)
    tot1 = jnp.sum(oh1, axis=0, keepdims=True)
    tot2 = jnp.sum(oh2, axis=0, keepdims=True)
    base = carry_ref[...]
    before = _dot(strict, jnp.concatenate([oh1, oh2], axis=1).astype(BF16))
    before1 = before[:, :LANES] + base
    before2 = before[:, LANES:] + (base + tot1)
    r1 = jnp.sum(jnp.where(hit1, before1, 0.0), axis=1, keepdims=True)
    r2 = jnp.sum(jnp.where(hit2, before2, 0.0), axis=1, keepdims=True)
    record = jnp.where(lane == 0.0, e1, jnp.where(lane == 1.0, e2, jnp.where(lane == 2.0, r1, r2)))
    route_ref[...] = record.T[:SUBLANES, :].astype(I32)
    total = base + tot1 + tot2
    carry_ref[...] = total
    cnt_ref[...] = total.astype(I32)


def _postattn(x, fox, moba, gf, gm, wo, gn, wr, br):
    t, d = x.shape
    tm = TM_PROJ
    fw = fox.shape[1]
    const = lambda i: (0, 0)
    rows = lambda i: (i, 0)
    return pl.pallas_call(
        _postattn_kernel,
        grid=(t // tm,),
        in_specs=[pl.BlockSpec((tm, d), rows), pl.BlockSpec((tm, fw), rows),
                  pl.BlockSpec((tm, moba.shape[1]), rows),
                  pl.BlockSpec((1, fw), const), pl.BlockSpec((1, moba.shape[1]), const),
                  pl.BlockSpec(wo.shape, const), pl.BlockSpec((1, d), const),
                  pl.BlockSpec(wr.shape, const), pl.BlockSpec((1, LANES), const)],
        out_specs=[pl.BlockSpec((tm, d), rows), pl.BlockSpec((tm * SUBLANES, LANES), rows),
                   pl.BlockSpec((tm, 2), rows), pl.BlockSpec((SUBLANES, tm), lambda i: (0, i)),
                   pl.BlockSpec((1, LANES), const)],
        out_shape=[jax.ShapeDtypeStruct((t, d), F32), jax.ShapeDtypeStruct((t * SUBLANES, LANES), F32),
                   jax.ShapeDtypeStruct((t, 2), F32), jax.ShapeDtypeStruct((SUBLANES, t), I32),
                   jax.ShapeDtypeStruct((1, LANES), I32)],
        scratch_shapes=[pltpu.VMEM((1, LANES), F32)],
        compiler_params=_params(("arbitrary",)),
        name="postattn",
    )(x, fox, moba, gf, gm, wo, gn, wr, br)


def _store_rows(ref, val):
    for g in range(SUBLANES):
        ref[pl.ds(g, val.shape[0], stride=SUBLANES), :] = val[:, g * LANES:(g + 1) * LANES]


def _load_rows(ref):
    tokens = ref.shape[0] // SUBLANES
    return jnp.concatenate([ref[pl.ds(g, tokens, stride=SUBLANES), :] for g in range(SUBLANES)], axis=1)


def _row_copy(src, src_row, dst, dst_row, sem, tokens=1):
    window = lambda r: pl.ds(pl.multiple_of(r * SUBLANES, SUBLANES), tokens * SUBLANES)
    return pltpu.make_async_copy(src.at[window(src_row)], dst.at[window(dst_row)], sem)


RING = 3
ZERO_TOKENS = 256


def _dispatch_kernel(pad_start_ref, pad_len_ref, dest_ref, h_ref, xs_ref,
                     ring_ref, zero_ref, fetch_sems, scatter_sems, pad_sem):
    i = pl.program_id(0)
    last = pl.num_programs(0) - 1
    tm = dest_ref.shape[2] // 2
    tile_rows = tm * SUBLANES
    row_copy = _row_copy
    zero_tokens = zero_ref.shape[0] // SUBLANES

    def fetch(tile):
        start = pl.multiple_of(tile * tile_rows, tile_rows)
        slot = lax.rem(tile, RING)
        return pltpu.make_async_copy(h_ref.at[pl.ds(start, tile_rows)], ring_ref.at[slot],
                                     fetch_sems.at[slot])

    def zero_fill(e, act):
        start, n = pad_start_ref[e], pad_len_ref[e]
        whole = lax.shift_right_logical(n, zero_tokens.bit_length() - 1)
        rest = jnp.bitwise_and(n, zero_tokens - 1)

        def chunk(c, carry):
            act(_row_copy(zero_ref, 0, xs_ref, start + c * zero_tokens, pad_sem, tokens=zero_tokens))
            return carry

        lax.fori_loop(0, whole, chunk, 0)
        bit = zero_tokens // 2
        while bit:
            @pl.when(jnp.bitwise_and(rest, bit) != 0)
            def _(bit=bit):
                above = jnp.bitwise_and(rest, -2 * bit)
                act(_row_copy(zero_ref, 0, xs_ref, start + whole * zero_tokens + above, pad_sem, tokens=bit))
            bit //= 2

    @pl.when(i == 0)
    def _():
        fetch(0).start()
        zero_ref[...] = jnp.zeros_like(zero_ref)
        n_regions = pad_start_ref.shape[0]
        lax.fori_loop(0, n_regions, lambda e, c: (zero_fill(e, lambda cp: cp.start()), c)[1], 0)
        lax.fori_loop(0, n_regions, lambda e, c: (zero_fill(e, lambda cp: cp.wait()), c)[1], 0)

    @pl.when(i < last)
    def _():
        fetch(i + 1).start()

    fetch(i).wait()
    src = ring_ref.at[lax.rem(i, RING)]

    def issue(r, c):
        for k in range(2):
            row_copy(src, r, xs_ref, dest_ref[0, 0, k * tm + r], scatter_sems.at[i % 2]).start(priority=k)
        return c

    lax.fori_loop(0, tm, issue, 0, unroll=ISSUE_UNROLL)

    def drain(parity):
        def one(r, c):
            row_copy(src, 0, xs_ref, 0, scatter_sems.at[parity]).wait()
            return c

        lax.fori_loop(0, 2 * tm, one, 0, unroll=ISSUE_UNROLL)

    @pl.when(i > 0)
    def _():
        drain((i - 1) % 2)

    @pl.when(i == last)
    def _():
        drain(i % 2)


def _dispatch(pad_start, pad_len, dest3, h2, n_rows):
    tm = dest3.shape[2] // 2
    grid_spec = pltpu.PrefetchScalarGridSpec(
        num_scalar_prefetch=2,
        grid=(dest3.shape[0],),
        in_specs=[pl.BlockSpec((1, 1, 2 * tm), lambda i, ps, pn: (i, 0, 0), memory_space=pltpu.SMEM),
                  pl.BlockSpec(memory_space=pl.ANY)],
        out_specs=pl.BlockSpec(memory_space=pl.ANY),
        scratch_shapes=[pltpu.VMEM((RING, tm * SUBLANES, LANES), F32),
                        pltpu.VMEM((ZERO_TOKENS * SUBLANES, LANES), F32),
                        pltpu.SemaphoreType.DMA((RING,)), pltpu.SemaphoreType.DMA((2,)),
                        pltpu.SemaphoreType.DMA(())],
    )
    return pl.pallas_call(
        _dispatch_kernel,
        grid_spec=grid_spec,
        out_shape=jax.ShapeDtypeStruct((n_rows * SUBLANES, LANES), F32),
        compiler_params=_params(("arbitrary",)),
        name="dispatch",
    )(pad_start, pad_len, dest3, h2)


def _experts_kernel(te_ref, ts_ref, nv_ref, xs_ref, wg_ref, wu_ref, wd_ref, ys_ref,
                    wgb_ref, wub_ref, wdb_ref):
    del ts_ref
    t = pl.program_id(0)

    @pl.when(jnp.logical_or(t == 0, te_ref[t] != te_ref[jnp.maximum(t - 1, 0)]))
    def _():
        wgb_ref[...] = wg_ref[0].astype(BF16)
        wub_ref[...] = wu_ref[0].astype(BF16)
        wdb_ref[...] = wd_ref[0].astype(BF16)

    @pl.when(t < nv_ref[0])
    def _():
        xb = _load_rows(xs_ref).astype(BF16)
        a = _dot(xb, wgb_ref[...])
        u = _dot(xb, wub_ref[...])
        act = (a * jax.nn.sigmoid(a) * u).astype(BF16)
        _store_rows(ys_ref, _dot(act, wdb_ref[...]))

    @pl.when(t >= nv_ref[0])
    def _():
        ys_ref[...] = jnp.zeros_like(ys_ref)


def _experts(tile_expert, tile_src, n_valid, xs, wg, wu, wd):
    tm = TM_EXPERT
    n_tiles = xs.shape[0] // (tm * SUBLANES)
    _, d, f = wg.shape
    row_block = (tm * SUBLANES, LANES)
    grid_spec = pltpu.PrefetchScalarGridSpec(
        num_scalar_prefetch=3,
        grid=(n_tiles,),
        in_specs=[pl.BlockSpec(row_block, lambda t, te, ts, nv: (ts[t], 0)),
                  pl.BlockSpec((1, d, f), lambda t, te, ts, nv: (te[t], 0, 0)),
                  pl.BlockSpec((1, d, f), lambda t, te, ts, nv: (te[t], 0, 0)),
                  pl.BlockSpec((1, f, d), lambda t, te, ts, nv: (te[t], 0, 0))],
        out_specs=pl.BlockSpec(row_block, lambda t, te, ts, nv: (t, 0)),
        scratch_shapes=[pltpu.VMEM((d, f), BF16), pltpu.VMEM((d, f), BF16), pltpu.VMEM((f, d), BF16)],
    )
    return pl.pallas_call(
        _experts_kernel,
        grid_spec=grid_spec,
        out_shape=jax.ShapeDtypeStruct((n_tiles * tm * SUBLANES, LANES), F32),
        compiler_params=_params(("arbitrary",)),
        name="experts",
    )(tile_expert, tile_src, n_valid, xs, wg, wu, wd)


def _combine_kernel(dest_ref, next_ref, x2_ref, wts_ref, g_ref, ys_ref, o_ref, buf_ref, sems):
    i = pl.program_id(0)
    tm = dest_ref.shape[2] // 2
    slot = i % 2

    def gather(d_ref, to):
        def issue(r, c):
            for k in range(2):
                _row_copy(ys_ref, d_ref[0, 0, k * tm + r], buf_ref.at[to, k], r,
                          sems.at[to]).start(priority=k)
            return c

        lax.fori_loop(0, tm, issue, 0, unroll=ISSUE_UNROLL)

    @pl.when(i == 0)
    def _():
        gather(dest_ref, 0)

    @pl.when(i + 1 < pl.num_programs(0))
    def _():
        gather(next_ref, 1 - slot)

    def drain(r, c):
        _row_copy(ys_ref, 0, buf_ref.at[slot, 0], 0, sems.at[slot]).wait()
        return c

    lax.fori_loop(0, 2 * tm, drain, 0, unroll=ISSUE_UNROLL)
    w = wts_ref[...]
    y = (x2_ref[...] + w[:, 0:1] * _load_rows(buf_ref.at[slot, 0])
         + w[:, 1:2] * _load_rows(buf_ref.at[slot, 1]))
    o_ref[...] = _rms(y, g_ref[...])


def _combine(dest3, x2, wts, g, ys):
    t, d = x2.shape
    tm = dest3.shape[2] // 2
    rows = lambda i: (i, 0)
    n = t // tm
    return pl.pallas_call(
        _combine_kernel,
        grid=(n,),
        in_specs=[pl.BlockSpec((1, 1, 2 * tm), lambda i: (i, 0, 0), memory_space=pltpu.SMEM),
                  pl.BlockSpec((1, 1, 2 * tm), lambda i: (jnp.minimum(i + 1, n - 1), 0, 0),
                               memory_space=pltpu.SMEM),
                  pl.BlockSpec((tm, d), rows), pl.BlockSpec((tm, 2), rows),
                  pl.BlockSpec((1, d), lambda i: (0, 0)),
                  pl.BlockSpec(memory_space=pl.ANY)],
        out_specs=pl.BlockSpec((tm, d), rows),
        out_shape=jax.ShapeDtypeStruct((t, d), F32),
        scratch_shapes=[pltpu.VMEM((2, 2, tm * SUBLANES, LANES), F32), pltpu.SemaphoreType.DMA((2,))],
        compiler_params=_params(("arbitrary",)),
        name="combine",
    )(dest3, dest3, x2, wts, g, ys)


def _rotary_tables(seq):
    half = ROPE_DIM // 2
    inv_freq = ROPE_THETA ** (-jnp.arange(half, dtype=F32) / half)
    ang = jnp.arange(seq, dtype=F32)[:, None] * inv_freq[None, :]
    cos, sin = jnp.cos(ang), jnp.sin(ang)
    ones = jnp.ones((seq, HEAD_DIM - ROPE_DIM), F32)
    zeros = jnp.zeros((seq, HEAD_DIM - ROPE_DIM), F32)
    zh = jnp.zeros((seq, half), F32)
    cosm = jnp.concatenate([cos, cos, ones], axis=1)
    sina = jnp.concatenate([-sin, zh, zeros], axis=1)
    sinb = jnp.concatenate([zh, sin, zeros], axis=1)
    tile = lambda a: jnp.tile(a, (1, HEADS_PER_BLOCK))
    return tile(cosm), tile(sina), tile(sinb)


def _pad_lanes(a):
    return jnp.pad(a, ((0, 0), (0, LANES - a.shape[1])))


def kernel(x, norm_mix_g, w_in, b_forget, fox_out_g, moba_out_g, w_out, norm_ffn_g, w_router_group,
           b_router_group, w_router_expert, b_router_expert, w_gate, w_up, w_down, norm_final_g):
    b, s, d = x.shape
    t = b * s
    assert w_in.shape[0] == 1, "the closing RMSNorm is fused into the only layer's combine step"
    cosm, sina, sinb = _rotary_tables(s)
    n_tiles = (2 * t) // TM_EXPERT + N_EXPERTS
    fw3 = 3 * FOX_WIDTH
    m0 = fw3 + N_FOX_HEADS
    wl = w_in[0]
    w_main = jnp.concatenate([wl[:, :2 * FOX_WIDTH], wl[:, m0:m0 + 2 * MOBA_WIDTH]],
                             axis=1).astype(BF16)
    w_vt = jnp.stack([wl[:, 2 * FOX_WIDTH:fw3].T, wl[:, m0 + 2 * MOBA_WIDTH:].T]).astype(BF16)
    aux_head = jnp.arange(AUX_PER_HEAD * N_FOX_HEADS) // AUX_PER_HEAD
    w_logit = _pad_lanes(wl[:, fw3:m0][:, aux_head]).astype(BF16)
    b_logit = _pad_lanes(b_forget[0][None, aux_head])
    fq, fqa, fk, fka, fvt, mq, mk, mvt, kmean = _inproj(
        x, norm_mix_g[0][None, :], w_main, w_vt, w_logit, b_logit, cosm, sina, sinb)
    fox = _fox(fq, fqa, fk, fka, fvt)
    moba = _moba(mq, mk, mvt, kmean)

    w_router = _pad_lanes(jnp.concatenate(
        [w_router_group[0], w_router_expert[0].reshape(d, N_EXPERTS)], axis=1))
    w_router_hi = w_router.astype(BF16)
    w_router = jnp.concatenate([w_router_hi, (w_router - w_router_hi.astype(F32)).astype(BF16)], axis=1)
    b_router =_pad_lanes(jnp.concatenate(
        [b_router_group[0], b_router_expert[0].reshape(N_EXPERTS)])[None, :])
    x2, h2, wts, route, counts = _postattn(
        x.reshape(t, d), fox.reshape(t, FOX_WIDTH), moba.reshape(t, MOBA_WIDTH),
        fox_out_g[0][None, :], moba_out_g[0][None, :], w_out[0].astype(BF16),
        norm_ffn_g[0][None, :], w_router, b_router)

    counts = counts[0, :N_EXPERTS]
    padded = (counts + TM_EXPERT - 1) // TM_EXPERT * TM_EXPERT
    ends = jnp.cumsum(padded)
    starts = ends - padded
    expert_ids = jnp.arange(N_EXPERTS, dtype=I32)[:, None, None]
    dest = route[2:4] + jnp.sum(jnp.where(route[None, 0:2] == expert_ids, starts[:, None, None], 0),
                                axis=0)
    dest3 = dest.reshape(2, t // TM_ROWS, TM_ROWS).transpose(1, 0, 2).reshape(t // TM_ROWS, 1, 2 * TM_ROWS)
    n_rows = n_tiles * TM_EXPERT
    pad_start = jnp.concatenate([starts + counts, ends[-1:]]).astype(I32)
    pad_len = jnp.concatenate([padded - counts, n_rows - ends[-1:]]).astype(I32)
    n_valid = ends[-1] // TM_EXPERT
    tile_src = jnp.minimum(jnp.arange(n_tiles, dtype=I32), n_valid - 1)
    tile_expert = jnp.sum(ends[None, :] <= (tile_src * TM_EXPERT)[:, None], axis=1).astype(I32)
    xs = _dispatch(pad_start, pad_len, dest3, h2, n_rows)
    ys = _experts(tile_expert, tile_src, n_valid.reshape(1).astype(I32), xs,
                  w_gate[0], w_up[0], w_down[0])
    return _combine(dest3, x2, wts, norm_final_g[None, :], ys).reshape(b, s, d)
```

```python
import math

import jax
import jax.numpy as jnp
from jax import lax
from jax.experimental import pallas as pl
from jax.experimental.pallas import tpu as pltpu

F32 = jnp.float32
BF16 = jnp.bfloat16
I32 = jnp.int32

HEAD_DIM = 64
N_FOX_HEADS = 8
N_MOBA_HEADS = 8
FOX_WIDTH = N_FOX_HEADS * HEAD_DIM
MOBA_WIDTH = N_MOBA_HEADS * HEAD_DIM
MOBA_BLOCK = 256
MOBA_TOPK = 3
ROPE_THETA = 500000.0
ROPE_DIM = HEAD_DIM // 4
N_GROUPS = 4
EXPERTS_PER_GROUP = 8
N_EXPERTS = N_GROUPS * EXPERTS_PER_GROUP
EPS = 1e-6

LANES = 128
SUBLANES = 8
BF16_SUBLANES = 16
LOG2_E = math.log2(math.e)
HEADS_PER_BLOCK = LANES // HEAD_DIM
BLOCKS_PER_STEP = 4
HEADS_PER_STEP = HEADS_PER_BLOCK * BLOCKS_PER_STEP
STEP_LANES = LANES * BLOCKS_PER_STEP
VMEM_LIMIT = 56 * 1024 * 1024
AUX_PER_HEAD = 6

TM_PROJ = 512
TQ = 256
TM_EXPERT = 512
TM_ROWS = 256
ISSUE_UNROLL = 8
ROUTE_ROWS = 64
TRIP_TILES = 4

NEG_INF = float("-inf")
MASKED = -1e30


def _params(sem):
    return pltpu.CompilerParams(dimension_semantics=sem, vmem_limit_bytes=VMEM_LIMIT)


def _rms(x, g):
    return x * lax.rsqrt(jnp.mean(x * x, axis=-1, keepdims=True) + EPS) * g


def _split3(x):
    hi = x.astype(BF16)
    r = x - hi.astype(F32)
    mid = r.astype(BF16)
    lo = (r - mid.astype(F32)).astype(BF16)
    return hi, mid, lo


def _dot(a, b):
    return jnp.dot(a, b, preferred_element_type=F32)


def _dot_nt(a, b):
    return lax.dot_general(a, b, (((1,), (1,)), ((), ())), preferred_element_type=F32)


def _inproj_kernel(x_ref, g_ref, w_ref, wvt_ref, bf_ref, kind_ref, cosm_ref, sina_ref, sinb_ref,
                   fq_ref, fqa_ref, fk_ref, fka_ref, fvt_ref, mq_ref, mk_ref, mvt_ref, kmean_ref,
                   carry_ref):
    j = pl.program_id(1)
    tm = x_ref.shape[1]
    tk = fvt_ref.shape[3]
    h = _rms(x_ref[0], g_ref[...]).astype(BF16)
    scale = HEAD_DIM ** -0.5 * LOG2_E

    def proj(seg):
        return _dot(h, w_ref[:, seg * FOX_WIDTH:(seg + 1) * FOX_WIDTH])

    fq_ref[0] = (proj(0) * scale).astype(BF16)
    fk_ref[0] = proj(1).astype(BF16)

    for vt_ref, seg in ((fvt_ref, 0), (mvt_ref, 1)):
        vt = _dot_nt(wvt_ref[seg], h).astype(BF16)
        for r in range(tm // tk):
            vt_ref[0, r] = vt[:, r * tk:(r + 1) * tk]

    cosm, sina, sinb = cosm_ref[...], sina_ref[...], sinb_ref[...]

    def rotary(t):
        outs = []
        for g in range(MOBA_WIDTH // LANES):
            tg = t[:, g * LANES:(g + 1) * LANES]
            outs.append(tg * cosm + pltpu.roll(tg, LANES - ROPE_DIM // 2, 1) * sina
                        + pltpu.roll(tg, ROPE_DIM // 2, 1) * sinb)
        return jnp.concatenate(outs, axis=1)

    mq_ref[0] = (rotary(proj(2)) * scale).astype(BF16)
    mk_z = _dot(h, w_ref[:, 3 * FOX_WIDTH:])
    mk = rotary(mk_z[:, :MOBA_WIDTH])
    mk_ref[0] = mk.astype(BF16)
    nblk_tile = tm // MOBA_BLOCK
    means = [jnp.mean(mk[r * MOBA_BLOCK:(r + 1) * MOBA_BLOCK], axis=0, keepdims=True)
             for r in range(nblk_tile)]
    means += [jnp.zeros_like(means[0])] * (kmean_ref.shape[2] - nblk_tile)
    kmean_ref[0, 0] = jnp.concatenate(means, axis=0)

    z = mk_z[:, MOBA_WIDTH:] + bf_ref[...]
    log_f = jnp.minimum(z, 0.0) - jnp.log1p(jnp.exp(-jnp.abs(z)))

    @pl.when(j == 0)
    def _():
        carry_ref[...] = jnp.zeros_like(carry_ref)

    half = tm // 2
    row = lax.broadcasted_iota(I32, (half, half), 0)
    col = lax.broadcasted_iota(I32, (half, half), 1)
    tri = jnp.where(row >= col, 1.0, 0.0).astype(BF16)
    pieces = jnp.concatenate(_split3(log_f), axis=1)
    carry = carry_ref[...]
    cs = []
    for r in range(2):
        local = _dot(tri, pieces[r * half:(r + 1) * half])
        cs.append(local[:, :LANES] + local[:, LANES:2 * LANES] + local[:, 2 * LANES:] + carry)
        carry = cs[-1][half - 1:half, :]
    carry_ref[...] = carry
    c = jnp.concatenate(cs, axis=0) * LOG2_E

    hi = c.astype(BF16).astype(F32)
    mid = (c - hi).astype(BF16).astype(F32)
    lo = c - hi - mid
    kind = kind_ref[...]
    one = jnp.where(kind < AUX_PER_HEAD, 1.0, 0.0)
    pick = lambda base: jnp.where(kind == base, hi, jnp.where(kind == base + 1, mid,
                                  jnp.where(kind == base + 2, lo, 0.0)))
    fqa_ref[0] = (pick(3) + jnp.where(kind < 3, one, 0.0)).astype(BF16)
    fka_ref[0] = (jnp.where(kind >= 3, one, 0.0) - pick(0)).astype(BF16)


def _inproj(x, g, w_main, w_vt, b_logit, cosm, sina, sinb):
    b, s, d = x.shape
    tm, tk = TM_PROJ, TQ
    lane = jnp.arange(LANES, dtype=I32)
    kind = jnp.where(lane < AUX_PER_HEAD * N_FOX_HEADS, lane % AUX_PER_HEAD, AUX_PER_HEAD)[None, :]
    act = jax.ShapeDtypeStruct((b, s, FOX_WIDTH), BF16)
    aux = jax.ShapeDtypeStruct((b, s, LANES), BF16)
    vt = jax.ShapeDtypeStruct((b, s // tk, FOX_WIDTH, tk), BF16)
    out_shape = [act, aux, act, aux, vt, act, act, vt,
                 jax.ShapeDtypeStruct((b, s // tm, SUBLANES, MOBA_WIDTH), F32)]
    act_spec = pl.BlockSpec((1, tm, FOX_WIDTH), lambda bi, j: (bi, j, 0))
    aux_spec = pl.BlockSpec((1, tm, LANES), lambda bi, j: (bi, j, 0))
    vt_spec = pl.BlockSpec((1, tm // tk, FOX_WIDTH, tk), lambda bi, j: (bi, j, 0, 0))
    tab_spec = pl.BlockSpec((tm, LANES), lambda bi, j: (j, 0))
    const2 = lambda bi, j: (0, 0)
    const3 = lambda bi, j: (0, 0, 0)
    *acts, kmean = pl.pallas_call(
        _inproj_kernel,
        grid=(b, s // tm),
        in_specs=[pl.BlockSpec((1, tm, d), lambda bi, j: (bi, j, 0)),
                  pl.BlockSpec((1, d), const2),
                  pl.BlockSpec(w_main.shape, const2),
                  pl.BlockSpec(w_vt.shape, const3),
                  pl.BlockSpec((1, LANES), const2), pl.BlockSpec((1, LANES), const2),
                  tab_spec, tab_spec, tab_spec],
        out_specs=[act_spec, aux_spec, act_spec, aux_spec, vt_spec, act_spec, act_spec, vt_spec,
                   pl.BlockSpec((1, 1, SUBLANES, MOBA_WIDTH), lambda bi, j: (bi, j, 0, 0))],
        out_shape=out_shape,
        scratch_shapes=[pltpu.VMEM((1, LANES), F32)],
        compiler_params=_params(("arbitrary", "arbitrary")),
        name="inproj",
    )(x, g, w_main, w_vt, b_logit, kind, cosm, sina, sinb)
    kmean = kmean[:, :, :tm // MOBA_BLOCK].reshape(b, s // MOBA_BLOCK, MOBA_WIDTH)
    return (*acts, kmean)


def _softmax(heads, scores, m_ref):
    stats = []
    for hh, s in zip(heads, scores):
        m_prev = m_ref[hh]
        m_new = jnp.maximum(m_prev, jnp.max(s, axis=0, keepdims=True))
        m_ref[hh] = m_new
        stats.append((jnp.exp2(m_prev - m_new), m_new))
    return [(alpha, jnp.exp2(s - m_new).astype(BF16)) for (alpha, m_new), s in zip(stats, scores)]


def _values(weighted, vt, l_ref, acc_ref):
    ones = jnp.ones((BF16_SUBLANES, vt.shape[1]), BF16)
    for hh, (alpha, p) in enumerate(weighted):
        rows = slice(hh * HEAD_DIM, (hh + 1) * HEAD_DIM)
        pv = _dot(jnp.concatenate([vt[rows, :], ones], axis=0), p)
        acc_ref[rows, :] = alpha * acc_ref[rows, :] + pv[:HEAD_DIM]
        l_ref[hh] = alpha * l_ref[hh] + pv[HEAD_DIM:HEAD_DIM + 1]


def _attend_tiles(i, scores_of, vt_ref, m_ref, l_ref, acc_ref):
    heads = tuple(range(HEADS_PER_STEP))

    def block(tiles):
        scores = [scores_of(kt, diag, heads) for kt, diag in tiles]
        for (kt, _), s in zip(tiles, scores):
            _values(_softmax(heads, s, m_ref), vt_ref[0, kt], l_ref, acc_ref)

    @pl.when(i % 2 == 1)
    def _():
        block([(i, True), (i - 1, False)])

    @pl.when(i % 2 == 0)
    def _():
        block([(i, True)])

    paired = i - i % 2
    big = TRIP_TILES

    def trip(j, c):
        block([(big * j + r, False) for r in range(big)])
        return c

    lax.fori_loop(0, paired // big, trip, 0)
    done = paired - paired % big
    size = big // 2
    while size >= 2:
        @pl.when(paired % (2 * size) >= size)
        def _(size=size, done=done):
            block([(done + r, False) for r in range(size)])
        done = done + jnp.where(paired % (2 * size) >= size, size, 0)
        size //= 2


def _attn_init(m_ref, l_ref, acc_ref):
    m_ref[...] = jnp.full(m_ref.shape, NEG_INF, F32)
    l_ref[...] = jnp.zeros_like(l_ref)
    acc_ref[...] = jnp.zeros_like(acc_ref)


def _attn_finish(o_ref, l_ref, acc_ref):
    out_t = jnp.concatenate(
        [acc_ref[hh * HEAD_DIM:(hh + 1) * HEAD_DIM, :] / l_ref[hh] for hh in range(HEADS_PER_STEP)],
        axis=0)
    o_ref[0] = out_t.T


def _block(a, g):
    return a[:, g * LANES:(g + 1) * LANES]


def _per_head(a, width):
    first = lax.broadcasted_iota(I32, (a.shape[0], LANES), 1) < width
    zero = jnp.zeros((a.shape[0], LANES), a.dtype)
    out = []
    for g in range(BLOCKS_PER_STEP):
        blk = _block(a, g)
        out += [jnp.where(first, blk, zero), jnp.where(first, zero, blk)]
    return out


def _key_le_query(tq):
    return lax.broadcasted_iota(I32, (tq, tq), 0) <= lax.broadcasted_iota(I32, (tq, tq), 1)


def _fox_kernel(q_ref, qa_ref, k_ref, ka_ref, vt_ref, o_ref, m_ref, l_ref, acc_ref):
    i = pl.program_id(2)
    tq = q_ref.shape[1]
    qa = qa_ref[0]
    lane = lax.broadcasted_iota(I32, qa.shape, 1) - pl.program_id(1) * (HEADS_PER_STEP * AUX_PER_HEAD)
    own_aux = lambda hh: jnp.logical_and(lane >= hh * AUX_PER_HEAD, lane < (hh + 1) * AUX_PER_HEAD)
    qq = [jnp.concatenate([qm, jnp.where(own_aux(hh), qa, jnp.zeros_like(qa))], axis=1)
          for hh, qm in enumerate(_per_head(q_ref[0], HEAD_DIM))]
    causal = _key_le_query(tq)
    _attn_init(m_ref, l_ref, acc_ref)

    def scores_of(kt, diag, heads):
        ks = pl.multiple_of(kt * tq, tq)
        ka = ka_ref[0, pl.ds(ks, tq), :]
        kk = {g: jnp.concatenate([k_ref[0, pl.ds(ks, tq), g * LANES:(g + 1) * LANES], ka], axis=1)
              for g in sorted({hh // HEADS_PER_BLOCK for hh in heads})}
        scores = [_dot_nt(kk[hh // HEADS_PER_BLOCK], qq[hh]) for hh in heads]
        if diag:
            scores = [jnp.where(causal, s, NEG_INF) for s in scores]
        return tuple(scores)

    _attend_tiles(i, scores_of, vt_ref, m_ref, l_ref, acc_ref)
    _attn_finish(o_ref, l_ref, acc_ref)


def _moba_kernel(q_ref, k_ref, hot_ref, vt_ref, kmean_ref, o_ref, m_ref, l_ref, acc_ref):
    i = pl.program_id(2)
    tq = q_ref.shape[1]
    nblk = kmean_ref.shape[1]
    qs = _per_head(q_ref[0], HEAD_DIM)
    causal = _key_le_query(tq)
    _attn_init(m_ref, l_ref, acc_ref)

    km_parts = jnp.concatenate(_split3(kmean_ref[0]), axis=0)
    blk = lax.broadcasted_iota(I32, (nblk, tq), 0).astype(F32)
    past = blk < i.astype(F32)
    masks = []
    for hh in range(HEADS_PER_STEP):
        pieces = _dot_nt(_block(km_parts, hh // HEADS_PER_BLOCK), qs[hh])
        gate = pieces[:nblk] + pieces[nblk:2 * nblk] + pieces[2 * nblk:]
        sel = jnp.zeros((nblk, tq), jnp.bool_)
        for _ in range(MOBA_TOPK):
            remaining = jnp.logical_and(past, jnp.logical_not(sel))
            g = jnp.where(remaining, gate, NEG_INF)
            first = jnp.min(jnp.where(g == jnp.max(g, axis=0, keepdims=True), blk, float(nblk)),
                            axis=0, keepdims=True)
            sel = jnp.logical_or(sel, jnp.logical_and(blk == first, remaining))
        masks.append(jnp.where(sel, 0.0, MASKED))
    unused = LANES - HEADS_PER_STEP * nblk
    masks += [jnp.zeros((unused, tq), F32)] if unused else []
    qa = jnp.concatenate(masks, axis=0).T.astype(BF16)
    lane = lax.broadcasted_iota(I32, qa.shape, 1)
    own = lambda hh: jnp.logical_and(lane >= hh * nblk, lane < (hh + 1) * nblk)
    qq = [jnp.concatenate([qs[hh], jnp.where(own(hh), qa, jnp.zeros_like(qa))], axis=1)
          for hh in range(HEADS_PER_STEP)]

    def scores_of(kt, diag, heads):
        ks = pl.multiple_of(kt * tq, tq)
        k = lambda hh: k_ref[0, pl.ds(ks, tq), (hh // HEADS_PER_BLOCK) * LANES:(hh // HEADS_PER_BLOCK + 1) * LANES]
        if diag:
            return tuple(jnp.where(causal, _dot_nt(k(hh), qs[hh]), NEG_INF) for hh in heads)
        hot = hot_ref[pl.ds(ks, tq), :]
        return tuple(_dot_nt(jnp.concatenate([k(hh), hot], axis=1), qq[hh]) for hh in heads)

    _attend_tiles(i, scores_of, vt_ref, m_ref, l_ref, acc_ref)
    _attn_finish(o_ref, l_ref, acc_ref)


def _attn_scratch(tq):
    return [pltpu.VMEM((HEADS_PER_STEP, 1, tq), F32), pltpu.VMEM((HEADS_PER_STEP, 1, tq), F32),
            pltpu.VMEM((STEP_LANES, tq), F32)]


def _attn_specs(s, tq):
    q_spec = pl.BlockSpec((1, tq, STEP_LANES), lambda bi, hb, i: (bi, i, hb))
    k_spec = pl.BlockSpec((1, s, STEP_LANES), lambda bi, hb, i: (bi, 0, hb))
    vt_spec = pl.BlockSpec((1, s // tq, STEP_LANES, tq), lambda bi, hb, i: (bi, 0, hb, 0))
    return q_spec, k_spec, vt_spec


def _fox(q, qa, k, ka, vt):
    b, s, width = q.shape
    tq = TQ
    q_spec, k_spec, vt_spec = _attn_specs(s, tq)
    return pl.pallas_call(
        _fox_kernel,
        grid=(b, width // STEP_LANES, s // tq),
        in_specs=[q_spec, pl.BlockSpec((1, tq, LANES), lambda bi, hb, i: (bi, i, 0)),
                  k_spec, pl.BlockSpec((1, s, LANES), lambda bi, hb, i: (bi, 0, 0)), vt_spec],
        out_specs=q_spec,
        out_shape=jax.ShapeDtypeStruct((b, s, width), F32),
        scratch_shapes=_attn_scratch(tq),
        compiler_params=_params(("arbitrary", "arbitrary", "arbitrary")),
        name="fox",
    )(q, qa, k, ka, vt)


def _moba(q, k, vt, kmean):
    b, s, width = q.shape
    tq = TQ
    nblk = kmean.shape[1]
    assert HEADS_PER_STEP * nblk <= LANES, "one aux lane per (head, key block)"
    lane = jnp.arange(LANES)
    hot = jnp.logical_and(lane[None, :] < HEADS_PER_STEP * nblk,
                          lane[None, :] % nblk == jnp.arange(s)[:, None] // MOBA_BLOCK).astype(BF16)
    q_spec, k_spec, vt_spec = _attn_specs(s, tq)
    return pl.pallas_call(
        _moba_kernel,
        grid=(b, width // STEP_LANES, s // tq),
        in_specs=[q_spec, k_spec, pl.BlockSpec((s, LANES), lambda bi, hb, i: (0, 0)), vt_spec,
                  pl.BlockSpec((1, nblk, STEP_LANES), lambda bi, hb, i: (bi, 0, hb))],
        out_specs=q_spec,
        out_shape=jax.ShapeDtypeStruct((b, s, width), F32),
        scratch_shapes=_attn_scratch(tq),
        compiler_params=_params(("arbitrary", "arbitrary", "arbitrary")),
        name="moba",
    )(q, k, hot, vt, kmean)


def _postattn_kernel(x_ref, fox_ref, moba_ref, gf_ref, gm_ref, wo_ref, gn_ref, wr_ref, br_ref,
                     x2_ref, h2_ref, wts_ref, route_ref, cnt_ref, carry_ref):
    t = pl.program_id(0)
    tm = x_ref.shape[0]
    fw = fox_ref.shape[1]
    mixed_f = _rms(fox_ref[...], gf_ref[...]).astype(BF16)
    mixed_m = _rms(moba_ref[...], gm_ref[...]).astype(BF16)
    x2 = x_ref[...] + _dot(mixed_f, wo_ref[:fw, :]) + _dot(mixed_m, wo_ref[fw:, :])
    x2_ref[...] = x2
    h2 = _rms(x2, gn_ref[...])
    _store_rows(h2_ref, h2)

    h_hi = h2.astype(BF16)
    h_lo = (h2 - h_hi.astype(F32)).astype(BF16)
    by_hi = _dot_nt(wr_ref[...], h_hi)
    logits = (by_hi[:ROUTE_ROWS] + by_hi[LANES:LANES + ROUTE_ROWS]
              + _dot_nt(wr_ref[:ROUTE_ROWS, :], h_lo) + br_ref[...])
    row = lax.broadcasted_iota(I32, (ROUTE_ROWS, tm), 0).astype(F32)

    def first_max(vals):
        mx = jnp.max(vals, axis=0, keepdims=True)
        return mx, jnp.min(jnp.where(vals == mx, row, float(ROUTE_ROWS)), axis=0, keepdims=True)

    gl = jnp.where(row < N_GROUPS, logits, NEG_INF)
    gmax, g_idx = first_max(gl)
    g_top = 1.0 / jnp.sum(jnp.exp(gl - gmax), axis=0, keepdims=True)
    e_lo = N_GROUPS + EXPERTS_PER_GROUP * g_idx
    el = jnp.where(jnp.logical_and(row >= e_lo, row < e_lo + EXPERTS_PER_GROUP), logits, NEG_INF)
    emax, i1 = first_max(el)
    esum = jnp.sum(jnp.exp(el - emax), axis=0, keepdims=True)
    e2max, i2 = first_max(jnp.where(row == i1, NEG_INF, el))
    p1 = 1.0 / esum
    p2 = jnp.exp(e2max - emax) / esum
    w1 = p1 / (p1 + p2) * g_top
    w2 = p2 / (p1 + p2) * g_top
    e1 = i1 - N_GROUPS
    e2 = i2 - N_GROUPS
    zeros = jnp.zeros((SUBLANES - 2, tm), F32)
    wts_ref[...] = jnp.concatenate([w1, w2, zeros], axis=0).T[:, :2]

    @pl.when(t == 0)
    def _():
        carry_ref[...] = jnp.zeros_like(carry_ref)

    earlier = jnp.where(lax.broadcasted_iota(I32, (tm, tm), 0) < lax.broadcasted_iota(I32, (tm, tm), 1),
                        1.0, 0.0).astype(BF16)
    hit1 = row == e1
    hit2 = row == e2
    oh1 = jnp.where(hit1, 1.0, 0.0)
    oh2 = jnp.where(hit2, 1.0, 0.0)
    tot1 = jnp.sum(oh1, axis=1, keepdims=True)
    tot2 = jnp.sum(oh2, axis=1, keepdims=True)
    base = carry_ref[...]
    before = _dot(jnp.concatenate([oh1, oh2], axis=0).astype(BF16), earlier)
    before1 = before[:ROUTE_ROWS] + base
    before2 = before[ROUTE_ROWS:] + (base + tot1)
    r1 = jnp.sum(jnp.where(hit1, before1, 0.0), axis=0, keepdims=True)
    r2 = jnp.sum(jnp.where(hit2, before2, 0.0), axis=0, keepdims=True)
    route_ref[...] = jnp.concatenate([e1, e2, r1, r2, zeros[:SUBLANES - 4]], axis=0).astype(I32)
    total = base + tot1 + tot2
    carry_ref[...] = total
    cnt_ref[...] = total.astype(I32)


def _postattn(x, fox, moba, gf, gm, wo, gn, wr, br):
    t, d = x.shape
    tm = TM_PROJ
    fw = fox.shape[1]
    const = lambda i: (0, 0)
    rows = lambda i: (i, 0)
    return pl.pallas_call(
        _postattn_kernel,
        grid=(t // tm,),
        in_specs=[pl.BlockSpec((tm, d), rows), pl.BlockSpec((tm, fw), rows),
                  pl.BlockSpec((tm, moba.shape[1]), rows),
                  pl.BlockSpec((1, fw), const), pl.BlockSpec((1, moba.shape[1]), const),
                  pl.BlockSpec(wo.shape, const), pl.BlockSpec((1, d), const),
                  pl.BlockSpec(wr.shape, const), pl.BlockSpec((ROUTE_ROWS, 1), const)],
        out_specs=[pl.BlockSpec((tm, d), rows), pl.BlockSpec((tm * SUBLANES, LANES), rows),
                   pl.BlockSpec((tm, 2), rows), pl.BlockSpec((SUBLANES, tm), lambda i: (0, i)),
                   pl.BlockSpec((ROUTE_ROWS, 1), const)],
        out_shape=[jax.ShapeDtypeStruct((t, d), F32), jax.ShapeDtypeStruct((t * SUBLANES, LANES), F32),
                   jax.ShapeDtypeStruct((t, 2), F32), jax.ShapeDtypeStruct((SUBLANES, t), I32),
                   jax.ShapeDtypeStruct((ROUTE_ROWS, 1), I32)],
        scratch_shapes=[pltpu.VMEM((ROUTE_ROWS, 1), F32)],
        compiler_params=_params(("arbitrary",)),
        name="postattn",
    )(x, fox, moba, gf, gm, wo, gn, wr, br)


def _store_rows(ref, val):
    for g in range(SUBLANES):
        ref[pl.ds(g, val.shape[0], stride=SUBLANES), :] = val[:, g * LANES:(g + 1) * LANES]


def _load_rows(ref):
    tokens = ref.shape[0] // SUBLANES
    return jnp.concatenate([ref[pl.ds(g, tokens, stride=SUBLANES), :] for g in range(SUBLANES)], axis=1)


def _row_copy(src, src_row, dst, dst_row, sem, tokens=1):
    window = lambda r: pl.ds(pl.multiple_of(r * SUBLANES, SUBLANES), tokens * SUBLANES)
    return pltpu.make_async_copy(src.at[window(src_row)], dst.at[window(dst_row)], sem)


RING = 3
ZERO_TOKENS = 256


def _dispatch_kernel(pad_start_ref, pad_len_ref, dest_ref, h_ref, xs_ref,
                     ring_ref, zero_ref, fetch_sems, scatter_sems, pad_sem):
    i = pl.program_id(0)
    last = pl.num_programs(0) - 1
    tm = dest_ref.shape[2] // 2
    tile_rows = tm * SUBLANES
    row_copy = _row_copy
    zero_tokens = zero_ref.shape[0] // SUBLANES

    def fetch(tile):
        start = pl.multiple_of(tile * tile_rows, tile_rows)
        slot = lax.rem(tile, RING)
        return pltpu.make_async_copy(h_ref.at[pl.ds(start, tile_rows)], ring_ref.at[slot],
                                     fetch_sems.at[slot])

    def zero_fill(e, act):
        start, n = pad_start_ref[e], pad_len_ref[e]
        whole = lax.shift_right_logical(n, zero_tokens.bit_length() - 1)
        rest = jnp.bitwise_and(n, zero_tokens - 1)

        def chunk(c, carry):
            act(_row_copy(zero_ref, 0, xs_ref, start + c * zero_tokens, pad_sem, tokens=zero_tokens))
            return carry

        lax.fori_loop(0, whole, chunk, 0)
        bit = zero_tokens // 2
        while bit:
            @pl.when(jnp.bitwise_and(rest, bit) != 0)
            def _(bit=bit):
                above = jnp.bitwise_and(rest, -2 * bit)
                act(_row_copy(zero_ref, 0, xs_ref, start + whole * zero_tokens + above, pad_sem, tokens=bit))
            bit //= 2

    @pl.when(i == 0)
    def _():
        fetch(0).start()
        zero_ref[...] = jnp.zeros_like(zero_ref)
        n_regions = pad_start_ref.shape[0]
        lax.fori_loop(0, n_regions, lambda e, c: (zero_fill(e, lambda cp: cp.start()), c)[1], 0)
        lax.fori_loop(0, n_regions, lambda e, c: (zero_fill(e, lambda cp: cp.wait()), c)[1], 0)

    @pl.when(i < last)
    def _():
        fetch(i + 1).start()

    fetch(i).wait()
    src = ring_ref.at[lax.rem(i, RING)]

    def issue(r, c):
        for k in range(2):
            row_copy(src, r, xs_ref, dest_ref[0, 0, k * tm + r], scatter_sems.at[i % 2]).start(priority=k)
        return c

    lax.fori_loop(0, tm, issue, 0, unroll=ISSUE_UNROLL)

    def drain(parity):
        def one(r, c):
            row_copy(src, 0, xs_ref, 0, scatter_sems.at[parity]).wait()
            return c

        lax.fori_loop(0, 2 * tm, one, 0, unroll=ISSUE_UNROLL)

    @pl.when(i > 0)
    def _():
        drain((i - 1) % 2)

    @pl.when(i == last)
    def _():
        drain(i % 2)


def _dispatch(pad_start, pad_len, dest3, h2, n_rows):
    tm = dest3.shape[2] // 2
    grid_spec = pltpu.PrefetchScalarGridSpec(
        num_scalar_prefetch=2,
        grid=(dest3.shape[0],),
        in_specs=[pl.BlockSpec((1, 1, 2 * tm), lambda i, ps, pn: (i, 0, 0), memory_space=pltpu.SMEM),
                  pl.BlockSpec(memory_space=pl.ANY)],
        out_specs=pl.BlockSpec(memory_space=pl.ANY),
        scratch_shapes=[pltpu.VMEM((RING, tm * SUBLANES, LANES), F32),
                        pltpu.VMEM((ZERO_TOKENS * SUBLANES, LANES), F32),
                        pltpu.SemaphoreType.DMA((RING,)), pltpu.SemaphoreType.DMA((2,)),
                        pltpu.SemaphoreType.DMA(())],
    )
    return pl.pallas_call(
        _dispatch_kernel,
        grid_spec=grid_spec,
        out_shape=jax.ShapeDtypeStruct((n_rows * SUBLANES, LANES), F32),
        compiler_params=_params(("arbitrary",)),
        name="dispatch",
    )(pad_start, pad_len, dest3, h2)


def _experts_kernel(te_ref, ts_ref, nv_ref, xs_ref, wg_ref, wu_ref, wd_ref, ys_ref,
                    wgb_ref, wub_ref, wdb_ref):
    del ts_ref
    t = pl.program_id(0)

    @pl.when(jnp.logical_or(t == 0, te_ref[t] != te_ref[jnp.maximum(t - 1, 0)]))
    def _():
        wgb_ref[...] = wg_ref[0].astype(BF16)
        wub_ref[...] = wu_ref[0].astype(BF16)
        wdb_ref[...] = wd_ref[0].astype(BF16)

    @pl.when(t < nv_ref[0])
    def _():
        xb = _load_rows(xs_ref).astype(BF16)
        a = _dot(xb, wgb_ref[...])
        u = _dot(xb, wub_ref[...])
        act = (a * jax.nn.sigmoid(a) * u).astype(BF16)
        _store_rows(ys_ref, _dot(act, wdb_ref[...]))

    @pl.when(t >= nv_ref[0])
    def _():
        ys_ref[...] = jnp.zeros_like(ys_ref)


def _experts(tile_expert, tile_src, n_valid, xs, wg, wu, wd):
    tm = TM_EXPERT
    n_tiles = xs.shape[0] // (tm * SUBLANES)
    _, d, f = wg.shape
    row_block = (tm * SUBLANES, LANES)
    grid_spec = pltpu.PrefetchScalarGridSpec(
        num_scalar_prefetch=3,
        grid=(n_tiles,),
        in_specs=[pl.BlockSpec(row_block, lambda t, te, ts, nv: (ts[t], 0)),
                  pl.BlockSpec((1, d, f), lambda t, te, ts, nv: (te[t], 0, 0)),
                  pl.BlockSpec((1, d, f), lambda t, te, ts, nv: (te[t], 0, 0)),
                  pl.BlockSpec((1, f, d), lambda t, te, ts, nv: (te[t], 0, 0))],
        out_specs=pl.BlockSpec(row_block, lambda t, te, ts, nv: (t, 0)),
        scratch_shapes=[pltpu.VMEM((d, f), BF16), pltpu.VMEM((d, f), BF16), pltpu.VMEM((f, d), BF16)],
    )
    return pl.pallas_call(
        _experts_kernel,
        grid_spec=grid_spec,
        out_shape=jax.ShapeDtypeStruct((n_tiles * tm * SUBLANES, LANES), F32),
        compiler_params=_params(("arbitrary",)),
        name="experts",
    )(tile_expert, tile_src, n_valid, xs, wg, wu, wd)


def _combine_kernel(dest_ref, next_ref, x2_ref, wts_ref, g_ref, ys_ref, o_ref, buf_ref, sems):
    i = pl.program_id(0)
    tm = dest_ref.shape[2] // 2
    slot = i % 2

    def gather(d_ref, to):
        def issue(r, c):
            for k in range(2):
                _row_copy(ys_ref, d_ref[0, 0, k * tm + r], buf_ref.at[to, k], r,
                          sems.at[to]).start(priority=k)
            return c

        lax.fori_loop(0, tm, issue, 0, unroll=ISSUE_UNROLL)

    @pl.when(i == 0)
    def _():
        gather(dest_ref, 0)

    @pl.when(i + 1 < pl.num_programs(0))
    def _():
        gather(next_ref, 1 - slot)

    def drain(r, c):
        _row_copy(ys_ref, 0, buf_ref.at[slot, 0], 0, sems.at[slot]).wait()
        return c

    lax.fori_loop(0, 2 * tm, drain, 0, unroll=ISSUE_UNROLL)
    w = wts_ref[...]
    y = (x2_ref[...] + w[:, 0:1] * _load_rows(buf_ref.at[slot, 0])
         + w[:, 1:2] * _load_rows(buf_ref.at[slot, 1]))
    o_ref[...] = _rms(y, g_ref[...])


def _combine(dest3, x2, wts, g, ys):
    t, d = x2.shape
    tm = dest3.shape[2] // 2
    rows = lambda i: (i, 0)
    n = t // tm
    return pl.pallas_call(
        _combine_kernel,
        grid=(n,),
        in_specs=[pl.BlockSpec((1, 1, 2 * tm), lambda i: (i, 0, 0), memory_space=pltpu.SMEM),
                  pl.BlockSpec((1, 1, 2 * tm), lambda i: (jnp.minimum(i + 1, n - 1), 0, 0),
                               memory_space=pltpu.SMEM),
                  pl.BlockSpec((tm, d), rows), pl.BlockSpec((tm, 2), rows),
                  pl.BlockSpec((1, d), lambda i: (0, 0)),
                  pl.BlockSpec(memory_space=pl.ANY)],
        out_specs=pl.BlockSpec((tm, d), rows),
        out_shape=jax.ShapeDtypeStruct((t, d), F32),
        scratch_shapes=[pltpu.VMEM((2, 2, tm * SUBLANES, LANES), F32), pltpu.SemaphoreType.DMA((2,))],
        compiler_params=_params(("arbitrary",)),
        name="combine",
    )(dest3, dest3, x2, wts, g, ys)


def _rotary_tables(seq):
    half = ROPE_DIM // 2
    inv_freq = ROPE_THETA ** (-jnp.arange(half, dtype=F32) / half)
    ang = jnp.arange(seq, dtype=F32)[:, None] * inv_freq[None, :]
    cos, sin = jnp.cos(ang), jnp.sin(ang)
    ones = jnp.ones((seq, HEAD_DIM - ROPE_DIM), F32)
    zeros = jnp.zeros((seq, HEAD_DIM - ROPE_DIM), F32)
    zh = jnp.zeros((seq, half), F32)
    cosm = jnp.concatenate([cos, cos, ones], axis=1)
    sina = jnp.concatenate([-sin, zh, zeros], axis=1)
    sinb = jnp.concatenate([zh, sin, zeros], axis=1)
    tile = lambda a: jnp.tile(a, (1, HEADS_PER_BLOCK))
    return tile(cosm), tile(sina), tile(sinb)


def _pad_lanes(a):
    return jnp.pad(a, ((0, 0), (0, LANES - a.shape[1])))


def kernel(x, norm_mix_g, w_in, b_forget, fox_out_g, moba_out_g, w_out, norm_ffn_g, w_router_group,
           b_router_group, w_router_expert, b_router_expert, w_gate, w_up, w_down, norm_final_g):
    b, s, d = x.shape
    t = b * s
    assert w_in.shape[0] == 1, "the closing RMSNorm is fused into the only layer's combine step"
    cosm, sina, sinb = _rotary_tables(s)
    n_tiles = (2 * t) // TM_EXPERT + N_EXPERTS
    fw3 = 3 * FOX_WIDTH
    m0 = fw3 + N_FOX_HEADS
    wl = w_in[0]
    aux_head = jnp.arange(AUX_PER_HEAD * N_FOX_HEADS) // AUX_PER_HEAD
    w_main = jnp.concatenate([wl[:, :2 * FOX_WIDTH], wl[:, m0:m0 + 2 * MOBA_WIDTH],
                              _pad_lanes(wl[:, fw3:m0][:, aux_head])],
                             axis=1).astype(BF16)
    w_vt = jnp.stack([wl[:, 2 * FOX_WIDTH:fw3].T, wl[:, m0 + 2 * MOBA_WIDTH:].T]).astype(BF16)
    b_logit = _pad_lanes(b_forget[0][None, aux_head])
    fq, fqa, fk, fka, fvt, mq, mk, mvt, kmean = _inproj(
        x, norm_mix_g[0][None, :], w_main, w_vt, b_logit, cosm, sina, sinb)
    fox = _fox(fq, fqa, fk, fka, fvt)
    moba = _moba(mq, mk, mvt, kmean)

    w_router = _pad_lanes(jnp.concatenate(
        [w_router_group[0], w_router_expert[0].reshape(d, N_EXPERTS)], axis=1))
    w_router_hi = w_router.astype(BF16)
    w_router = jnp.concatenate([w_router_hi, (w_router - w_router_hi.astype(F32)).astype(BF16)], axis=1).T
    b_router = _pad_lanes(jnp.concatenate(
        [b_router_group[0], b_router_expert[0].reshape(N_EXPERTS)])[None, :])[:, :ROUTE_ROWS].T
    x2, h2, wts, route, counts = _postattn(
        x.reshape(t, d), fox.reshape(t, FOX_WIDTH), moba.reshape(t, MOBA_WIDTH),
        fox_out_g[0][None, :], moba_out_g[0][None, :], w_out[0].astype(BF16),
        norm_ffn_g[0][None, :], w_router, b_router)

    counts = counts[:N_EXPERTS, 0]
    padded = (counts + TM_EXPERT - 1) // TM_EXPERT * TM_EXPERT
    ends = jnp.cumsum(padded)
    starts = ends - padded
    expert_ids = jnp.arange(N_EXPERTS, dtype=I32)[:, None, None]
    dest = route[2:4] + jnp.sum(jnp.where(route[None, 0:2] == expert_ids, starts[:, None, None], 0),
                                axis=0)
    dest3 = dest.reshape(2, t // TM_ROWS, TM_ROWS).transpose(1, 0, 2).reshape(t // TM_ROWS, 1, 2 * TM_ROWS)
    n_rows = n_tiles * TM_EXPERT
    pad_start = jnp.concatenate([starts + counts, ends[-1:]]).astype(I32)
    pad_len = jnp.concatenate([padded - counts, n_rows - ends[-1:]]).astype(I32)
    n_valid = ends[-1] // TM_EXPERT
    tile_src = jnp.minimum(jnp.arange(n_tiles, dtype=I32), n_valid - 1)
    tile_expert = jnp.sum(ends[None, :] <= (tile_src * TM_EXPERT)[:, None], axis=1).astype(I32)
    xs = _dispatch(pad_start, pad_len, dest3, h2, n_rows)
    ys = _experts(tile_expert, tile_src, n_valid.reshape(1).astype(I32), xs,
                  w_gate[0], w_up[0], w_down[0])
    return _combine(dest3, x2, wts, norm_final_g[None, :], ys).reshape(b, s, d)
```

```python
import math

import jax
import jax.numpy as jnp
from jax import lax
from jax.experimental import pallas as pl
from jax.experimental.pallas import tpu as pltpu

F32 = jnp.float32
BF16 = jnp.bfloat16
I32 = jnp.int32

HEAD_DIM = 64
N_FOX_HEADS = 8
N_MOBA_HEADS = 8
FOX_WIDTH = N_FOX_HEADS * HEAD_DIM
MOBA_WIDTH = N_MOBA_HEADS * HEAD_DIM
MOBA_BLOCK = 256
MOBA_TOPK = 3
ROPE_THETA = 500000.0
ROPE_DIM = HEAD_DIM // 4
N_GROUPS = 4
EXPERTS_PER_GROUP = 8
N_EXPERTS = N_GROUPS * EXPERTS_PER_GROUP
EPS = 1e-6

LANES = 128
SUBLANES = 8
BF16_SUBLANES = 16
LOG2_E = math.log2(math.e)
HEADS_PER_BLOCK = LANES // HEAD_DIM
BLOCKS_PER_STEP = 4
HEADS_PER_STEP = HEADS_PER_BLOCK * BLOCKS_PER_STEP
STEP_LANES = LANES * BLOCKS_PER_STEP
VMEM_LIMIT = 56 * 1024 * 1024
AUX_PER_HEAD = 6

TM_IN = 1024
TM_PROJ = 512
TQ = 256
TM_EXPERT = 512
TM_ROWS = 512
ISSUE_UNROLL = 8
ROUTE_ROWS = 64
TRIP_TILES = 4

NEG_INF = float("-inf")
MASKED = -1e30


def _params(sem):
    return pltpu.CompilerParams(dimension_semantics=sem, vmem_limit_bytes=VMEM_LIMIT)


def _rms(x, g):
    return x * lax.rsqrt(jnp.mean(x * x, axis=-1, keepdims=True) + EPS) * g


def _split3(x):
    hi = x.astype(BF16)
    r = x - hi.astype(F32)
    mid = r.astype(BF16)
    lo = (r - mid.astype(F32)).astype(BF16)
    return hi, mid, lo


def _dot(a, b):
    return jnp.dot(a, b, preferred_element_type=F32)


def _dot_nt(a, b):
    return lax.dot_general(a, b, (((1,), (1,)), ((), ())), preferred_element_type=F32)


def _inproj_kernel(x_ref, g_ref, w_ref, wvt_ref, bf_ref, kind_ref, cosm_ref, sina_ref, sinb_ref,
                   fq_ref, fqa_ref, fk_ref, fka_ref, fvt_ref, mq_ref, mk_ref, mvt_ref, kmean_ref,
                   carry_ref):
    j = pl.program_id(1)
    tm = x_ref.shape[1]
    tk = fvt_ref.shape[3]
    h = _rms(x_ref[0], g_ref[...]).astype(BF16)
    scale = HEAD_DIM ** -0.5 * LOG2_E

    def proj(seg):
        return _dot(h, w_ref[:, seg * FOX_WIDTH:(seg + 1) * FOX_WIDTH])

    fq_ref[0] = (proj(0) * scale).astype(BF16)
    fk_ref[0] = proj(1).astype(BF16)

    for vt_ref, seg in ((fvt_ref, 0), (mvt_ref, 1)):
        vt = _dot_nt(wvt_ref[seg], h).astype(BF16)
        for r in range(tm // tk):
            vt_ref[0, r] = vt[:, r * tk:(r + 1) * tk]

    cosm, sina, sinb = cosm_ref[...], sina_ref[...], sinb_ref[...]

    def rotary(t):
        outs = []
        for g in range(MOBA_WIDTH // LANES):
            tg = t[:, g * LANES:(g + 1) * LANES]
            outs.append(tg * cosm + pltpu.roll(tg, LANES - ROPE_DIM // 2, 1) * sina
                        + pltpu.roll(tg, ROPE_DIM // 2, 1) * sinb)
        return jnp.concatenate(outs, axis=1)

    mq_ref[0] = (rotary(proj(2)) * scale).astype(BF16)
    mk_z = _dot(h, w_ref[:, 3 * FOX_WIDTH:])
    mk = rotary(mk_z[:, :MOBA_WIDTH])
    mk_ref[0] = mk.astype(BF16)
    nblk_tile = tm // MOBA_BLOCK
    means = [jnp.mean(mk[r * MOBA_BLOCK:(r + 1) * MOBA_BLOCK], axis=0, keepdims=True)
             for r in range(nblk_tile)]
    means += [jnp.zeros_like(means[0])] * (kmean_ref.shape[2] - nblk_tile)
    kmean_ref[0, 0] = jnp.concatenate(means, axis=0)

    z = mk_z[:, MOBA_WIDTH:] + bf_ref[...]
    log_f = jnp.minimum(z, 0.0) - jnp.log1p(jnp.exp(-jnp.abs(z)))

    @pl.when(j == 0)
    def _():
        carry_ref[...] = jnp.zeros_like(carry_ref)

    half = tm // 2
    row = lax.broadcasted_iota(I32, (half, half), 0)
    col = lax.broadcasted_iota(I32, (half, half), 1)
    tri = jnp.where(row >= col, 1.0, 0.0).astype(BF16)
    pieces = jnp.concatenate(_split3(log_f), axis=1)
    carry = carry_ref[...]
    cs = []
    for r in range(2):
        local = _dot(tri, pieces[r * half:(r + 1) * half])
        cs.append(local[:, :LANES] + local[:, LANES:2 * LANES] + local[:, 2 * LANES:] + carry)
        carry = cs[-1][half - 1:half, :]
    carry_ref[...] = carry
    c = jnp.concatenate(cs, axis=0) * LOG2_E

    hi = c.astype(BF16).astype(F32)
    mid = (c - hi).astype(BF16).astype(F32)
    lo = c - hi - mid
    kind = kind_ref[...]
    one = jnp.where(kind < AUX_PER_HEAD, 1.0, 0.0)
    pick = lambda base: jnp.where(kind == base, hi, jnp.where(kind == base + 1, mid,
                                  jnp.where(kind == base + 2, lo, 0.0)))
    fqa_ref[0] = (pick(3) + jnp.where(kind < 3, one, 0.0)).astype(BF16)
    fka_ref[0] = (jnp.where(kind >= 3, one, 0.0) - pick(0)).astype(BF16)


def _inproj(x, g, w_main, w_vt, b_logit, cosm, sina, sinb):
    b, s, d = x.shape
    tm, tk = TM_IN, TQ
    lane = jnp.arange(LANES, dtype=I32)
    kind = jnp.where(lane < AUX_PER_HEAD * N_FOX_HEADS, lane % AUX_PER_HEAD, AUX_PER_HEAD)[None, :]
    act = jax.ShapeDtypeStruct((b, s, FOX_WIDTH), BF16)
    aux = jax.ShapeDtypeStruct((b, s, LANES), BF16)
    vt = jax.ShapeDtypeStruct((b, s // tk, FOX_WIDTH, tk), BF16)
    out_shape = [act, aux, act, aux, vt, act, act, vt,
                 jax.ShapeDtypeStruct((b, s // tm, SUBLANES, MOBA_WIDTH), F32)]
    act_spec = pl.BlockSpec((1, tm, FOX_WIDTH), lambda bi, j: (bi, j, 0))
    aux_spec = pl.BlockSpec((1, tm, LANES), lambda bi, j: (bi, j, 0))
    vt_spec = pl.BlockSpec((1, tm // tk, FOX_WIDTH, tk), lambda bi, j: (bi, j, 0, 0))
    tab_spec = pl.BlockSpec((tm, LANES), lambda bi, j: (j, 0))
    const2 = lambda bi, j: (0, 0)
    const3 = lambda bi, j: (0, 0, 0)
    *acts, kmean = pl.pallas_call(
        _inproj_kernel,
        grid=(b, s // tm),
        in_specs=[pl.BlockSpec((1, tm, d), lambda bi, j: (bi, j, 0)),
                  pl.BlockSpec((1, d), const2),
                  pl.BlockSpec(w_main.shape, const2),
                  pl.BlockSpec(w_vt.shape, const3),
                  pl.BlockSpec((1, LANES), const2), pl.BlockSpec((1, LANES), const2),
                  tab_spec, tab_spec, tab_spec],
        out_specs=[act_spec, aux_spec, act_spec, aux_spec, vt_spec, act_spec, act_spec, vt_spec,
                   pl.BlockSpec((1, 1, SUBLANES, MOBA_WIDTH), lambda bi, j: (bi, j, 0, 0))],
        out_shape=out_shape,
        scratch_shapes=[pltpu.VMEM((1, LANES), F32)],
        compiler_params=_params(("arbitrary", "arbitrary")),
        name="inproj",
    )(x, g, w_main, w_vt, b_logit, kind, cosm, sina, sinb)
    kmean = kmean[:, :, :tm // MOBA_BLOCK].reshape(b, s // MOBA_BLOCK, MOBA_WIDTH)
    return (*acts, kmean)


def _softmax(heads, scores, m_ref):
    stats = []
    for hh, s in zip(heads, scores):
        m_prev = m_ref[hh]
        m_new = jnp.maximum(m_prev, jnp.max(s, axis=0, keepdims=True))
        m_ref[hh] = m_new
        stats.append((jnp.exp2(m_prev - m_new), m_new))
    return [(alpha, jnp.exp2(s - m_new).astype(BF16)) for (alpha, m_new), s in zip(stats, scores)]


def _values(weighted, vt, l_ref, acc_ref):
    ones = jnp.ones((BF16_SUBLANES, vt.shape[1]), BF16)
    for hh, (alpha, p) in enumerate(weighted):
        rows = slice(hh * HEAD_DIM, (hh + 1) * HEAD_DIM)
        pv = _dot(jnp.concatenate([vt[rows, :], ones], axis=0), p)
        acc_ref[rows, :] = alpha * acc_ref[rows, :] + pv[:HEAD_DIM]
        l_ref[hh] = alpha * l_ref[hh] + pv[HEAD_DIM:HEAD_DIM + 1]


def _attend_tiles(i, scores_of, vt_ref, m_ref, l_ref, acc_ref):
    heads = tuple(range(HEADS_PER_STEP))

    def block(tiles):
        scores = [scores_of(kt, diag, heads) for kt, diag in tiles]
        for (kt, _), s in zip(tiles, scores):
            _values(_softmax(heads, s, m_ref), vt_ref[0, kt], l_ref, acc_ref)

    for extra in range(TRIP_TILES):
        @pl.when(i % TRIP_TILES == extra)
        def _(extra=extra):
            block([(i, True)] + [(i - extra + r, False) for r in range(extra)])

    def trip(j, c):
        block([(TRIP_TILES * j + r, False) for r in range(TRIP_TILES)])
        return c

    lax.fori_loop(0, i // TRIP_TILES, trip, 0)


def _attn_init(m_ref, l_ref, acc_ref):
    m_ref[...] = jnp.full(m_ref.shape, NEG_INF, F32)
    l_ref[...] = jnp.zeros_like(l_ref)
    acc_ref[...] = jnp.zeros_like(acc_ref)


def _attn_finish(o_ref, l_ref, acc_ref):
    out_t = jnp.concatenate(
        [acc_ref[hh * HEAD_DIM:(hh + 1) * HEAD_DIM, :] / l_ref[hh] for hh in range(HEADS_PER_STEP)],
        axis=0)
    o_ref[0] = out_t.T


def _block(a, g):
    return a[:, g * LANES:(g + 1) * LANES]


def _per_head(a, width):
    first = lax.broadcasted_iota(I32, (a.shape[0], LANES), 1) < width
    zero = jnp.zeros((a.shape[0], LANES), a.dtype)
    out = []
    for g in range(BLOCKS_PER_STEP):
        blk = _block(a, g)
        out += [jnp.where(first, blk, zero), jnp.where(first, zero, blk)]
    return out


def _key_le_query(tq):
    return lax.broadcasted_iota(I32, (tq, tq), 0) <= lax.broadcasted_iota(I32, (tq, tq), 1)


def _fox_kernel(q_ref, qa_ref, k_ref, ka_ref, vt_ref, o_ref, m_ref, l_ref, acc_ref):
    i = pl.program_id(2)
    tq = q_ref.shape[1]
    qa = qa_ref[0]
    lane = lax.broadcasted_iota(I32, qa.shape, 1) - pl.program_id(1) * (HEADS_PER_STEP * AUX_PER_HEAD)
    own_aux = lambda hh: jnp.logical_and(lane >= hh * AUX_PER_HEAD, lane < (hh + 1) * AUX_PER_HEAD)
    qq = [jnp.concatenate([qm, jnp.where(own_aux(hh), qa, jnp.zeros_like(qa))], axis=1)
          for hh, qm in enumerate(_per_head(q_ref[0], HEAD_DIM))]
    causal = _key_le_query(tq)
    _attn_init(m_ref, l_ref, acc_ref)

    def scores_of(kt, diag, heads):
        ks = pl.multiple_of(kt * tq, tq)
        ka = ka_ref[0, pl.ds(ks, tq), :]
        kk = {g: jnp.concatenate([k_ref[0, pl.ds(ks, tq), g * LANES:(g + 1) * LANES], ka], axis=1)
              for g in sorted({hh // HEADS_PER_BLOCK for hh in heads})}
        scores = [_dot_nt(kk[hh // HEADS_PER_BLOCK], qq[hh]) for hh in heads]
        if diag:
            scores = [jnp.where(causal, s, NEG_INF) for s in scores]
        return tuple(scores)

    _attend_tiles(i, scores_of, vt_ref, m_ref, l_ref, acc_ref)
    _attn_finish(o_ref, l_ref, acc_ref)


def _moba_kernel(q_ref, k_ref, hot_ref, vt_ref, kmean_ref, o_ref, m_ref, l_ref, acc_ref):
    i = pl.program_id(2)
    tq = q_ref.shape[1]
    nblk = kmean_ref.shape[1]
    qs = _per_head(q_ref[0], HEAD_DIM)
    causal = _key_le_query(tq)
    _attn_init(m_ref, l_ref, acc_ref)

    km_parts = jnp.concatenate(_split3(kmean_ref[0]), axis=0)
    blk = lax.broadcasted_iota(I32, (nblk, tq), 0).astype(F32)
    past = blk < i.astype(F32)
    masks = []
    for hh in range(HEADS_PER_STEP):
        pieces = _dot_nt(_block(km_parts, hh // HEADS_PER_BLOCK), qs[hh])
        gate = pieces[:nblk] + pieces[nblk:2 * nblk] + pieces[2 * nblk:]
        sel = jnp.zeros((nblk, tq), jnp.bool_)
        for _ in range(MOBA_TOPK):
            remaining = jnp.logical_and(past, jnp.logical_not(sel))
            g = jnp.where(remaining, gate, NEG_INF)
            first = jnp.min(jnp.where(g == jnp.max(g, axis=0, keepdims=True), blk, float(nblk)),
                            axis=0, keepdims=True)
            sel = jnp.logical_or(sel, jnp.logical_and(blk == first, remaining))
        masks.append(jnp.where(sel, 0.0, MASKED))
    unused = LANES - HEADS_PER_STEP * nblk
    masks += [jnp.zeros((unused, tq), F32)] if unused else []
    qa = jnp.concatenate(masks, axis=0).T.astype(BF16)
    lane = lax.broadcasted_iota(I32, qa.shape, 1)
    own = lambda hh: jnp.logical_and(lane >= hh * nblk, lane < (hh + 1) * nblk)
    qq = [jnp.concatenate([qs[hh], jnp.where(own(hh), qa, jnp.zeros_like(qa))], axis=1)
          for hh in range(HEADS_PER_STEP)]

    def scores_of(kt, diag, heads):
        ks = pl.multiple_of(kt * tq, tq)
        k = lambda hh: k_ref[0, pl.ds(ks, tq), (hh // HEADS_PER_BLOCK) * LANES:(hh // HEADS_PER_BLOCK + 1) * LANES]
        if diag:
            return tuple(jnp.where(causal, _dot_nt(k(hh), qs[hh]), NEG_INF) for hh in heads)
        hot = hot_ref[pl.ds(ks, tq), :]
        return tuple(_dot_nt(jnp.concatenate([k(hh), hot], axis=1), qq[hh]) for hh in heads)

    _attend_tiles(i, scores_of, vt_ref, m_ref, l_ref, acc_ref)
    _attn_finish(o_ref, l_ref, acc_ref)


def _attn_scratch(tq):
    return [pltpu.VMEM((HEADS_PER_STEP, 1, tq), F32), pltpu.VMEM((HEADS_PER_STEP, 1, tq), F32),
            pltpu.VMEM((STEP_LANES, tq), F32)]


def _attn_specs(s, tq):
    q_spec = pl.BlockSpec((1, tq, STEP_LANES), lambda bi, hb, i: (bi, i, hb))
    k_spec = pl.BlockSpec((1, s, STEP_LANES), lambda bi, hb, i: (bi, 0, hb))
    vt_spec = pl.BlockSpec((1, s // tq, STEP_LANES, tq), lambda bi, hb, i: (bi, 0, hb, 0))
    return q_spec, k_spec, vt_spec


def _fox(q, qa, k, ka, vt):
    b, s, width = q.shape
    tq = TQ
    q_spec, k_spec, vt_spec = _attn_specs(s, tq)
    return pl.pallas_call(
        _fox_kernel,
        grid=(b, width // STEP_LANES, s // tq),
        in_specs=[q_spec, pl.BlockSpec((1, tq, LANES), lambda bi, hb, i: (bi, i, 0)),
                  k_spec, pl.BlockSpec((1, s, LANES), lambda bi, hb, i: (bi, 0, 0)), vt_spec],
        out_specs=q_spec,
        out_shape=jax.ShapeDtypeStruct((b, s, width), F32),
        scratch_shapes=_attn_scratch(tq),
        compiler_params=_params(("arbitrary", "arbitrary", "arbitrary")),
        name="fox",
    )(q, qa, k, ka, vt)


def _moba(q, k, vt, kmean):
    b, s, width = q.shape
    tq = TQ
    nblk = kmean.shape[1]
    assert HEADS_PER_STEP * nblk <= LANES, "one aux lane per (head, key block)"
    lane = jnp.arange(LANES)
    hot = jnp.logical_and(lane[None, :] < HEADS_PER_STEP * nblk,
                          lane[None, :] % nblk == jnp.arange(s)[:, None] // MOBA_BLOCK).astype(BF16)
    q_spec, k_spec, vt_spec = _attn_specs(s, tq)
    return pl.pallas_call(
        _moba_kernel,
        grid=(b, width // STEP_LANES, s // tq),
        in_specs=[q_spec, k_spec, pl.BlockSpec((s, LANES), lambda bi, hb, i: (0, 0)), vt_spec,
                  pl.BlockSpec((1, nblk, STEP_LANES), lambda bi, hb, i: (bi, 0, hb))],
        out_specs=q_spec,
        out_shape=jax.ShapeDtypeStruct((b, s, width), F32),
        scratch_shapes=_attn_scratch(tq),
        compiler_params=_params(("arbitrary", "arbitrary", "arbitrary")),
        name="moba",
    )(q, k, hot, vt, kmean)


def _postattn_kernel(x_ref, fox_ref, moba_ref, gf_ref, gm_ref, wo_ref, gn_ref, wr_ref, br_ref,
                     x2_ref, h2_ref, wts_ref, route_ref, cnt_ref, carry_ref):
    t = pl.program_id(0)
    tm = x_ref.shape[0]
    fw = fox_ref.shape[1]
    mixed_f = _rms(fox_ref[...], gf_ref[...]).astype(BF16)
    mixed_m = _rms(moba_ref[...], gm_ref[...]).astype(BF16)
    x2 = x_ref[...] + _dot(mixed_f, wo_ref[:fw, :]) + _dot(mixed_m, wo_ref[fw:, :])
    x2_ref[...] = x2
    h2 = _rms(x2, gn_ref[...])
    _store_rows(h2_ref, h2)

    h_hi = h2.astype(BF16)
    h_lo = (h2 - h_hi.astype(F32)).astype(BF16)
    by_hi = _dot_nt(wr_ref[...], h_hi)
    logits = (by_hi[:ROUTE_ROWS] + by_hi[LANES:LANES + ROUTE_ROWS]
              + _dot_nt(wr_ref[:ROUTE_ROWS, :], h_lo) + br_ref[...])
    row = lax.broadcasted_iota(I32, (ROUTE_ROWS, tm), 0).astype(F32)

    def first_max(vals):
        mx = jnp.max(vals, axis=0, keepdims=True)
        return mx, jnp.min(jnp.where(vals == mx, row, float(ROUTE_ROWS)), axis=0, keepdims=True)

    gl = jnp.where(row < N_GROUPS, logits, NEG_INF)
    gmax, g_idx = first_max(gl)
    g_top = 1.0 / jnp.sum(jnp.exp(gl - gmax), axis=0, keepdims=True)
    e_lo = N_GROUPS + EXPERTS_PER_GROUP * g_idx
    el = jnp.where(jnp.logical_and(row >= e_lo, row < e_lo + EXPERTS_PER_GROUP), logits, NEG_INF)
    emax, i1 = first_max(el)
    esum = jnp.sum(jnp.exp(el - emax), axis=0, keepdims=True)
    e2max, i2 = first_max(jnp.where(row == i1, NEG_INF, el))
    p1 = 1.0 / esum
    p2 = jnp.exp(e2max - emax) / esum
    w1 = p1 / (p1 + p2) * g_top
    w2 = p2 / (p1 + p2) * g_top
    e1 = i1 - N_GROUPS
    e2 = i2 - N_GROUPS
    zeros = jnp.zeros((SUBLANES - 2, tm), F32)
    wts_ref[...] = jnp.concatenate([w1, w2, zeros], axis=0).T[:, :2]

    @pl.when(t == 0)
    def _():
        carry_ref[...] = jnp.zeros_like(carry_ref)

    earlier = jnp.where(lax.broadcasted_iota(I32, (tm, tm), 0) < lax.broadcasted_iota(I32, (tm, tm), 1),
                        1.0, 0.0).astype(BF16)
    hit1 = row == e1
    hit2 = row == e2
    oh1 = jnp.where(hit1, 1.0, 0.0)
    oh2 = jnp.where(hit2, 1.0, 0.0)
    tot1 = jnp.sum(oh1, axis=1, keepdims=True)
    tot2 = jnp.sum(oh2, axis=1, keepdims=True)
    base = carry_ref[...]
    before = _dot(jnp.concatenate([oh1, oh2], axis=0).astype(BF16), earlier)
    before1 = before[:ROUTE_ROWS] + base
    before2 = before[ROUTE_ROWS:] + (base + tot1)
    r1 = jnp.sum(jnp.where(hit1, before1, 0.0), axis=0, keepdims=True)
    r2 = jnp.sum(jnp.where(hit2, before2, 0.0), axis=0, keepdims=True)
    route_ref[...] = jnp.concatenate([e1, e2, r1, r2, zeros[:SUBLANES - 4]], axis=0).astype(I32)
    total = base + tot1 + tot2
    carry_ref[...] = total
    cnt_ref[...] = total.astype(I32)


def _postattn(x, fox, moba, gf, gm, wo, gn, wr, br):
    t, d = x.shape
    tm = TM_PROJ
    fw = fox.shape[1]
    const = lambda i: (0, 0)
    rows = lambda i: (i, 0)
    return pl.pallas_call(
        _postattn_kernel,
        grid=(t // tm,),
        in_specs=[pl.BlockSpec((tm, d), rows), pl.BlockSpec((tm, fw), rows),
                  pl.BlockSpec((tm, moba.shape[1]), rows),
                  pl.BlockSpec((1, fw), const), pl.BlockSpec((1, moba.shape[1]), const),
                  pl.BlockSpec(wo.shape, const), pl.BlockSpec((1, d), const),
                  pl.BlockSpec(wr.shape, const), pl.BlockSpec((ROUTE_ROWS, 1), const)],
        out_specs=[pl.BlockSpec((tm, d), rows), pl.BlockSpec((tm * SUBLANES, LANES), rows),
                   pl.BlockSpec((tm, 2), rows), pl.BlockSpec((SUBLANES, tm), lambda i: (0, i)),
                   pl.BlockSpec((ROUTE_ROWS, 1), const)],
        out_shape=[jax.ShapeDtypeStruct((t, d), F32), jax.ShapeDtypeStruct((t * SUBLANES, LANES), F32),
                   jax.ShapeDtypeStruct((t, 2), F32), jax.ShapeDtypeStruct((SUBLANES, t), I32),
                   jax.ShapeDtypeStruct((ROUTE_ROWS, 1), I32)],
        scratch_shapes=[pltpu.VMEM((ROUTE_ROWS, 1), F32)],
        compiler_params=_params(("arbitrary",)),
        name="postattn",
    )(x, fox, moba, gf, gm, wo, gn, wr, br)


def _store_rows(ref, val):
    for g in range(SUBLANES):
        ref[pl.ds(g, val.shape[0], stride=SUBLANES), :] = val[:, g * LANES:(g + 1) * LANES]


def _load_rows(ref):
    tokens = ref.shape[0] // SUBLANES
    return jnp.concatenate([ref[pl.ds(g, tokens, stride=SUBLANES), :] for g in range(SUBLANES)], axis=1)


def _row_copy(src, src_row, dst, dst_row, sem, tokens=1):
    window = lambda r: pl.ds(pl.multiple_of(r * SUBLANES, SUBLANES), tokens * SUBLANES)
    return pltpu.make_async_copy(src.at[window(src_row)], dst.at[window(dst_row)], sem)


RING = 3
ZERO_TOKENS = 256


def _dispatch_kernel(pad_start_ref, pad_len_ref, dest_ref, h_ref, xs_ref,
                     ring_ref, zero_ref, fetch_sems, scatter_sems, pad_sem):
    i = pl.program_id(0)
    last = pl.num_programs(0) - 1
    tm = dest_ref.shape[2] // 2
    tile_rows = tm * SUBLANES
    row_copy = _row_copy
    zero_tokens = zero_ref.shape[0] // SUBLANES

    def fetch(tile):
        start = pl.multiple_of(tile * tile_rows, tile_rows)
        slot = lax.rem(tile, RING)
        return pltpu.make_async_copy(h_ref.at[pl.ds(start, tile_rows)], ring_ref.at[slot],
                                     fetch_sems.at[slot])

    def zero_fill(e, act):
        start, n = pad_start_ref[e], pad_len_ref[e]
        whole = lax.shift_right_logical(n, zero_tokens.bit_length() - 1)
        rest = jnp.bitwise_and(n, zero_tokens - 1)

        def chunk(c, carry):
            act(_row_copy(zero_ref, 0, xs_ref, start + c * zero_tokens, pad_sem, tokens=zero_tokens))
            return carry

        lax.fori_loop(0, whole, chunk, 0)
        bit = zero_tokens // 2
        while bit:
            @pl.when(jnp.bitwise_and(rest, bit) != 0)
            def _(bit=bit):
                above = jnp.bitwise_and(rest, -2 * bit)
                act(_row_copy(zero_ref, 0, xs_ref, start + whole * zero_tokens + above, pad_sem, tokens=bit))
            bit //= 2

    @pl.when(i == 0)
    def _():
        fetch(0).start()
        zero_ref[...] = jnp.zeros_like(zero_ref)
        n_regions = pad_start_ref.shape[0]
        lax.fori_loop(0, n_regions, lambda e, c: (zero_fill(e, lambda cp: cp.start()), c)[1], 0)
        lax.fori_loop(0, n_regions, lambda e, c: (zero_fill(e, lambda cp: cp.wait()), c)[1], 0)

    @pl.when(i < last)
    def _():
        fetch(i + 1).start()

    fetch(i).wait()
    src = ring_ref.at[lax.rem(i, RING)]

    def issue(r, c):
        for k in range(2):
            row_copy(src, r, xs_ref, dest_ref[0, 0, k * tm + r], scatter_sems.at[i % 2]).start(priority=k)
        return c

    lax.fori_loop(0, tm, issue, 0, unroll=ISSUE_UNROLL)

    def drain(parity):
        def one(r, c):
            row_copy(src, 0, xs_ref, 0, scatter_sems.at[parity]).wait()
            return c

        lax.fori_loop(0, 2 * tm, one, 0, unroll=ISSUE_UNROLL)

    @pl.when(i > 0)
    def _():
        drain((i - 1) % 2)

    @pl.when(i == last)
    def _():
        drain(i % 2)


def _dispatch(pad_start, pad_len, dest3, h2, n_rows):
    tm = dest3.shape[2] // 2
    grid_spec = pltpu.PrefetchScalarGridSpec(
        num_scalar_prefetch=2,
        grid=(dest3.shape[0],),
        in_specs=[pl.BlockSpec((1, 1, 2 * tm), lambda i, ps, pn: (i, 0, 0), memory_space=pltpu.SMEM),
                  pl.BlockSpec(memory_space=pl.ANY)],
        out_specs=pl.BlockSpec(memory_space=pl.ANY),
        scratch_shapes=[pltpu.VMEM((RING, tm * SUBLANES, LANES), F32),
                        pltpu.VMEM((ZERO_TOKENS * SUBLANES, LANES), F32),
                        pltpu.SemaphoreType.DMA((RING,)), pltpu.SemaphoreType.DMA((2,)),
                        pltpu.SemaphoreType.DMA(())],
    )
    return pl.pallas_call(
        _dispatch_kernel,
        grid_spec=grid_spec,
        out_shape=jax.ShapeDtypeStruct((n_rows * SUBLANES, LANES), F32),
        compiler_params=_params(("arbitrary",)),
        name="dispatch",
    )(pad_start, pad_len, dest3, h2)


def _experts_kernel(te_ref, ts_ref, nv_ref, xs_ref, wg_ref, wu_ref, wd_ref, ys_ref,
                    wgb_ref, wub_ref, wdb_ref):
    del ts_ref
    t = pl.program_id(0)

    @pl.when(jnp.logical_or(t == 0, te_ref[t] != te_ref[jnp.maximum(t - 1, 0)]))
    def _():
        wgb_ref[...] = wg_ref[0].astype(BF16)
        wub_ref[...] = wu_ref[0].astype(BF16)
        wdb_ref[...] = wd_ref[0].astype(BF16)

    @pl.when(t < nv_ref[0])
    def _():
        xb = _load_rows(xs_ref).astype(BF16)
        a = _dot(xb, wgb_ref[...])
        u = _dot(xb, wub_ref[...])
        act = (a * jax.nn.sigmoid(a) * u).astype(BF16)
        _store_rows(ys_ref, _dot(act, wdb_ref[...]))

    @pl.when(t >= nv_ref[0])
    def _():
        ys_ref[...] = jnp.zeros_like(ys_ref)


def _experts(tile_expert, tile_src, n_valid, xs, wg, wu, wd):
    tm = TM_EXPERT
    n_tiles = xs.shape[0] // (tm * SUBLANES)
    _, d, f = wg.shape
    row_block = (tm * SUBLANES, LANES)
    grid_spec = pltpu.PrefetchScalarGridSpec(
        num_scalar_prefetch=3,
        grid=(n_tiles,),
        in_specs=[pl.BlockSpec(row_block, lambda t, te, ts, nv: (ts[t], 0)),
                  pl.BlockSpec((1, d, f), lambda t, te, ts, nv: (te[t], 0, 0)),
                  pl.BlockSpec((1, d, f), lambda t, te, ts, nv: (te[t], 0, 0)),
                  pl.BlockSpec((1, f, d), lambda t, te, ts, nv: (te[t], 0, 0))],
        out_specs=pl.BlockSpec(row_block, lambda t, te, ts, nv: (t, 0)),
        scratch_shapes=[pltpu.VMEM((d, f), BF16), pltpu.VMEM((d, f), BF16), pltpu.VMEM((f, d), BF16)],
    )
    return pl.pallas_call(
        _experts_kernel,
        grid_spec=grid_spec,
        out_shape=jax.ShapeDtypeStruct((n_tiles * tm * SUBLANES, LANES), F32),
        compiler_params=_params(("arbitrary",)),
        name="experts",
    )(tile_expert, tile_src, n_valid, xs, wg, wu, wd)


def _combine_kernel(dest_ref, next_ref, x2_ref, wts_ref, g_ref, ys_ref, o_ref, buf_ref, sems):
    i = pl.program_id(0)
    tm = dest_ref.shape[2] // 2
    slot = i % 2

    def gather(d_ref, to):
        def issue(r, c):
            for k in range(2):
                _row_copy(ys_ref, d_ref[0, 0, k * tm + r], buf_ref.at[to, k], r,
                          sems.at[to]).start(priority=k)
            return c

        lax.fori_loop(0, tm, issue, 0, unroll=ISSUE_UNROLL)

    @pl.when(i == 0)
    def _():
        gather(dest_ref, 0)

    @pl.when(i + 1 < pl.num_programs(0))
    def _():
        gather(next_ref, 1 - slot)

    def drain(r, c):
        _row_copy(ys_ref, 0, buf_ref.at[slot, 0], 0, sems.at[slot]).wait()
        return c

    lax.fori_loop(0, 2 * tm, drain, 0, unroll=ISSUE_UNROLL)
    w = wts_ref[...]
    y = (x2_ref[...] + w[:, 0:1] * _load_rows(buf_ref.at[slot, 0])
         + w[:, 1:2] * _load_rows(buf_ref.at[slot, 1]))
    o_ref[...] = _rms(y, g_ref[...])


def _combine(dest3, x2, wts, g, ys):
    t, d = x2.shape
    tm = dest3.shape[2] // 2
    rows = lambda i: (i, 0)
    n = t // tm
    return pl.pallas_call(
        _combine_kernel,
        grid=(n,),
        in_specs=[pl.BlockSpec((1, 1, 2 * tm), lambda i: (i, 0, 0), memory_space=pltpu.SMEM),
                  pl.BlockSpec((1, 1, 2 * tm), lambda i: (jnp.minimum(i + 1, n - 1), 0, 0),
                               memory_space=pltpu.SMEM),
                  pl.BlockSpec((tm, d), rows), pl.BlockSpec((tm, 2), rows),
                  pl.BlockSpec((1, d), lambda i: (0, 0)),
                  pl.BlockSpec(memory_space=pl.ANY)],
        out_specs=pl.BlockSpec((tm, d), rows),
        out_shape=jax.ShapeDtypeStruct((t, d), F32),
        scratch_shapes=[pltpu.VMEM((2, 2, tm * SUBLANES, LANES), F32), pltpu.SemaphoreType.DMA((2,))],
        compiler_params=_params(("arbitrary",)),
        name="combine",
    )(dest3, dest3, x2, wts, g, ys)


def _rotary_tables(seq):
    half = ROPE_DIM // 2
    inv_freq = ROPE_THETA ** (-jnp.arange(half, dtype=F32) / half)
    ang = jnp.arange(seq, dtype=F32)[:, None] * inv_freq[None, :]
    cos, sin = jnp.cos(ang), jnp.sin(ang)
    ones = jnp.ones((seq, HEAD_DIM - ROPE_DIM), F32)
    zeros = jnp.zeros((seq, HEAD_DIM - ROPE_DIM), F32)
    zh = jnp.zeros((seq, half), F32)
    cosm = jnp.concatenate([cos, cos, ones], axis=1)
    sina = jnp.concatenate([-sin, zh, zeros], axis=1)
    sinb = jnp.concatenate([zh, sin, zeros], axis=1)
    tile = lambda a: jnp.tile(a, (1, HEADS_PER_BLOCK))
    return tile(cosm), tile(sina), tile(sinb)


def _pad_lanes(a):
    return jnp.pad(a, ((0, 0), (0, LANES - a.shape[1])))


def kernel(x, norm_mix_g, w_in, b_forget, fox_out_g, moba_out_g, w_out, norm_ffn_g, w_router_group,
           b_router_group, w_router_expert, b_router_expert, w_gate, w_up, w_down, norm_final_g):
    b, s, d = x.shape
    t = b * s
    assert w_in.shape[0] == 1, "the closing RMSNorm is fused into the only layer's combine step"
    cosm, sina, sinb = _rotary_tables(s)
    n_tiles = (2 * t) // TM_EXPERT + N_EXPERTS
    fw3 = 3 * FOX_WIDTH
    m0 = fw3 + N_FOX_HEADS
    wl = w_in[0]
    aux_head = jnp.arange(AUX_PER_HEAD * N_FOX_HEADS) // AUX_PER_HEAD
    w_main = jnp.concatenate([wl[:, :2 * FOX_WIDTH], wl[:, m0:m0 + 2 * MOBA_WIDTH],
                              _pad_lanes(wl[:, fw3:m0][:, aux_head])],
                             axis=1).astype(BF16)
    w_vt = jnp.stack([wl[:, 2 * FOX_WIDTH:fw3].T, wl[:, m0 + 2 * MOBA_WIDTH:].T]).astype(BF16)
    b_logit = _pad_lanes(b_forget[0][None, aux_head])
    fq, fqa, fk, fka, fvt, mq, mk, mvt, kmean = _inproj(
        x, norm_mix_g[0][None, :], w_main, w_vt, b_logit, cosm, sina, sinb)
    fox = _fox(fq, fqa, fk, fka, fvt)
    moba = _moba(mq, mk, mvt, kmean)

    w_router = _pad_lanes(jnp.concatenate(
        [w_router_group[0], w_router_expert[0].reshape(d, N_EXPERTS)], axis=1))
    w_router_hi = w_router.astype(BF16)
    w_router = jnp.concatenate([w_router_hi, (w_router - w_router_hi.astype(F32)).astype(BF16)], axis=1).T
    b_router = _pad_lanes(jnp.concatenate(
        [b_router_group[0], b_router_expert[0].reshape(N_EXPERTS)])[None, :])[:, :ROUTE_ROWS].T
    x2, h2, wts, route, counts = _postattn(
        x.reshape(t, d), fox.reshape(t, FOX_WIDTH), moba.reshape(t, MOBA_WIDTH),
        fox_out_g[0][None, :], moba_out_g[0][None, :], w_out[0].astype(BF16),
        norm_ffn_g[0][None, :], w_router, b_router)

    counts = counts[:N_EXPERTS, 0]
    padded = (counts + TM_EXPERT - 1) // TM_EXPERT * TM_EXPERT
    ends = jnp.cumsum(padded)
    starts = ends - padded
    expert_ids = jnp.arange(N_EXPERTS, dtype=I32)[:, None, None]
    dest = route[2:4] + jnp.sum(jnp.where(route[None, 0:2] == expert_ids, starts[:, None, None], 0),
                                axis=0)
    dest3 = dest.reshape(2, t // TM_ROWS, TM_ROWS).transpose(1, 0, 2).reshape(t // TM_ROWS, 1, 2 * TM_ROWS)
    n_rows = n_tiles * TM_EXPERT
    pad_start = jnp.concatenate([starts + counts, ends[-1:]]).astype(I32)
    pad_len = jnp.concatenate([padded - counts, n_rows - ends[-1:]]).astype(I32)
    n_valid = ends[-1] // TM_EXPERT
    tile_src = jnp.minimum(jnp.arange(n_tiles, dtype=I32), n_valid - 1)
    tile_expert = jnp.sum(ends[None, :] <= (tile_src * TM_EXPERT)[:, None], axis=1).astype(I32)
    xs = _dispatch(pad_start, pad_len, dest3, h2, n_rows)
    ys = _experts(tile_expert, tile_src, n_valid.reshape(1).astype(I32), xs,
                  w_gate[0], w_up[0], w_down[0])
    return _combine(dest3, x2, wts, norm_final_g[None, :], ys).reshape(b, s, d)
```

```python
import math

import jax
import jax.numpy as jnp
from jax import lax
from jax.experimental import pallas as pl
from jax.experimental.pallas import tpu as pltpu

F32 = jnp.float32
BF16 = jnp.bfloat16
I32 = jnp.int32

HEAD_DIM = 64
N_FOX_HEADS = 8
N_MOBA_HEADS = 8
FOX_WIDTH = N_FOX_HEADS * HEAD_DIM
MOBA_WIDTH = N_MOBA_HEADS * HEAD_DIM
MOBA_BLOCK = 256
MOBA_TOPK = 3
ROPE_THETA = 500000.0
ROPE_DIM = HEAD_DIM // 4
N_GROUPS = 4
EXPERTS_PER_GROUP = 8
N_EXPERTS = N_GROUPS * EXPERTS_PER_GROUP
EPS = 1e-6

LANES = 128
SUBLANES = 8
BF16_SUBLANES = 16
LOG2_E = math.log2(math.e)
HEADS_PER_BLOCK = LANES // HEAD_DIM
BLOCKS_PER_STEP = 4
HEADS_PER_STEP = HEADS_PER_BLOCK * BLOCKS_PER_STEP
STEP_LANES = LANES * BLOCKS_PER_STEP
VMEM_LIMIT = 56 * 1024 * 1024
AUX_PER_HEAD = 6

TM_IN = 1024
TM_PROJ = 512
TQ = 256
TM_EXPERT = 512
TM_ROWS = 512
ISSUE_UNROLL = 8
ROUTE_ROWS = 64
TRIP_TILES = 4

NEG_INF = float("-inf")
MASKED = -1e30


def _params(sem):
    return pltpu.CompilerParams(dimension_semantics=sem, vmem_limit_bytes=VMEM_LIMIT)


def _rms(x, g):
    return x * lax.rsqrt(jnp.mean(x * x, axis=-1, keepdims=True) + EPS) * g


def _split3(x):
    hi = x.astype(BF16)
    r = x - hi.astype(F32)
    mid = r.astype(BF16)
    lo = (r - mid.astype(F32)).astype(BF16)
    return hi, mid, lo


def _dot(a, b):
    return jnp.dot(a, b, preferred_element_type=F32)


def _dot_nt(a, b):
    return lax.dot_general(a, b, (((1,), (1,)), ((), ())), preferred_element_type=F32)


def _inproj_kernel(x_ref, g_ref, w_ref, wvt_ref, bf_ref, kind_ref, cosm_ref, sina_ref, sinb_ref,
                   fq_ref, fqa_ref, fk_ref, fka_ref, fvt_ref, mq_ref, mk_ref, mvt_ref, kmean_ref,
                   carry_ref):
    j = pl.program_id(1)
    tm = x_ref.shape[1]
    tk = fvt_ref.shape[3]
    h = _rms(x_ref[0], g_ref[...]).astype(BF16)
    scale = HEAD_DIM ** -0.5 * LOG2_E

    def proj(seg):
        return _dot(h, w_ref[:, seg * FOX_WIDTH:(seg + 1) * FOX_WIDTH])

    fq_ref[0] = (proj(0) * scale).astype(BF16)
    fk_ref[0] = proj(1).astype(BF16)

    for vt_ref, seg in ((fvt_ref, 0), (mvt_ref, 1)):
        vt = _dot_nt(wvt_ref[seg], h).astype(BF16)
        for r in range(tm // tk):
            vt_ref[0, r] = vt[:, r * tk:(r + 1) * tk]

    cosm, sina, sinb = cosm_ref[...], sina_ref[...], sinb_ref[...]

    def rotary(t):
        outs = []
        for g in range(MOBA_WIDTH // LANES):
            tg = t[:, g * LANES:(g + 1) * LANES]
            outs.append(tg * cosm + pltpu.roll(tg, LANES - ROPE_DIM // 2, 1) * sina
                        + pltpu.roll(tg, ROPE_DIM // 2, 1) * sinb)
        return jnp.concatenate(outs, axis=1)

    mq_ref[0] = (rotary(proj(2)) * scale).astype(BF16)
    mk_z = _dot(h, w_ref[:, 3 * FOX_WIDTH:])
    mk = rotary(mk_z[:, :MOBA_WIDTH])
    mk_ref[0] = mk.astype(BF16)
    nblk_tile = tm // MOBA_BLOCK
    means = [jnp.mean(mk[r * MOBA_BLOCK:(r + 1) * MOBA_BLOCK], axis=0, keepdims=True)
             for r in range(nblk_tile)]
    means += [jnp.zeros_like(means[0])] * (kmean_ref.shape[2] - nblk_tile)
    kmean_ref[0, 0] = jnp.concatenate(means, axis=0)

    z = mk_z[:, MOBA_WIDTH:] + bf_ref[...]
    log_f = jnp.minimum(z, 0.0) - jnp.log1p(jnp.exp(-jnp.abs(z)))

    @pl.when(j == 0)
    def _():
        carry_ref[...] = jnp.zeros_like(carry_ref)

    half = tm // 2
    row = lax.broadcasted_iota(I32, (half, half), 0)
    col = lax.broadcasted_iota(I32, (half, half), 1)
    tri = jnp.where(row >= col, 1.0, 0.0).astype(BF16)
    pieces = jnp.concatenate(_split3(log_f), axis=1)
    carry = carry_ref[...]
    cs = []
    for r in range(2):
        local = _dot(tri, pieces[r * half:(r + 1) * half])
        cs.append(local[:, :LANES] + local[:, LANES:2 * LANES] + local[:, 2 * LANES:] + carry)
        carry = cs[-1][half - 1:half, :]
    carry_ref[...] = carry
    c = jnp.concatenate(cs, axis=0) * LOG2_E

    hi = c.astype(BF16).astype(F32)
    mid = (c - hi).astype(BF16).astype(F32)
    lo = c - hi - mid
    kind = kind_ref[...]
    one = jnp.where(kind < AUX_PER_HEAD, 1.0, 0.0)
    pick = lambda base: jnp.where(kind == base, hi, jnp.where(kind == base + 1, mid,
                                  jnp.where(kind == base + 2, lo, 0.0)))
    fqa_ref[0] = (pick(3) + jnp.where(kind < 3, one, 0.0)).astype(BF16)
    fka_ref[0] = (jnp.where(kind >= 3, one, 0.0) - pick(0)).astype(BF16)


def _inproj(x, g, w_main, w_vt, b_logit, cosm, sina, sinb):
    b, s, d = x.shape
    tm, tk = TM_IN, TQ
    lane = jnp.arange(LANES, dtype=I32)
    kind = jnp.where(lane < AUX_PER_HEAD * N_FOX_HEADS, lane % AUX_PER_HEAD, AUX_PER_HEAD)[None, :]
    act = jax.ShapeDtypeStruct((b, s, FOX_WIDTH), BF16)
    aux = jax.ShapeDtypeStruct((b, s, LANES), BF16)
    vt = jax.ShapeDtypeStruct((b, s // tk, FOX_WIDTH, tk), BF16)
    out_shape = [act, aux, act, aux, vt, act, act, vt,
                 jax.ShapeDtypeStruct((b, s // tm, SUBLANES, MOBA_WIDTH), F32)]
    act_spec = pl.BlockSpec((1, tm, FOX_WIDTH), lambda bi, j: (bi, j, 0))
    aux_spec = pl.BlockSpec((1, tm, LANES), lambda bi, j: (bi, j, 0))
    vt_spec = pl.BlockSpec((1, tm // tk, FOX_WIDTH, tk), lambda bi, j: (bi, j, 0, 0))
    tab_spec = pl.BlockSpec((tm, LANES), lambda bi, j: (j, 0))
    const2 = lambda bi, j: (0, 0)
    const3 = lambda bi, j: (0, 0, 0)
    *acts, kmean = pl.pallas_call(
        _inproj_kernel,
        grid=(b, s // tm),
        in_specs=[pl.BlockSpec((1, tm, d), lambda bi, j: (bi, j, 0)),
                  pl.BlockSpec((1, d), const2),
                  pl.BlockSpec(w_main.shape, const2),
                  pl.BlockSpec(w_vt.shape, const3),
                  pl.BlockSpec((1, LANES), const2), pl.BlockSpec((1, LANES), const2),
                  tab_spec, tab_spec, tab_spec],
        out_specs=[act_spec, aux_spec, act_spec, aux_spec, vt_spec, act_spec, act_spec, vt_spec,
                   pl.BlockSpec((1, 1, SUBLANES, MOBA_WIDTH), lambda bi, j: (bi, j, 0, 0))],
        out_shape=out_shape,
        scratch_shapes=[pltpu.VMEM((1, LANES), F32)],
        compiler_params=_params(("arbitrary", "arbitrary")),
        name="inproj",
    )(x, g, w_main, w_vt, b_logit, kind, cosm, sina, sinb)
    kmean = kmean[:, :, :tm // MOBA_BLOCK].reshape(b, s // MOBA_BLOCK, MOBA_WIDTH)
    return (*acts, kmean)


def _softmax(heads, scores, m_ref):
    stats = []
    for hh, s in zip(heads, scores):
        m_prev = m_ref[hh]
        m_new = jnp.maximum(m_prev, jnp.max(s, axis=0, keepdims=True))
        m_ref[hh] = m_new
        stats.append((jnp.exp2(m_prev - m_new), m_new))
    return [(alpha, jnp.exp2(s - m_new).astype(BF16)) for (alpha, m_new), s in zip(stats, scores)]


def _values(weighted, vt, l_ref, acc_ref):
    ones = jnp.ones((BF16_SUBLANES, vt.shape[1]), BF16)
    for hh, (alpha, p) in enumerate(weighted):
        rows = slice(hh * HEAD_DIM, (hh + 1) * HEAD_DIM)
        pv = _dot(jnp.concatenate([vt[rows, :], ones], axis=0), p)
        acc_ref[rows, :] = alpha * acc_ref[rows, :] + pv[:HEAD_DIM]
        l_ref[hh] = alpha * l_ref[hh] + pv[HEAD_DIM:HEAD_DIM + 1]


def _attend_tiles(i, scores_of, vt_ref, m_ref, l_ref, acc_ref):
    heads = tuple(range(HEADS_PER_STEP))

    def block(tiles):
        scores = [scores_of(kt, diag, heads) for kt, diag in tiles]
        for (kt, _), s in zip(tiles, scores):
            _values(_softmax(heads, s, m_ref), vt_ref[0, kt], l_ref, acc_ref)

    for extra in range(TRIP_TILES):
        @pl.when(i % TRIP_TILES == extra)
        def _(extra=extra):
            block([(i, True)] + [(i - extra + r, False) for r in range(extra)])

    def trip(j, c):
        block([(TRIP_TILES * j + r, False) for r in range(TRIP_TILES)])
        return c

    lax.fori_loop(0, i // TRIP_TILES, trip, 0)


def _attn_init(m_ref, l_ref, acc_ref):
    m_ref[...] = jnp.full(m_ref.shape, NEG_INF, F32)
    l_ref[...] = jnp.zeros_like(l_ref)
    acc_ref[...] = jnp.zeros_like(acc_ref)


def _attn_finish(o_ref, l_ref, acc_ref):
    out_t = jnp.concatenate(
        [acc_ref[hh * HEAD_DIM:(hh + 1) * HEAD_DIM, :] / l_ref[hh] for hh in range(HEADS_PER_STEP)],
        axis=0)
    o_ref[0] = out_t.T


def _block(a, g):
    return a[:, g * LANES:(g + 1) * LANES]


def _per_head(a, width):
    first = lax.broadcasted_iota(I32, (a.shape[0], LANES), 1) < width
    zero = jnp.zeros((a.shape[0], LANES), a.dtype)
    out = []
    for g in range(BLOCKS_PER_STEP):
        blk = _block(a, g)
        out += [jnp.where(first, blk, zero), jnp.where(first, zero, blk)]
    return out


def _key_le_query(tq):
    return lax.broadcasted_iota(I32, (tq, tq), 0) <= lax.broadcasted_iota(I32, (tq, tq), 1)


def _fox_kernel(q_ref, qa_ref, k_ref, ka_ref, vt_ref, o_ref, m_ref, l_ref, acc_ref):
    i = pl.program_id(2)
    tq = q_ref.shape[1]
    qa = qa_ref[0]
    lane = lax.broadcasted_iota(I32, qa.shape, 1) - pl.program_id(1) * (HEADS_PER_STEP * AUX_PER_HEAD)
    own_aux = lambda hh: jnp.logical_and(lane >= hh * AUX_PER_HEAD, lane < (hh + 1) * AUX_PER_HEAD)
    qq = [jnp.concatenate([qm, jnp.where(own_aux(hh), qa, jnp.zeros_like(qa))], axis=1)
          for hh, qm in enumerate(_per_head(q_ref[0], HEAD_DIM))]
    causal = _key_le_query(tq)
    _attn_init(m_ref, l_ref, acc_ref)

    def scores_of(kt, diag, heads):
        ks = pl.multiple_of(kt * tq, tq)
        ka = ka_ref[0, pl.ds(ks, tq), :]
        kk = {g: jnp.concatenate([k_ref[0, pl.ds(ks, tq), g * LANES:(g + 1) * LANES], ka], axis=1)
              for g in sorted({hh // HEADS_PER_BLOCK for hh in heads})}
        scores = [_dot_nt(kk[hh // HEADS_PER_BLOCK], qq[hh]) for hh in heads]
        if diag:
            scores = [jnp.where(causal, s, NEG_INF) for s in scores]
        return tuple(scores)

    _attend_tiles(i, scores_of, vt_ref, m_ref, l_ref, acc_ref)
    _attn_finish(o_ref, l_ref, acc_ref)


def _moba_kernel(q_ref, k_ref, hot_ref, vt_ref, kmean_ref, o_ref, m_ref, l_ref, acc_ref):
    i = pl.program_id(2)
    tq = q_ref.shape[1]
    nblk = kmean_ref.shape[1]
    qs = _per_head(q_ref[0], HEAD_DIM)
    causal = _key_le_query(tq)
    _attn_init(m_ref, l_ref, acc_ref)

    km_parts = jnp.concatenate(_split3(kmean_ref[0]), axis=0)
    blk = lax.broadcasted_iota(I32, (nblk, tq), 0).astype(F32)
    past = blk < i.astype(F32)
    masks = []
    for hh in range(HEADS_PER_STEP):
        pieces = _dot_nt(_block(km_parts, hh // HEADS_PER_BLOCK), qs[hh])
        gate = pieces[:nblk] + pieces[nblk:2 * nblk] + pieces[2 * nblk:]
        sel = jnp.zeros((nblk, tq), jnp.bool_)
        for _ in range(MOBA_TOPK):
            remaining = jnp.logical_and(past, jnp.logical_not(sel))
            g = jnp.where(remaining, gate, NEG_INF)
            first = jnp.min(jnp.where(g == jnp.max(g, axis=0, keepdims=True), blk, float(nblk)),
                            axis=0, keepdims=True)
            sel = jnp.logical_or(sel, jnp.logical_and(blk == first, remaining))
        masks.append(jnp.where(sel, 0.0, MASKED))
    unused = LANES - HEADS_PER_STEP * nblk
    masks += [jnp.zeros((unused, tq), F32)] if unused else []
    qa = jnp.concatenate(masks, axis=0).T.astype(BF16)
    lane = lax.broadcasted_iota(I32, qa.shape, 1)
    own = lambda hh: jnp.logical_and(lane >= hh * nblk, lane < (hh + 1) * nblk)
    qq = [jnp.concatenate([qs[hh], jnp.where(own(hh), qa, jnp.zeros_like(qa))], axis=1)
          for hh in range(HEADS_PER_STEP)]

    def scores_of(kt, diag, heads):
        ks = pl.multiple_of(kt * tq, tq)
        k = lambda hh: k_ref[0, pl.ds(ks, tq), (hh // HEADS_PER_BLOCK) * LANES:(hh // HEADS_PER_BLOCK + 1) * LANES]
        if diag:
            return tuple(jnp.where(causal, _dot_nt(k(hh), qs[hh]), NEG_INF) for hh in heads)
        hot = hot_ref[pl.ds(ks, tq), :]
        return tuple(_dot_nt(jnp.concatenate([k(hh), hot], axis=1), qq[hh]) for hh in heads)

    _attend_tiles(i, scores_of, vt_ref, m_ref, l_ref, acc_ref)
    _attn_finish(o_ref, l_ref, acc_ref)


def _attn_scratch(tq):
    return [pltpu.VMEM((HEADS_PER_STEP, 1, tq), F32), pltpu.VMEM((HEADS_PER_STEP, 1, tq), F32),
            pltpu.VMEM((STEP_LANES, tq), F32)]


def _attn_specs(s, tq):
    q_spec = pl.BlockSpec((1, tq, STEP_LANES), lambda bi, hb, i: (bi, i, hb))
    k_spec = pl.BlockSpec((1, s, STEP_LANES), lambda bi, hb, i: (bi, 0, hb))
    vt_spec = pl.BlockSpec((1, s // tq, STEP_LANES, tq), lambda bi, hb, i: (bi, 0, hb, 0))
    return q_spec, k_spec, vt_spec


def _fox(q, qa, k, ka, vt):
    b, s, width = q.shape
    tq = TQ
    q_spec, k_spec, vt_spec = _attn_specs(s, tq)
    return pl.pallas_call(
        _fox_kernel,
        grid=(b, width // STEP_LANES, s // tq),
        in_specs=[q_spec, pl.BlockSpec((1, tq, LANES), lambda bi, hb, i: (bi, i, 0)),
                  k_spec, pl.BlockSpec((1, s, LANES), lambda bi, hb, i: (bi, 0, 0)), vt_spec],
        out_specs=q_spec,
        out_shape=jax.ShapeDtypeStruct((b, s, width), F32),
        scratch_shapes=_attn_scratch(tq),
        compiler_params=_params(("arbitrary", "arbitrary", "arbitrary")),
        name="fox",
    )(q, qa, k, ka, vt)


def _moba(q, k, vt, kmean):
    b, s, width = q.shape
    tq = TQ
    nblk = kmean.shape[1]
    assert HEADS_PER_STEP * nblk <= LANES, "one aux lane per (head, key block)"
    lane = jnp.arange(LANES)
    hot = jnp.logical_and(lane[None, :] < HEADS_PER_STEP * nblk,
                          lane[None, :] % nblk == jnp.arange(s)[:, None] // MOBA_BLOCK).astype(BF16)
    q_spec, k_spec, vt_spec = _attn_specs(s, tq)
    return pl.pallas_call(
        _moba_kernel,
        grid=(b, width // STEP_LANES, s // tq),
        in_specs=[q_spec, k_spec, pl.BlockSpec((s, LANES), lambda bi, hb, i: (0, 0)), vt_spec,
                  pl.BlockSpec((1, nblk, STEP_LANES), lambda bi, hb, i: (bi, 0, hb))],
        out_specs=q_spec,
        out_shape=jax.ShapeDtypeStruct((b, s, width), F32),
        scratch_shapes=_attn_scratch(tq),
        compiler_params=_params(("arbitrary", "arbitrary", "arbitrary")),
        name="moba",
    )(q, k, hot, vt, kmean)


def _postattn_kernel(x_ref, fox_ref, moba_ref, gf_ref, gm_ref, wo_ref, gn_ref, wr_ref, br_ref,
                     x2_ref, h2_ref, wts_ref, route_ref, cnt_ref, carry_ref):
    t = pl.program_id(0)
    tm = x_ref.shape[0]
    fw = fox_ref.shape[1]
    mixed_f = _rms(fox_ref[...], gf_ref[...]).astype(BF16)
    mixed_m = _rms(moba_ref[...], gm_ref[...]).astype(BF16)
    x2 = x_ref[...] + _dot(mixed_f, wo_ref[:fw, :]) + _dot(mixed_m, wo_ref[fw:, :])
    x2_ref[...] = x2
    h2 = _rms(x2, gn_ref[...])
    _store_rows(h2_ref, h2)

    h_hi = h2.astype(BF16)
    h_lo = (h2 - h_hi.astype(F32)).astype(BF16)
    by_hi = _dot_nt(wr_ref[...], h_hi)
    logits = (by_hi[:ROUTE_ROWS] + by_hi[LANES:LANES + ROUTE_ROWS]
              + _dot_nt(wr_ref[:ROUTE_ROWS, :], h_lo) + br_ref[...])
    row = lax.broadcasted_iota(I32, (ROUTE_ROWS, tm), 0).astype(F32)

    def first_max(vals):
        mx = jnp.max(vals, axis=0, keepdims=True)
        return mx, jnp.min(jnp.where(vals == mx, row, float(ROUTE_ROWS)), axis=0, keepdims=True)

    gl = jnp.where(row < N_GROUPS, logits, NEG_INF)
    gmax, g_idx = first_max(gl)
    g_top = 1.0 / jnp.sum(jnp.exp(gl - gmax), axis=0, keepdims=True)
    e_lo = N_GROUPS + EXPERTS_PER_GROUP * g_idx
    el = jnp.where(jnp.logical_and(row >= e_lo, row < e_lo + EXPERTS_PER_GROUP), logits, NEG_INF)
    emax, i1 = first_max(el)
    esum = jnp.sum(jnp.exp(el - emax), axis=0, keepdims=True)
    e2max, i2 = first_max(jnp.where(row == i1, NEG_INF, el))
    p1 = 1.0 / esum
    p2 = jnp.exp(e2max - emax) / esum
    w1 = p1 / (p1 + p2) * g_top
    w2 = p2 / (p1 + p2) * g_top
    e1 = i1 - N_GROUPS
    e2 = i2 - N_GROUPS
    zeros = jnp.zeros((SUBLANES - 2, tm), F32)
    wts_ref[...] = jnp.concatenate([w1, w2, zeros], axis=0).T[:, :2]

    @pl.when(t == 0)
    def _():
        carry_ref[...] = jnp.zeros_like(carry_ref)

    earlier = jnp.where(lax.broadcasted_iota(I32, (tm, tm), 0) < lax.broadcasted_iota(I32, (tm, tm), 1),
                        1.0, 0.0).astype(BF16)
    hit1 = row == e1
    hit2 = row == e2
    oh1 = jnp.where(hit1, 1.0, 0.0)
    oh2 = jnp.where(hit2, 1.0, 0.0)
    tot1 = jnp.sum(oh1, axis=1, keepdims=True)
    tot2 = jnp.sum(oh2, axis=1, keepdims=True)
    base = carry_ref[...]
    before = _dot(jnp.concatenate([oh1, oh2], axis=0).astype(BF16), earlier)
    before1 = before[:ROUTE_ROWS] + base
    before2 = before[ROUTE_ROWS:] + (base + tot1)
    r1 = jnp.sum(jnp.where(hit1, before1, 0.0), axis=0, keepdims=True)
    r2 = jnp.sum(jnp.where(hit2, before2, 0.0), axis=0, keepdims=True)
    route_ref[...] = jnp.concatenate([e1, e2, r1, r2, zeros[:SUBLANES - 4]], axis=0).astype(I32)
    total = base + tot1 + tot2
    carry_ref[...] = total
    cnt_ref[...] = total.astype(I32)


def _postattn(x, fox, moba, gf, gm, wo, gn, wr, br):
    t, d = x.shape
    tm = TM_PROJ
    fw = fox.shape[1]
    const = lambda i: (0, 0)
    rows = lambda i: (i, 0)
    return pl.pallas_call(
        _postattn_kernel,
        grid=(t // tm,),
        in_specs=[pl.BlockSpec((tm, d), rows), pl.BlockSpec((tm, fw), rows),
                  pl.BlockSpec((tm, moba.shape[1]), rows),
                  pl.BlockSpec((1, fw), const), pl.BlockSpec((1, moba.shape[1]), const),
                  pl.BlockSpec(wo.shape, const), pl.BlockSpec((1, d), const),
                  pl.BlockSpec(wr.shape, const), pl.BlockSpec((ROUTE_ROWS, 1), const)],
        out_specs=[pl.BlockSpec((tm, d), rows), pl.BlockSpec((tm * SUBLANES, LANES), rows),
                   pl.BlockSpec((tm, 2), rows), pl.BlockSpec((SUBLANES, tm), lambda i: (0, i)),
                   pl.BlockSpec((ROUTE_ROWS, 1), const)],
        out_shape=[jax.ShapeDtypeStruct((t, d), F32), jax.ShapeDtypeStruct((t * SUBLANES, LANES), F32),
                   jax.ShapeDtypeStruct((t, 2), F32), jax.ShapeDtypeStruct((SUBLANES, t), I32),
                   jax.ShapeDtypeStruct((ROUTE_ROWS, 1), I32)],
        scratch_shapes=[pltpu.VMEM((ROUTE_ROWS, 1), F32)],
        compiler_params=_params(("arbitrary",)),
        name="postattn",
    )(x, fox, moba, gf, gm, wo, gn, wr, br)


def _store_rows(ref, val):
    for g in range(SUBLANES):
        ref[pl.ds(g, val.shape[0], stride=SUBLANES), :] = val[:, g * LANES:(g + 1) * LANES]


def _load_rows(ref):
    tokens = ref.shape[0] // SUBLANES
    return jnp.concatenate([ref[pl.ds(g, tokens, stride=SUBLANES), :] for g in range(SUBLANES)], axis=1)


def _row_copy(src, src_row, dst, dst_row, sem, tokens=1):
    window = lambda r: pl.ds(pl.multiple_of(r * SUBLANES, SUBLANES), tokens * SUBLANES)
    return pltpu.make_async_copy(src.at[window(src_row)], dst.at[window(dst_row)], sem)


RING = 3
ZERO_TOKENS = 256


def _dispatch_kernel(pad_start_ref, pad_len_ref, dest_ref, h_ref, xs_ref,
                     ring_ref, zero_ref, fetch_sems, scatter_sems, pad_sem):
    i = pl.program_id(0)
    last = pl.num_programs(0) - 1
    tm = dest_ref.shape[2] // 2
    tile_rows = tm * SUBLANES
    row_copy = _row_copy
    zero_tokens = zero_ref.shape[0] // SUBLANES

    def fetch(tile):
        start = pl.multiple_of(tile * tile_rows, tile_rows)
        slot = lax.rem(tile, RING)
        return pltpu.make_async_copy(h_ref.at[pl.ds(start, tile_rows)], ring_ref.at[slot],
                                     fetch_sems.at[slot])

    def zero_fill(e, act):
        start, n = pad_start_ref[e], pad_len_ref[e]
        whole = lax.shift_right_logical(n, zero_tokens.bit_length() - 1)
        rest = jnp.bitwise_and(n, zero_tokens - 1)

        def chunk(c, carry):
            act(_row_copy(zero_ref, 0, xs_ref, start + c * zero_tokens, pad_sem, tokens=zero_tokens))
            return carry

        lax.fori_loop(0, whole, chunk, 0)
        bit = zero_tokens // 2
        while bit:
            @pl.when(jnp.bitwise_and(rest, bit) != 0)
            def _(bit=bit):
                above = jnp.bitwise_and(rest, -2 * bit)
                act(_row_copy(zero_ref, 0, xs_ref, start + whole * zero_tokens + above, pad_sem, tokens=bit))
            bit //= 2

    @pl.when(i == 0)
    def _():
        fetch(0).start()
        zero_ref[...] = jnp.zeros_like(zero_ref)
        n_regions = pad_start_ref.shape[0]
        lax.fori_loop(0, n_regions, lambda e, c: (zero_fill(e, lambda cp: cp.start()), c)[1], 0)
        lax.fori_loop(0, n_regions, lambda e, c: (zero_fill(e, lambda cp: cp.wait()), c)[1], 0)

    @pl.when(i < last)
    def _():
        fetch(i + 1).start()

    fetch(i).wait()
    src = ring_ref.at[lax.rem(i, RING)]

    def issue(r, c):
        for k in range(2):
            row_copy(src, r, xs_ref, dest_ref[0, 0, k * tm + r], scatter_sems.at[i % 2]).start(priority=k)
        return c

    lax.fori_loop(0, tm, issue, 0, unroll=ISSUE_UNROLL)

    def drain(parity):
        for _ in range(2):
            pltpu.make_async_copy(src, src, scatter_sems.at[parity]).wait()

    @pl.when(i > 0)
    def _():
        drain((i - 1) % 2)

    @pl.when(i == last)
    def _():
        drain(i % 2)


def _dispatch(pad_start, pad_len, dest3, h2, n_rows):
    tm = dest3.shape[2] // 2
    grid_spec = pltpu.PrefetchScalarGridSpec(
        num_scalar_prefetch=2,
        grid=(dest3.shape[0],),
        in_specs=[pl.BlockSpec((1, 1, 2 * tm), lambda i, ps, pn: (i, 0, 0), memory_space=pltpu.SMEM),
                  pl.BlockSpec(memory_space=pl.ANY)],
        out_specs=pl.BlockSpec(memory_space=pl.ANY),
        scratch_shapes=[pltpu.VMEM((RING, tm * SUBLANES, LANES), F32),
                        pltpu.VMEM((ZERO_TOKENS * SUBLANES, LANES), F32),
                        pltpu.SemaphoreType.DMA((RING,)), pltpu.SemaphoreType.DMA((2,)),
                        pltpu.SemaphoreType.DMA(())],
    )
    return pl.pallas_call(
        _dispatch_kernel,
        grid_spec=grid_spec,
        out_shape=jax.ShapeDtypeStruct((n_rows * SUBLANES, LANES), F32),
        compiler_params=_params(("arbitrary",)),
        name="dispatch",
    )(pad_start, pad_len, dest3, h2)


def _experts_kernel(te_ref, ts_ref, nv_ref, xs_ref, wg_ref, wu_ref, wd_ref, ys_ref,
                    wgb_ref, wub_ref, wdb_ref):
    del ts_ref
    t = pl.program_id(0)

    @pl.when(jnp.logical_or(t == 0, te_ref[t] != te_ref[jnp.maximum(t - 1, 0)]))
    def _():
        wgb_ref[...] = wg_ref[0].astype(BF16)
        wub_ref[...] = wu_ref[0].astype(BF16)
        wdb_ref[...] = wd_ref[0].astype(BF16)

    @pl.when(t < nv_ref[0])
    def _():
        xb = _load_rows(xs_ref).astype(BF16)
        a = _dot(xb, wgb_ref[...])
        u = _dot(xb, wub_ref[...])
        act = (a * jax.nn.sigmoid(a) * u).astype(BF16)
        _store_rows(ys_ref, _dot(act, wdb_ref[...]))

    @pl.when(t >= nv_ref[0])
    def _():
        ys_ref[...] = jnp.zeros_like(ys_ref)


def _experts(tile_expert, tile_src, n_valid, xs, wg, wu, wd):
    tm = TM_EXPERT
    n_tiles = xs.shape[0] // (tm * SUBLANES)
    _, d, f = wg.shape
    row_block = (tm * SUBLANES, LANES)
    grid_spec = pltpu.PrefetchScalarGridSpec(
        num_scalar_prefetch=3,
        grid=(n_tiles,),
        in_specs=[pl.BlockSpec(row_block, lambda t, te, ts, nv: (ts[t], 0)),
                  pl.BlockSpec((1, d, f), lambda t, te, ts, nv: (te[t], 0, 0)),
                  pl.BlockSpec((1, d, f), lambda t, te, ts, nv: (te[t], 0, 0)),
                  pl.BlockSpec((1, f, d), lambda t, te, ts, nv: (te[t], 0, 0))],
        out_specs=pl.BlockSpec(row_block, lambda t, te, ts, nv: (t, 0)),
        scratch_shapes=[pltpu.VMEM((d, f), BF16), pltpu.VMEM((d, f), BF16), pltpu.VMEM((f, d), BF16)],
    )
    return pl.pallas_call(
        _experts_kernel,
        grid_spec=grid_spec,
        out_shape=jax.ShapeDtypeStruct((n_tiles * tm * SUBLANES, LANES), F32),
        compiler_params=_params(("arbitrary",)),
        name="experts",
    )(tile_expert, tile_src, n_valid, xs, wg, wu, wd)


def _combine_kernel(dest_ref, next_ref, x2_ref, wts_ref, g_ref, ys_ref, o_ref, buf_ref, sems):
    i = pl.program_id(0)
    tm = dest_ref.shape[2] // 2
    slot = i % 2

    def gather(d_ref, to):
        def issue(r, c):
            for k in range(2):
                _row_copy(ys_ref, d_ref[0, 0, k * tm + r], buf_ref.at[to, k], r,
                          sems.at[to]).start(priority=k)
            return c

        lax.fori_loop(0, tm, issue, 0, unroll=ISSUE_UNROLL)

    @pl.when(i == 0)
    def _():
        gather(dest_ref, 0)

    @pl.when(i + 1 < pl.num_programs(0))
    def _():
        gather(next_ref, 1 - slot)

    pltpu.make_async_copy(buf_ref.at[slot], buf_ref.at[slot], sems.at[slot]).wait()
    w = wts_ref[...]
    y = (x2_ref[...] + w[:, 0:1] * _load_rows(buf_ref.at[slot, 0])
         + w[:, 1:2] * _load_rows(buf_ref.at[slot, 1]))
    o_ref[...] = _rms(y, g_ref[...])


def _combine(dest3, x2, wts, g, ys):
    t, d = x2.shape
    tm = dest3.shape[2] // 2
    rows = lambda i: (i, 0)
    n = t // tm
    return pl.pallas_call(
        _combine_kernel,
        grid=(n,),
        in_specs=[pl.BlockSpec((1, 1, 2 * tm), lambda i: (i, 0, 0), memory_space=pltpu.SMEM),
                  pl.BlockSpec((1, 1, 2 * tm), lambda i: (jnp.minimum(i + 1, n - 1), 0, 0),
                               memory_space=pltpu.SMEM),
                  pl.BlockSpec((tm, d), rows), pl.BlockSpec((tm, 2), rows),
                  pl.BlockSpec((1, d), lambda i: (0, 0)),
                  pl.BlockSpec(memory_space=pl.ANY)],
        out_specs=pl.BlockSpec((tm, d), rows),
        out_shape=jax.ShapeDtypeStruct((t, d), F32),
        scratch_shapes=[pltpu.VMEM((2, 2, tm * SUBLANES, LANES), F32), pltpu.SemaphoreType.DMA((2,))],
        compiler_params=_params(("arbitrary",)),
        name="combine",
    )(dest3, dest3, x2, wts, g, ys)


def _rotary_tables(seq):
    half = ROPE_DIM // 2
    inv_freq = ROPE_THETA ** (-jnp.arange(half, dtype=F32) / half)
    ang = jnp.arange(seq, dtype=F32)[:, None] * inv_freq[None, :]
    cos, sin = jnp.cos(ang), jnp.sin(ang)
    ones = jnp.ones((seq, HEAD_DIM - ROPE_DIM), F32)
    zeros = jnp.zeros((seq, HEAD_DIM - ROPE_DIM), F32)
    zh = jnp.zeros((seq, half), F32)
    cosm = jnp.concatenate([cos, cos, ones], axis=1)
    sina = jnp.concatenate([-sin, zh, zeros], axis=1)
    sinb = jnp.concatenate([zh, sin, zeros], axis=1)
    tile = lambda a: jnp.tile(a, (1, HEADS_PER_BLOCK))
    return tile(cosm), tile(sina), tile(sinb)


def _pad_lanes(a):
    return jnp.pad(a, ((0, 0), (0, LANES - a.shape[1])))


def kernel(x, norm_mix_g, w_in, b_forget, fox_out_g, moba_out_g, w_out, norm_ffn_g, w_router_group,
           b_router_group, w_router_expert, b_router_expert, w_gate, w_up, w_down, norm_final_g):
    b, s, d = x.shape
    t = b * s
    assert w_in.shape[0] == 1, "the closing RMSNorm is fused into the only layer's combine step"
    cosm, sina, sinb = _rotary_tables(s)
    n_tiles = (2 * t) // TM_EXPERT + N_EXPERTS
    fw3 = 3 * FOX_WIDTH
    m0 = fw3 + N_FOX_HEADS
    wl = w_in[0]
    aux_head = jnp.arange(AUX_PER_HEAD * N_FOX_HEADS) // AUX_PER_HEAD
    w_main = jnp.concatenate([wl[:, :2 * FOX_WIDTH], wl[:, m0:m0 + 2 * MOBA_WIDTH],
                              _pad_lanes(wl[:, fw3:m0][:, aux_head])],
                             axis=1).astype(BF16)
    w_vt = jnp.stack([wl[:, 2 * FOX_WIDTH:fw3].T, wl[:, m0 + 2 * MOBA_WIDTH:].T]).astype(BF16)
    b_logit = _pad_lanes(b_forget[0][None, aux_head])
    fq, fqa, fk, fka, fvt, mq, mk, mvt, kmean = _inproj(
        x, norm_mix_g[0][None, :], w_main, w_vt, b_logit, cosm, sina, sinb)
    fox = _fox(fq, fqa, fk, fka, fvt)
    moba = _moba(mq, mk, mvt, kmean)

    w_router = _pad_lanes(jnp.concatenate(
        [w_router_group[0], w_router_expert[0].reshape(d, N_EXPERTS)], axis=1))
    w_router_hi = w_router.astype(BF16)
    w_router = jnp.concatenate([w_router_hi, (w_router - w_router_hi.astype(F32)).astype(BF16)], axis=1).T
    b_router = _pad_lanes(jnp.concatenate(
        [b_router_group[0], b_router_expert[0].reshape(N_EXPERTS)])[None, :])[:, :ROUTE_ROWS].T
    x2, h2, wts, route, counts = _postattn(
        x.reshape(t, d), fox.reshape(t, FOX_WIDTH), moba.reshape(t, MOBA_WIDTH),
        fox_out_g[0][None, :], moba_out_g[0][None, :], w_out[0].astype(BF16),
        norm_ffn_g[0][None, :], w_router, b_router)

    counts = counts[:N_EXPERTS, 0]
    padded = (counts + TM_EXPERT - 1) // TM_EXPERT * TM_EXPERT
    ends = jnp.cumsum(padded)
    starts = ends - padded
    expert_ids = jnp.arange(N_EXPERTS, dtype=I32)[:, None, None]
    dest = route[2:4] + jnp.sum(jnp.where(route[None, 0:2] == expert_ids, starts[:, None, None], 0),
                                axis=0)
    dest3 = dest.reshape(2, t // TM_ROWS, TM_ROWS).transpose(1, 0, 2).reshape(t // TM_ROWS, 1, 2 * TM_ROWS)
    n_rows = n_tiles * TM_EXPERT
    pad_start = jnp.concatenate([starts + counts, ends[-1:]]).astype(I32)
    pad_len = jnp.concatenate([padded - counts, n_rows - ends[-1:]]).astype(I32)
    n_valid = ends[-1] // TM_EXPERT
    tile_src = jnp.minimum(jnp.arange(n_tiles, dtype=I32), n_valid - 1)
    tile_expert = jnp.sum(ends[None, :] <= (tile_src * TM_EXPERT)[:, None], axis=1).astype(I32)
    xs = _dispatch(pad_start, pad_len, dest3, h2, n_rows)
    ys = _experts(tile_expert, tile_src, n_valid.reshape(1).astype(I32), xs,
                  w_gate[0], w_up[0], w_down[0])
    return _combine(dest3, x2, wts, norm_final_g[None, :], ys).reshape(b, s, d)
```

```python
import math

import jax
import jax.numpy as jnp
from jax import lax
from jax.experimental import pallas as pl
from jax.experimental.pallas import tpu as pltpu

F32 = jnp.float32
BF16 = jnp.bfloat16
I32 = jnp.int32

HEAD_DIM = 64
N_FOX_HEADS = 8
N_MOBA_HEADS = 8
FOX_WIDTH = N_FOX_HEADS * HEAD_DIM
MOBA_WIDTH = N_MOBA_HEADS * HEAD_DIM
MOBA_BLOCK = 256
MOBA_TOPK = 3
ROPE_THETA = 500000.0
ROPE_DIM = HEAD_DIM // 4
N_GROUPS = 4
EXPERTS_PER_GROUP = 8
N_EXPERTS = N_GROUPS * EXPERTS_PER_GROUP
EPS = 1e-6

LANES = 128
SUBLANES = 8
BF16_SUBLANES = 16
LOG2_E = math.log2(math.e)
HEADS_PER_BLOCK = LANES // HEAD_DIM
BLOCKS_PER_STEP = 4
HEADS_PER_STEP = HEADS_PER_BLOCK * BLOCKS_PER_STEP
STEP_LANES = LANES * BLOCKS_PER_STEP
VMEM_LIMIT = 56 * 1024 * 1024
AUX_PER_HEAD = 6

TM_IN = 1024
TM_PROJ = 512
TQ = 256
TM_EXPERT = 512
TM_SCATTER = 512
TM_GATHER = 256
ISSUE_UNROLL = 16
ROUTE_ROWS = 64
TRIP_TILES = 4

NEG_INF = float("-inf")
MASKED = -1e30


def _params(sem):
    return pltpu.CompilerParams(dimension_semantics=sem, vmem_limit_bytes=VMEM_LIMIT)


def _rms(x, g):
    return x * lax.rsqrt(jnp.mean(x * x, axis=-1, keepdims=True) + EPS) * g


def _split3(x):
    hi = x.astype(BF16)
    r = x - hi.astype(F32)
    mid = r.astype(BF16)
    lo = (r - mid.astype(F32)).astype(BF16)
    return hi, mid, lo


def _dot(a, b):
    return jnp.dot(a, b, preferred_element_type=F32)


def _dot_nt(a, b):
    return lax.dot_general(a, b, (((1,), (1,)), ((), ())), preferred_element_type=F32)


def _inproj_kernel(x_ref, g_ref, w_ref, wvt_ref, bf_ref, kind_ref, cosm_ref, sina_ref, sinb_ref,
                   fq_ref, fqa_ref, fk_ref, fka_ref, fvt_ref, mq_ref, mk_ref, mvt_ref, kmean_ref,
                   carry_ref):
    j = pl.program_id(1)
    tm = x_ref.shape[1]
    tk = fvt_ref.shape[3]
    h = _rms(x_ref[0], g_ref[...]).astype(BF16)
    scale = HEAD_DIM ** -0.5 * LOG2_E

    def proj(seg):
        return _dot(h, w_ref[:, seg * FOX_WIDTH:(seg + 1) * FOX_WIDTH])

    fq_ref[0] = (proj(0) * scale).astype(BF16)
    fk_ref[0] = proj(1).astype(BF16)

    for vt_ref, seg in ((fvt_ref, 0), (mvt_ref, 1)):
        vt = _dot_nt(wvt_ref[seg], h).astype(BF16)
        for r in range(tm // tk):
            vt_ref[0, r] = vt[:, r * tk:(r + 1) * tk]

    cosm, sina, sinb = cosm_ref[...], sina_ref[...], sinb_ref[...]

    def rotary(t):
        outs = []
        for g in range(MOBA_WIDTH // LANES):
            tg = t[:, g * LANES:(g + 1) * LANES]
            outs.append(tg * cosm + pltpu.roll(tg, LANES - ROPE_DIM // 2, 1) * sina
                        + pltpu.roll(tg, ROPE_DIM // 2, 1) * sinb)
        return jnp.concatenate(outs, axis=1)

    mq_ref[0] = (rotary(proj(2)) * scale).astype(BF16)
    mk_z = _dot(h, w_ref[:, 3 * FOX_WIDTH:])
    mk = rotary(mk_z[:, :MOBA_WIDTH])
    mk_ref[0] = mk.astype(BF16)
    nblk_tile = tm // MOBA_BLOCK
    means = [jnp.mean(mk[r * MOBA_BLOCK:(r + 1) * MOBA_BLOCK], axis=0, keepdims=True)
             for r in range(nblk_tile)]
    means += [jnp.zeros_like(means[0])] * (kmean_ref.shape[2] - nblk_tile)
    kmean_ref[0, 0] = jnp.concatenate(means, axis=0)

    z = mk_z[:, MOBA_WIDTH:] + bf_ref[...]
    log_f = jnp.minimum(z, 0.0) - jnp.log1p(jnp.exp(-jnp.abs(z)))

    @pl.when(j == 0)
    def _():
        carry_ref[...] = jnp.zeros_like(carry_ref)

    half = tm // 2
    row = lax.broadcasted_iota(I32, (half, half), 0)
    col = lax.broadcasted_iota(I32, (half, half), 1)
    tri = jnp.where(row >= col, 1.0, 0.0).astype(BF16)
    pieces = jnp.concatenate(_split3(log_f), axis=1)
    carry = carry_ref[...]
    cs = []
    for r in range(2):
        local = _dot(tri, pieces[r * half:(r + 1) * half])
        cs.append(local[:, :LANES] + local[:, LANES:2 * LANES] + local[:, 2 * LANES:] + carry)
        carry = cs[-1][half - 1:half, :]
    carry_ref[...] = carry
    c = jnp.concatenate(cs, axis=0) * LOG2_E

    hi = c.astype(BF16).astype(F32)
    mid = (c - hi).astype(BF16).astype(F32)
    lo = c - hi - mid
    kind = kind_ref[...]
    one = jnp.where(kind < AUX_PER_HEAD, 1.0, 0.0)
    pick = lambda base: jnp.where(kind == base, hi, jnp.where(kind == base + 1, mid,
                                  jnp.where(kind == base + 2, lo, 0.0)))
    fqa_ref[0] = (pick(3) + jnp.where(kind < 3, one, 0.0)).astype(BF16)
    fka_ref[0] = (jnp.where(kind >= 3, one, 0.0) - pick(0)).astype(BF16)


def _inproj(x, g, w_main, w_vt, b_logit, cosm, sina, sinb):
    b, s, d = x.shape
    tm, tk = TM_IN, TQ
    lane = jnp.arange(LANES, dtype=I32)
    kind = jnp.where(lane < AUX_PER_HEAD * N_FOX_HEADS, lane % AUX_PER_HEAD, AUX_PER_HEAD)[None, :]
    act = jax.ShapeDtypeStruct((b, s, FOX_WIDTH), BF16)
    aux = jax.ShapeDtypeStruct((b, s, LANES), BF16)
    vt = jax.ShapeDtypeStruct((b, s // tk, FOX_WIDTH, tk), BF16)
    out_shape = [act, aux, act, aux, vt, act, act, vt,
                 jax.ShapeDtypeStruct((b, s // tm, SUBLANES, MOBA_WIDTH), F32)]
    act_spec = pl.BlockSpec((1, tm, FOX_WIDTH), lambda bi, j: (bi, j, 0))
    aux_spec = pl.BlockSpec((1, tm, LANES), lambda bi, j: (bi, j, 0))
    vt_spec = pl.BlockSpec((1, tm // tk, FOX_WIDTH, tk), lambda bi, j: (bi, j, 0, 0))
    tab_spec = pl.BlockSpec((tm, LANES), lambda bi, j: (j, 0))
    const2 = lambda bi, j: (0, 0)
    const3 = lambda bi, j: (0, 0, 0)
    *acts, kmean = pl.pallas_call(
        _inproj_kernel,
        grid=(b, s // tm),
        in_specs=[pl.BlockSpec((1, tm, d), lambda bi, j: (bi, j, 0)),
                  pl.BlockSpec((1, d), const2),
                  pl.BlockSpec(w_main.shape, const2),
                  pl.BlockSpec(w_vt.shape, const3),
                  pl.BlockSpec((1, LANES), const2), pl.BlockSpec((1, LANES), const2),
                  tab_spec, tab_spec, tab_spec],
        out_specs=[act_spec, aux_spec, act_spec, aux_spec, vt_spec, act_spec, act_spec, vt_spec,
                   pl.BlockSpec((1, 1, SUBLANES, MOBA_WIDTH), lambda bi, j: (bi, j, 0, 0))],
        out_shape=out_shape,
        scratch_shapes=[pltpu.VMEM((1, LANES), F32)],
        compiler_params=_params(("arbitrary", "arbitrary")),
        name="inproj",
    )(x, g, w_main, w_vt, b_logit, kind, cosm, sina, sinb)
    kmean = kmean[:, :, :tm // MOBA_BLOCK].reshape(b, s // MOBA_BLOCK, MOBA_WIDTH)
    return (*acts, kmean)


def _softmax(heads, scores, m_ref):
    stats = []
    for hh, s in zip(heads, scores):
        m_prev = m_ref[hh]
        m_new = jnp.maximum(m_prev, jnp.max(s, axis=0, keepdims=True))
        m_ref[hh] = m_new
        stats.append((jnp.exp2(m_prev - m_new), m_new))
    return [(alpha, jnp.exp2(s - m_new).astype(BF16)) for (alpha, m_new), s in zip(stats, scores)]


def _values(weighted, vt, l_ref, acc_ref):
    ones = jnp.ones((BF16_SUBLANES, vt.shape[1]), BF16)
    for hh, (alpha, p) in enumerate(weighted):
        rows = slice(hh * HEAD_DIM, (hh + 1) * HEAD_DIM)
        pv = _dot(jnp.concatenate([vt[rows, :], ones], axis=0), p)
        acc_ref[rows, :] = alpha * acc_ref[rows, :] + pv[:HEAD_DIM]
        l_ref[hh] = alpha * l_ref[hh] + pv[HEAD_DIM:HEAD_DIM + 1]


def _attend_tiles(i, scores_of, vt_ref, m_ref, l_ref, acc_ref):
    heads = tuple(range(HEADS_PER_STEP))

    def block(tiles):
        scores = [scores_of(kt, diag, heads) for kt, diag in tiles]
        for (kt, _), s in zip(tiles, scores):
            _values(_softmax(heads, s, m_ref), vt_ref[0, kt], l_ref, acc_ref)

    for extra in range(TRIP_TILES):
        @pl.when(i % TRIP_TILES == extra)
        def _(extra=extra):
            block([(i, True)] + [(i - extra + r, False) for r in range(extra)])

    def trip(j, c):
        block([(TRIP_TILES * j + r, False) for r in range(TRIP_TILES)])
        return c

    lax.fori_loop(0, i // TRIP_TILES, trip, 0)


def _attn_init(m_ref, l_ref, acc_ref):
    m_ref[...] = jnp.full(m_ref.shape, NEG_INF, F32)
    l_ref[...] = jnp.zeros_like(l_ref)
    acc_ref[...] = jnp.zeros_like(acc_ref)


def _attn_finish(o_ref, l_ref, acc_ref):
    out_t = jnp.concatenate(
        [acc_ref[hh * HEAD_DIM:(hh + 1) * HEAD_DIM, :] / l_ref[hh] for hh in range(HEADS_PER_STEP)],
        axis=0)
    o_ref[0] = out_t.T


def _block(a, g):
    return a[:, g * LANES:(g + 1) * LANES]


def _per_head(a, width):
    first = lax.broadcasted_iota(I32, (a.shape[0], LANES), 1) < width
    zero = jnp.zeros((a.shape[0], LANES), a.dtype)
    out = []
    for g in range(BLOCKS_PER_STEP):
        blk = _block(a, g)
        out += [jnp.where(first, blk, zero), jnp.where(first, zero, blk)]
    return out


def _key_le_query(tq):
    return lax.broadcasted_iota(I32, (tq, tq), 0) <= lax.broadcasted_iota(I32, (tq, tq), 1)


def _fox_kernel(q_ref, qa_ref, k_ref, ka_ref, vt_ref, o_ref, m_ref, l_ref, acc_ref):
    i = pl.program_id(2)
    tq = q_ref.shape[1]
    qa = qa_ref[0]
    lane = lax.broadcasted_iota(I32, qa.shape, 1) - pl.program_id(1) * (HEADS_PER_STEP * AUX_PER_HEAD)
    own_aux = lambda hh: jnp.logical_and(lane >= hh * AUX_PER_HEAD, lane < (hh + 1) * AUX_PER_HEAD)
    qq = [jnp.concatenate([qm, jnp.where(own_aux(hh), qa, jnp.zeros_like(qa))], axis=1)
          for hh, qm in enumerate(_per_head(q_ref[0], HEAD_DIM))]
    causal = _key_le_query(tq)
    _attn_init(m_ref, l_ref, acc_ref)

    def scores_of(kt, diag, heads):
        ks = pl.multiple_of(kt * tq, tq)
        ka = ka_ref[0, pl.ds(ks, tq), :]
        kk = {g: jnp.concatenate([k_ref[0, pl.ds(ks, tq), g * LANES:(g + 1) * LANES], ka], axis=1)
              for g in sorted({hh // HEADS_PER_BLOCK for hh in heads})}
        scores = [_dot_nt(kk[hh // HEADS_PER_BLOCK], qq[hh]) for hh in heads]
        if diag:
            scores = [jnp.where(causal, s, NEG_INF) for s in scores]
        return tuple(scores)

    _attend_tiles(i, scores_of, vt_ref, m_ref, l_ref, acc_ref)
    _attn_finish(o_ref, l_ref, acc_ref)


def _moba_kernel(q_ref, k_ref, hot_ref, vt_ref, kmean_ref, o_ref, m_ref, l_ref, acc_ref):
    i = pl.program_id(2)
    tq = q_ref.shape[1]
    nblk = kmean_ref.shape[1]
    qs = _per_head(q_ref[0], HEAD_DIM)
    causal = _key_le_query(tq)
    _attn_init(m_ref, l_ref, acc_ref)

    km_parts = jnp.concatenate(_split3(kmean_ref[0]), axis=0)
    blk = lax.broadcasted_iota(I32, (nblk, tq), 0).astype(F32)
    past = blk < i.astype(F32)
    masks = []
    for hh in range(HEADS_PER_STEP):
        pieces = _dot_nt(_block(km_parts, hh // HEADS_PER_BLOCK), qs[hh])
        gate = pieces[:nblk] + pieces[nblk:2 * nblk] + pieces[2 * nblk:]
        sel = jnp.zeros((nblk, tq), jnp.bool_)
        for _ in range(MOBA_TOPK):
            remaining = jnp.logical_and(past, jnp.logical_not(sel))
            g = jnp.where(remaining, gate, NEG_INF)
            first = jnp.min(jnp.where(g == jnp.max(g, axis=0, keepdims=True), blk, float(nblk)),
                            axis=0, keepdims=True)
            sel = jnp.logical_or(sel, jnp.logical_and(blk == first, remaining))
        masks.append(jnp.where(sel, 0.0, MASKED))
    unused = LANES - HEADS_PER_STEP * nblk
    masks += [jnp.zeros((unused, tq), F32)] if unused else []
    qa = jnp.concatenate(masks, axis=0).T.astype(BF16)
    lane = lax.broadcasted_iota(I32, qa.shape, 1)
    own = lambda hh: jnp.logical_and(lane >= hh * nblk, lane < (hh + 1) * nblk)
    qq = [jnp.concatenate([qs[hh], jnp.where(own(hh), qa, jnp.zeros_like(qa))], axis=1)
          for hh in range(HEADS_PER_STEP)]

    def scores_of(kt, diag, heads):
        ks = pl.multiple_of(kt * tq, tq)
        k = lambda hh: k_ref[0, pl.ds(ks, tq), (hh // HEADS_PER_BLOCK) * LANES:(hh // HEADS_PER_BLOCK + 1) * LANES]
        if diag:
            return tuple(jnp.where(causal, _dot_nt(k(hh), qs[hh]), NEG_INF) for hh in heads)
        hot = hot_ref[pl.ds(ks, tq), :]
        return tuple(_dot_nt(jnp.concatenate([k(hh), hot], axis=1), qq[hh]) for hh in heads)

    _attend_tiles(i, scores_of, vt_ref, m_ref, l_ref, acc_ref)
    _attn_finish(o_ref, l_ref, acc_ref)


def _attn_scratch(tq):
    return [pltpu.VMEM((HEADS_PER_STEP, 1, tq), F32), pltpu.VMEM((HEADS_PER_STEP, 1, tq), F32),
            pltpu.VMEM((STEP_LANES, tq), F32)]


def _attn_specs(s, tq):
    q_spec = pl.BlockSpec((1, tq, STEP_LANES), lambda bi, hb, i: (bi, i, hb))
    k_spec = pl.BlockSpec((1, s, STEP_LANES), lambda bi, hb, i: (bi, 0, hb))
    vt_spec = pl.BlockSpec((1, s // tq, STEP_LANES, tq), lambda bi, hb, i: (bi, 0, hb, 0))
    return q_spec, k_spec, vt_spec


def _fox(q, qa, k, ka, vt):
    b, s, width = q.shape
    tq = TQ
    q_spec, k_spec, vt_spec = _attn_specs(s, tq)
    return pl.pallas_call(
        _fox_kernel,
        grid=(b, width // STEP_LANES, s // tq),
        in_specs=[q_spec, pl.BlockSpec((1, tq, LANES), lambda bi, hb, i: (bi, i, 0)),
                  k_spec, pl.BlockSpec((1, s, LANES), lambda bi, hb, i: (bi, 0, 0)), vt_spec],
        out_specs=q_spec,
        out_shape=jax.ShapeDtypeStruct((b, s, width), F32),
        scratch_shapes=_attn_scratch(tq),
        compiler_params=_params(("arbitrary", "arbitrary", "arbitrary")),
        name="fox",
    )(q, qa, k, ka, vt)


def _moba(q, k, vt, kmean):
    b, s, width = q.shape
    tq = TQ
    nblk = kmean.shape[1]
    assert HEADS_PER_STEP * nblk <= LANES, "one aux lane per (head, key block)"
    lane = jnp.arange(LANES)
    hot = jnp.logical_and(lane[None, :] < HEADS_PER_STEP * nblk,
                          lane[None, :] % nblk == jnp.arange(s)[:, None] // MOBA_BLOCK).astype(BF16)
    q_spec, k_spec, vt_spec = _attn_specs(s, tq)
    return pl.pallas_call(
        _moba_kernel,
        grid=(b, width // STEP_LANES, s // tq),
        in_specs=[q_spec, k_spec, pl.BlockSpec((s, LANES), lambda bi, hb, i: (0, 0)), vt_spec,
                  pl.BlockSpec((1, nblk, STEP_LANES), lambda bi, hb, i: (bi, 0, hb))],
        out_specs=q_spec,
        out_shape=jax.ShapeDtypeStruct((b, s, width), F32),
        scratch_shapes=_attn_scratch(tq),
        compiler_params=_params(("arbitrary", "arbitrary", "arbitrary")),
        name="moba",
    )(q, k, hot, vt, kmean)


def _postattn_kernel(x_ref, fox_ref, moba_ref, gf_ref, gm_ref, wo_ref, gn_ref, wr_ref, br_ref,
                     x2_ref, h2_ref, wts_ref, route_ref, cnt_ref, carry_ref):
    t = pl.program_id(0)
    tm = x_ref.shape[0]
    fw = fox_ref.shape[1]
    mixed_f = _rms(fox_ref[...], gf_ref[...]).astype(BF16)
    mixed_m = _rms(moba_ref[...], gm_ref[...]).astype(BF16)
    x2 = x_ref[...] + _dot(mixed_f, wo_ref[:fw, :]) + _dot(mixed_m, wo_ref[fw:, :])
    x2_ref[...] = x2
    h2 = _rms(x2, gn_ref[...])
    _store_rows(h2_ref, h2)

    h_hi = h2.astype(BF16)
    h_lo = (h2 - h_hi.astype(F32)).astype(BF16)
    by_hi = _dot_nt(wr_ref[...], h_hi)
    logits = (by_hi[:ROUTE_ROWS] + by_hi[LANES:LANES + ROUTE_ROWS]
              + _dot_nt(wr_ref[:ROUTE_ROWS, :], h_lo) + br_ref[...])
    row = lax.broadcasted_iota(I32, (ROUTE_ROWS, tm), 0).astype(F32)

    def first_max(vals):
        mx = jnp.max(vals, axis=0, keepdims=True)
        return mx, jnp.min(jnp.where(vals == mx, row, float(ROUTE_ROWS)), axis=0, keepdims=True)

    gl = jnp.where(row < N_GROUPS, logits, NEG_INF)
    gmax, g_idx = first_max(gl)
    g_top = 1.0 / jnp.sum(jnp.exp(gl - gmax), axis=0, keepdims=True)
    e_lo = N_GROUPS + EXPERTS_PER_GROUP * g_idx
    el = jnp.where(jnp.logical_and(row >= e_lo, row < e_lo + EXPERTS_PER_GROUP), logits, NEG_INF)
    emax, i1 = first_max(el)
    esum = jnp.sum(jnp.exp(el - emax), axis=0, keepdims=True)
    e2max, i2 = first_max(jnp.where(row == i1, NEG_INF, el))
    p1 = 1.0 / esum
    p2 = jnp.exp(e2max - emax) / esum
    w1 = p1 / (p1 + p2) * g_top
    w2 = p2 / (p1 + p2) * g_top
    e1 = i1 - N_GROUPS
    e2 = i2 - N_GROUPS
    zeros = jnp.zeros((SUBLANES - 2, tm), F32)
    wts_ref[...] = jnp.concatenate([w1, w2, zeros], axis=0).T[:, :2]

    @pl.when(t == 0)
    def _():
        carry_ref[...] = jnp.zeros_like(carry_ref)

    earlier = jnp.where(lax.broadcasted_iota(I32, (tm, tm), 0) < lax.broadcasted_iota(I32, (tm, tm), 1),
                        1.0, 0.0).astype(BF16)
    hit1 = row == e1
    hit2 = row == e2
    oh1 = jnp.where(hit1, 1.0, 0.0)
    oh2 = jnp.where(hit2, 1.0, 0.0)
    tot1 = jnp.sum(oh1, axis=1, keepdims=True)
    tot2 = jnp.sum(oh2, axis=1, keepdims=True)
    base = carry_ref[...]
    before = _dot(jnp.concatenate([oh1, oh2], axis=0).astype(BF16), earlier)
    before1 = before[:ROUTE_ROWS] + base
    before2 = before[ROUTE_ROWS:] + (base + tot1)
    r1 = jnp.sum(jnp.where(hit1, before1, 0.0), axis=0, keepdims=True)
    r2 = jnp.sum(jnp.where(hit2, before2, 0.0), axis=0, keepdims=True)
    route_ref[...] = jnp.concatenate([e1, e2, r1, r2, zeros[:SUBLANES - 4]], axis=0).astype(I32)
    total = base + tot1 + tot2
    carry_ref[...] = total
    cnt_ref[...] = total.astype(I32)


def _postattn(x, fox, moba, gf, gm, wo, gn, wr, br):
    t, d = x.shape
    tm = TM_PROJ
    fw = fox.shape[1]
    const = lambda i: (0, 0)
    rows = lambda i: (i, 0)
    return pl.pallas_call(
        _postattn_kernel,
        grid=(t // tm,),
        in_specs=[pl.BlockSpec((tm, d), rows), pl.BlockSpec((tm, fw), rows),
                  pl.BlockSpec((tm, moba.shape[1]), rows),
                  pl.BlockSpec((1, fw), const), pl.BlockSpec((1, moba.shape[1]), const),
                  pl.BlockSpec(wo.shape, const), pl.BlockSpec((1, d), const),
                  pl.BlockSpec(wr.shape, const), pl.BlockSpec((ROUTE_ROWS, 1), const)],
        out_specs=[pl.BlockSpec((tm, d), rows), pl.BlockSpec((tm * SUBLANES, LANES), rows),
                   pl.BlockSpec((tm, 2), rows), pl.BlockSpec((SUBLANES, tm), lambda i: (0, i)),
                   pl.BlockSpec((ROUTE_ROWS, 1), const)],
        out_shape=[jax.ShapeDtypeStruct((t, d), F32), jax.ShapeDtypeStruct((t * SUBLANES, LANES), F32),
                   jax.ShapeDtypeStruct((t, 2), F32), jax.ShapeDtypeStruct((SUBLANES, t), I32),
                   jax.ShapeDtypeStruct((ROUTE_ROWS, 1), I32)],
        scratch_shapes=[pltpu.VMEM((ROUTE_ROWS, 1), F32)],
        compiler_params=_params(("arbitrary",)),
        name="postattn",
    )(x, fox, moba, gf, gm, wo, gn, wr, br)


def _store_rows(ref, val):
    for g in range(SUBLANES):
        ref[pl.ds(g, val.shape[0], stride=SUBLANES), :] = val[:, g * LANES:(g + 1) * LANES]


def _load_rows(ref):
    tokens = ref.shape[0] // SUBLANES
    return jnp.concatenate([ref[pl.ds(g, tokens, stride=SUBLANES), :] for g in range(SUBLANES)], axis=1)


def _row_copy(src, src_row, dst, dst_row, sem, tokens=1):
    window = lambda r: pl.ds(pl.multiple_of(r * SUBLANES, SUBLANES), tokens * SUBLANES)
    return pltpu.make_async_copy(src.at[window(src_row)], dst.at[window(dst_row)], sem)


RING = 3
ZERO_TOKENS = 256


def _dispatch_kernel(pad_start_ref, pad_len_ref, dest_ref, h_ref, xs_ref,
                     ring_ref, zero_ref, fetch_sems, scatter_sems, pad_sem):
    i = pl.program_id(0)
    last = pl.num_programs(0) - 1
    tm = dest_ref.shape[2] // 2
    tile_rows = tm * SUBLANES
    row_copy = _row_copy
    zero_tokens = zero_ref.shape[0] // SUBLANES

    def fetch(tile):
        start = pl.multiple_of(tile * tile_rows, tile_rows)
        slot = lax.rem(tile, RING)
        return pltpu.make_async_copy(h_ref.at[pl.ds(start, tile_rows)], ring_ref.at[slot],
                                     fetch_sems.at[slot])

    def zero_fill(e, act):
        start, n = pad_start_ref[e], pad_len_ref[e]
        whole = lax.shift_right_logical(n, zero_tokens.bit_length() - 1)
        rest = jnp.bitwise_and(n, zero_tokens - 1)

        def chunk(c, carry):
            act(_row_copy(zero_ref, 0, xs_ref, start + c * zero_tokens, pad_sem, tokens=zero_tokens))
            return carry

        lax.fori_loop(0, whole, chunk, 0)
        bit = zero_tokens // 2
        while bit:
            @pl.when(jnp.bitwise_and(rest, bit) != 0)
            def _(bit=bit):
                above = jnp.bitwise_and(rest, -2 * bit)
                act(_row_copy(zero_ref, 0, xs_ref, start + whole * zero_tokens + above, pad_sem, tokens=bit))
            bit //= 2

    @pl.when(i == 0)
    def _():
        fetch(0).start()
        zero_ref[...] = jnp.zeros_like(zero_ref)
        n_regions = pad_start_ref.shape[0]
        lax.fori_loop(0, n_regions, lambda e, c: (zero_fill(e, lambda cp: cp.start()), c)[1], 0)
        lax.fori_loop(0, n_regions, lambda e, c: (zero_fill(e, lambda cp: cp.wait()), c)[1], 0)

    @pl.when(i < last)
    def _():
        fetch(i + 1).start()

    fetch(i).wait()
    src = ring_ref.at[lax.rem(i, RING)]

    def issue(r, c):
        for k in range(2):
            row_copy(src, r, xs_ref, dest_ref[0, 0, k * tm + r], scatter_sems.at[i % 2]).start(priority=k)
        return c

    lax.fori_loop(0, tm, issue, 0, unroll=ISSUE_UNROLL)

    def drain(parity):
        for _ in range(2):
            pltpu.make_async_copy(src, src, scatter_sems.at[parity]).wait()

    @pl.when(i > 0)
    def _():
        drain((i - 1) % 2)

    @pl.when(i == last)
    def _():
        drain(i % 2)


def _dispatch(pad_start, pad_len, dest3, h2, n_rows):
    tm = dest3.shape[2] // 2
    grid_spec = pltpu.PrefetchScalarGridSpec(
        num_scalar_prefetch=2,
        grid=(dest3.shape[0],),
        in_specs=[pl.BlockSpec((1, 1, 2 * tm), lambda i, ps, pn: (i, 0, 0), memory_space=pltpu.SMEM),
                  pl.BlockSpec(memory_space=pl.ANY)],
        out_specs=pl.BlockSpec(memory_space=pl.ANY),
        scratch_shapes=[pltpu.VMEM((RING, tm * SUBLANES, LANES), F32),
                        pltpu.VMEM((ZERO_TOKENS * SUBLANES, LANES), F32),
                        pltpu.SemaphoreType.DMA((RING,)), pltpu.SemaphoreType.DMA((2,)),
                        pltpu.SemaphoreType.DMA(())],
    )
    return pl.pallas_call(
        _dispatch_kernel,
        grid_spec=grid_spec,
        out_shape=jax.ShapeDtypeStruct((n_rows * SUBLANES, LANES), F32),
        compiler_params=_params(("arbitrary",)),
        name="dispatch",
    )(pad_start, pad_len, dest3, h2)


def _experts_kernel(te_ref, ts_ref, nv_ref, xs_ref, wg_ref, wu_ref, wd_ref, ys_ref,
                    wgb_ref, wub_ref, wdb_ref):
    del ts_ref
    t = pl.program_id(0)

    @pl.when(jnp.logical_or(t == 0, te_ref[t] != te_ref[jnp.maximum(t - 1, 0)]))
    def _():
        wgb_ref[...] = wg_ref[0].astype(BF16)
        wub_ref[...] = wu_ref[0].astype(BF16)
        wdb_ref[...] = wd_ref[0].astype(BF16)

    @pl.when(t < nv_ref[0])
    def _():
        xb = _load_rows(xs_ref).astype(BF16)
        a = _dot(xb, wgb_ref[...])
        u = _dot(xb, wub_ref[...])
        act = (a * jax.nn.sigmoid(a) * u).astype(BF16)
        _store_rows(ys_ref, _dot(act, wdb_ref[...]))

    @pl.when(t >= nv_ref[0])
    def _():
        ys_ref[...] = jnp.zeros_like(ys_ref)


def _experts(tile_expert, tile_src, n_valid, xs, wg, wu, wd):
    tm = TM_EXPERT
    n_tiles = xs.shape[0] // (tm * SUBLANES)
    _, d, f = wg.shape
    row_block = (tm * SUBLANES, LANES)
    grid_spec = pltpu.PrefetchScalarGridSpec(
        num_scalar_prefetch=3,
        grid=(n_tiles,),
        in_specs=[pl.BlockSpec(row_block, lambda t, te, ts, nv: (ts[t], 0)),
                  pl.BlockSpec((1, d, f), lambda t, te, ts, nv: (te[t], 0, 0)),
                  pl.BlockSpec((1, d, f), lambda t, te, ts, nv: (te[t], 0, 0)),
                  pl.BlockSpec((1, f, d), lambda t, te, ts, nv: (te[t], 0, 0))],
        out_specs=pl.BlockSpec(row_block, lambda t, te, ts, nv: (t, 0)),
        scratch_shapes=[pltpu.VMEM((d, f), BF16), pltpu.VMEM((d, f), BF16), pltpu.VMEM((f, d), BF16)],
    )
    return pl.pallas_call(
        _experts_kernel,
        grid_spec=grid_spec,
        out_shape=jax.ShapeDtypeStruct((n_tiles * tm * SUBLANES, LANES), F32),
        compiler_params=_params(("arbitrary",)),
        name="experts",
    )(tile_expert, tile_src, n_valid, xs, wg, wu, wd)


def _combine_kernel(dest_ref, next_ref, x2_ref, wts_ref, g_ref, ys_ref, o_ref, buf_ref, sems):
    i = pl.program_id(0)
    tm = dest_ref.shape[2] // 2
    slot = i % 2

    def gather(d_ref, to):
        def issue(r, c):
            for k in range(2):
                _row_copy(ys_ref, d_ref[0, 0, k * tm + r], buf_ref.at[to, k], r,
                          sems.at[to]).start(priority=k)
            return c

        lax.fori_loop(0, tm, issue, 0, unroll=ISSUE_UNROLL)

    @pl.when(i == 0)
    def _():
        gather(dest_ref, 0)

    @pl.when(i + 1 < pl.num_programs(0))
    def _():
        gather(next_ref, 1 - slot)

    pltpu.make_async_copy(buf_ref.at[slot], buf_ref.at[slot], sems.at[slot]).wait()
    w = wts_ref[...]
    y = (x2_ref[...] + w[:, 0:1] * _load_rows(buf_ref.at[slot, 0])
         + w[:, 1:2] * _load_rows(buf_ref.at[slot, 1]))
    o_ref[...] = _rms(y, g_ref[...])


def _combine(dest3, x2, wts, g, ys):
    t, d = x2.shape
    tm = dest3.shape[2] // 2
    rows = lambda i: (i, 0)
    n = t // tm
    return pl.pallas_call(
        _combine_kernel,
        grid=(n,),
        in_specs=[pl.BlockSpec((1, 1, 2 * tm), lambda i: (i, 0, 0), memory_space=pltpu.SMEM),
                  pl.BlockSpec((1, 1, 2 * tm), lambda i: (jnp.minimum(i + 1, n - 1), 0, 0),
                               memory_space=pltpu.SMEM),
                  pl.BlockSpec((tm, d), rows), pl.BlockSpec((tm, 2), rows),
                  pl.BlockSpec((1, d), lambda i: (0, 0)),
                  pl.BlockSpec(memory_space=pl.ANY)],
        out_specs=pl.BlockSpec((tm, d), rows),
        out_shape=jax.ShapeDtypeStruct((t, d), F32),
        scratch_shapes=[pltpu.VMEM((2, 2, tm * SUBLANES, LANES), F32), pltpu.SemaphoreType.DMA((2,))],
        compiler_params=_params(("arbitrary",)),
        name="combine",
    )(dest3, dest3, x2, wts, g, ys)


def _rotary_tables(seq):
    half = ROPE_DIM // 2
    inv_freq = ROPE_THETA ** (-jnp.arange(half, dtype=F32) / half)
    ang = jnp.arange(seq, dtype=F32)[:, None] * inv_freq[None, :]
    cos, sin = jnp.cos(ang), jnp.sin(ang)
    ones = jnp.ones((seq, HEAD_DIM - ROPE_DIM), F32)
    zeros = jnp.zeros((seq, HEAD_DIM - ROPE_DIM), F32)
    zh = jnp.zeros((seq, half), F32)
    cosm = jnp.concatenate([cos, cos, ones], axis=1)
    sina = jnp.concatenate([-sin, zh, zeros], axis=1)
    sinb = jnp.concatenate([zh, sin, zeros], axis=1)
    tile = lambda a: jnp.tile(a, (1, HEADS_PER_BLOCK))
    return tile(cosm), tile(sina), tile(sinb)


def _pad_lanes(a):
    return jnp.pad(a, ((0, 0), (0, LANES - a.shape[1])))


def kernel(x, norm_mix_g, w_in, b_forget, fox_out_g, moba_out_g, w_out, norm_ffn_g, w_router_group,
           b_router_group, w_router_expert, b_router_expert, w_gate, w_up, w_down, norm_final_g):
    b, s, d = x.shape
    t = b * s
    assert w_in.shape[0] == 1, "the closing RMSNorm is fused into the only layer's combine step"
    cosm, sina, sinb = _rotary_tables(s)
    n_tiles = (2 * t) // TM_EXPERT + N_EXPERTS
    fw3 = 3 * FOX_WIDTH
    m0 = fw3 + N_FOX_HEADS
    wl = w_in[0]
    aux_head = jnp.arange(AUX_PER_HEAD * N_FOX_HEADS) // AUX_PER_HEAD
    w_main = jnp.concatenate([wl[:, :2 * FOX_WIDTH], wl[:, m0:m0 + 2 * MOBA_WIDTH],
                              _pad_lanes(wl[:, fw3:m0][:, aux_head])],
                             axis=1).astype(BF16)
    w_vt = jnp.stack([wl[:, 2 * FOX_WIDTH:fw3].T, wl[:, m0 + 2 * MOBA_WIDTH:].T]).astype(BF16)
    b_logit = _pad_lanes(b_forget[0][None, aux_head])
    fq, fqa, fk, fka, fvt, mq, mk, mvt, kmean = _inproj(
        x, norm_mix_g[0][None, :], w_main, w_vt, b_logit, cosm, sina, sinb)
    fox = _fox(fq, fqa, fk, fka, fvt)
    moba = _moba(mq, mk, mvt, kmean)

    w_router = _pad_lanes(jnp.concatenate(
        [w_router_group[0], w_router_expert[0].reshape(d, N_EXPERTS)], axis=1))
    w_router_hi = w_router.astype(BF16)
    w_router = jnp.concatenate([w_router_hi, (w_router - w_router_hi.astype(F32)).astype(BF16)], axis=1).T
    b_router = _pad_lanes(jnp.concatenate(
        [b_router_group[0], b_router_expert[0].reshape(N_EXPERTS)])[None, :])[:, :ROUTE_ROWS].T
    x2, h2, wts, route, counts = _postattn(
        x.reshape(t, d), fox.reshape(t, FOX_WIDTH), moba.reshape(t, MOBA_WIDTH),
        fox_out_g[0][None, :], moba_out_g[0][None, :], w_out[0].astype(BF16),
        norm_ffn_g[0][None, :], w_router, b_router)

    counts = counts[:N_EXPERTS, 0]
    padded = (counts + TM_EXPERT - 1) // TM_EXPERT * TM_EXPERT
    ends = jnp.cumsum(padded)
    starts = ends - padded
    expert_ids = jnp.arange(N_EXPERTS, dtype=I32)[:, None, None]
    dest = route[2:4] + jnp.sum(jnp.where(route[None, 0:2] == expert_ids, starts[:, None, None], 0),
                                axis=0)
    by_tile = lambda tm: dest.reshape(2, t // tm, tm).transpose(1, 0, 2).reshape(t // tm, 1, 2 * tm)
    n_rows = n_tiles * TM_EXPERT
    pad_start = jnp.concatenate([starts + counts, ends[-1:]]).astype(I32)
    pad_len = jnp.concatenate([padded - counts, n_rows - ends[-1:]]).astype(I32)
    n_valid = ends[-1] // TM_EXPERT
    tile_src = jnp.minimum(jnp.arange(n_tiles, dtype=I32), n_valid - 1)
    tile_expert = jnp.sum(ends[None, :] <= (tile_src * TM_EXPERT)[:, None], axis=1).astype(I32)
    xs = _dispatch(pad_start, pad_len, by_tile(TM_SCATTER), h2, n_rows)
    ys = _experts(tile_expert, tile_src, n_valid.reshape(1).astype(I32), xs,
                  w_gate[0], w_up[0], w_down[0])
    return _combine(by_tile(TM_GATHER), x2, wts, norm_final_g[None, :], ys).reshape(b, s, d)
```
